```python
import math
import jax
import jax.numpy as jnp
from jax import lax
import numpy as np

D_MODEL = 1024
BATCH = 4
SEQ = 8192
DEPTH = 4

CHUNK = 64
N_MEM = 256
N_MIXERS = 3
EPS = 1e-6

S5_WIDTH = D_MODEL
S5_GROUP = 16
S5_GROUPS = S5_WIDTH // S5_GROUP
S5_STATE = 64
S5_DT_MIN = 1e-3
S5_DT_MAX = 1e-1

ATT_HEADS = 16
ATT_HEAD_DIM = D_MODEL // ATT_HEADS
LEFT_CHUNKS = 8
BAND = (LEFT_CHUNKS + 1) * CHUNK
MAX_REL = 128

LRU_WIDTH = D_MODEL
LRU_BLOCKS = 8
LRU_BLOCK_W = LRU_WIDTH // LRU_BLOCKS
CONV_W = 4
LRU_C = 8.0

MEM_HEADS = 4
MEM_HEAD_DIM = D_MODEL // MEM_HEADS

FF_DENSE = 2816
N_EXPERTS = 8
TOP_K = 2
FF_EXPERT = 2048

N_A = (DEPTH + 2) // 3
N_B = (DEPTH + 1) // 3
N_C = DEPTH // 3
N_DENSE = (DEPTH + 1) // 2
N_MOE = DEPTH // 2

kernel_name = "hybrid_s5_chunkattn_rglru_moe_encoder"


def rmsnorm(x, g):
    x32 = x.astype(jnp.float32)
    y = x32 * lax.rsqrt(jnp.mean(x32 * x32, axis=-1, keepdims=True) + EPS)
    return (y * g.astype(jnp.float32)).astype(x.dtype)


def _complex_combine(e1, e2):
    a1r, a1i, b1r, b1i = e1
    a2r, a2i, b2r, b2i = e2
    return (a1r * a2r - a1i * a2i,
            a1r * a2i + a1i * a2r,
            a2r * b1r - a2i * b1i + b2r,
            a2r * b1i + a2i * b1r + b2i)


def _linear_combine(e1, e2):
    a1, b1 = e1
    a2, b2 = e2
    return a1 * a2, a2 * b1 + b2


def s5_mixer(h, w_in, lam_re, lam_im, log_dt, b_re, b_im, c_re, c_im, d_skip, w_glu):
    bsz, seq, _ = h.shape
    n_chunks = seq // CHUNK
    f32 = jnp.float32
    u = h @ w_in
    lr = lam_re.astype(f32)
    li = lam_im.astype(f32)
    dt = jnp.exp(log_dt.astype(f32))[:, None]
    mag = jnp.exp(lr * dt)
    ar = mag * jnp.cos(li * dt)
    ai = mag * jnp.sin(li * dt)
    den = lr * lr + li * li
    fr = ((ar - 1.0) * lr + ai * li) / den
    fi = (ai * lr - (ar - 1.0) * li) / den
    br = b_re.astype(f32)
    bi = b_im.astype(f32)
    bbr = fr[..., None] * br - fi[..., None] * bi
    bbi = fr[..., None] * bi + fi[..., None] * br
    cr = c_re.astype(f32)
    ci = c_im.astype(f32)
    a_r = jnp.broadcast_to(ar, (CHUNK, bsz, S5_GROUPS, S5_STATE))
    a_i = jnp.broadcast_to(ai, (CHUNK, bsz, S5_GROUPS, S5_STATE))
    u_blocks = u.astype(f32).reshape(bsz, n_chunks, CHUNK, S5_GROUPS, S5_GROUP).transpose(1, 2, 0, 3, 4)

    def step(carry, u_blk):
        h0r, h0i = carry
        xr = jnp.einsum('tbgp,gnp->tbgn', u_blk, bbr)
        xi = jnp.einsum('tbgp,gnp->tbgn', u_blk, bbi)
        pw_r, pw_i, sr, si = lax.associative_scan(_complex_combine, (a_r, a_i, xr, xi), axis=0)
        hr = pw_r * h0r - pw_i * h0i + sr
        hi = pw_r * h0i + pw_i * h0r + si
        y = jnp.einsum('gpn,tbgn->tbgp', cr, hr) - jnp.einsum('gpn,tbgn->tbgp', ci, hi)
        return (hr[-1], hi[-1]), y

    h0 = jnp.zeros((bsz, S5_GROUPS, S5_STATE), f32)
    _, ys = lax.scan(step, (h0, h0), u_blocks)
    y = ys.transpose(2, 0, 1, 3, 4).reshape(bsz, seq, S5_WIDTH).astype(h.dtype)
    y = y + d_skip * u
    v = jax.nn.gelu(y)
    z = v @ w_glu
    return z[..., :D_MODEL] * jax.nn.sigmoid(z[..., D_MODEL:])


def chunk_attention(h, w_qkv, w_o, rel_bias):
    bsz, seq, _ = h.shape
    n_chunks = seq // CHUNK
    f32 = jnp.float32
    pad = LEFT_CHUNKS * CHUNK
    qkv = (h @ w_qkv).reshape(bsz, seq, 3, ATT_HEADS, ATT_HEAD_DIM)
    q, k, v = qkv[:, :, 0], qkv[:, :, 1], qkv[:, :, 2]
    k = jnp.pad(k, ((0, 0), (pad, 0), (0, 0), (0, 0)))
    v = jnp.pad(v, ((0, 0), (pad, 0), (0, 0), (0, 0)))
    q_c = q.reshape(bsz, n_chunks, CHUNK, ATT_HEADS, ATT_HEAD_DIM).transpose(1, 0, 2, 3, 4)
    q_pos = jnp.arange(CHUNK)[:, None] + pad
    k_pos = jnp.arange(BAND)[None, :]
    rel = jnp.clip(q_pos - k_pos, -MAX_REL, MAX_REL) + MAX_REL
    bias = rel_bias[:, rel].astype(f32)
    scale = ATT_HEAD_DIM ** -0.5

    def one_chunk(args):
        c, q_blk = args
        start = c * CHUNK
        k_band = lax.dynamic_slice_in_dim(k, start, BAND, axis=1)
        v_band = lax.dynamic_slice_in_dim(v, start, BAND, axis=1)
        s = jnp.einsum('bqhd,bkhd->bhqk', q_blk, k_band).astype(f32) * scale + bias
        valid = (start + jnp.arange(BAND)) >= pad
        s = jnp.where(valid[None, None, None, :], s, -1e30)
        p = jax.nn.softmax(s, axis=-1).astype(v_band.dtype)
        return jnp.einsum('bhqk,bkhd->bqhd', p, v_band)

    o = lax.map(one_chunk, (jnp.arange(n_chunks), q_c))
    o = o.transpose(1, 0, 2, 3, 4).reshape(bsz, seq, D_MODEL)
    return o @ w_o


def rglru_mixer(h, w_in, conv_w, conv_b, w_a, b_a, w_x, b_x, lam, w_out):
    bsz, seq, _ = h.shape
    f32 = jnp.float32
    z = h @ w_in
    gate, xr = z[..., :LRU_WIDTH], z[..., LRU_WIDTH:]
    xc = lax.conv_general_dilated(
        xr, conv_w.astype(xr.dtype)[:, None, :], window_strides=(1,),
        padding=[(CONV_W - 1, 0)], dimension_numbers=('NWC', 'WIO', 'NWC'),
        feature_group_count=LRU_WIDTH) + conv_b
    xb = xc.reshape(bsz, seq, LRU_BLOCKS, LRU_BLOCK_W)
    r = jax.nn.sigmoid(jnp.einsum('blhi,hij->blhj', xb, w_a).reshape(bsz, seq, LRU_WIDTH) + b_a)
    i = jax.nn.sigmoid(jnp.einsum('blhi,hij->blhj', xb, w_x).reshape(bsz, seq, LRU_WIDTH) + b_x)
    log_a = -LRU_C * r.astype(f32) * jax.nn.softplus(-lam.astype(f32))
    a = jnp.exp(log_a)
    b = jnp.sqrt(-jnp.expm1(2.0 * log_a)) * (i * xc).astype(f32)
    _, hs = lax.associative_scan(_linear_combine, (a, b), axis=1)
    y = jax.nn.gelu(gate) * hs.astype(h.dtype)
    return y @ w_out


def memory_attention(h, m, w_q, w_kv, w_o):
    bsz, seq, _ = h.shape
    q = (h @ w_q).reshape(bsz, seq, MEM_HEADS, MEM_HEAD_DIM)
    kv = (m @ w_kv).reshape(bsz, m.shape[1], 2, MEM_HEADS, MEM_HEAD_DIM)
    s = jnp.einsum('blhd,bmhd->bhlm', q, kv[:, :, 0]).astype(jnp.float32) * (MEM_HEAD_DIM ** -0.5)
    p = jax.nn.softmax(s, axis=-1).astype(h.dtype)
    o = jnp.einsum('bhlm,bmhd->blhd', p, kv[:, :, 1]).reshape(bsz, seq, D_MODEL)
    return o @ w_o


def swiglu(h, w_gu, w_down):
    f = w_down.shape[0]
    gu = h @ w_gu
    return (jax.nn.silu(gu[..., :f]) * gu[..., f:]) @ w_down


def moe_swiglu(h, w_router, w_gu, w_down):
    bsz, seq, d = h.shape
    t = h.reshape(-1, d)
    logits = (t @ w_router).astype(jnp.float32)
    top_val, top_idx = lax.top_k(logits, TOP_K)
    gates = jax.nn.softmax(top_val, axis=-1)
    combine = jnp.sum(jax.nn.one_hot(top_idx, N_EXPERTS, dtype=jnp.float32) * gates[..., None], axis=1)
    combine = combine.astype(t.dtype)
    out = jnp.zeros_like(t)
    for e in range(N_EXPERTS):
        out = out + combine[:, e:e + 1] * swiglu(t, w_gu[e], w_down[e])
    return out.reshape(bsz, seq, d)


def setup_inputs(seed: int = 0) -> dict:
    key = jax.random.key(seed)
    keys = jax.random.split(key, 64)
    counter = [0]

    def nk():
        k = keys[counter[0]]
        counter[0] += 1
        return k

    def nrm(shape, scale):
        return jax.random.normal(nk(), shape, jnp.float32) * scale

    def gain(shape):
        return 1.0 + 0.02 * jax.random.normal(nk(), shape, jnp.float32)

    d = D_MODEL
    x = nrm((BATCH, SEQ, d), 1.0)
    mem = nrm((BATCH, N_MEM, d), 1.0)
    norm_mix = gain((DEPTH, d))
    norm_mem = gain((DEPTH, d))
    norm_ffn = gain((DEPTH, d))
    mem_norm = gain((d,))
    final_norm = gain((d,))

    s5_w_in = nrm((N_A, d, S5_WIDTH), d ** -0.5)
    s5_lam_re = -0.5 + nrm((N_A, S5_GROUPS, S5_STATE), 0.01)
    s5_lam_im = math.pi * jnp.arange(S5_STATE, dtype=jnp.float32) + nrm((N_A, S5_GROUPS, S5_STATE), 0.01)
    s5_log_dt = jax.random.uniform(nk(), (N_A, S5_GROUPS), jnp.float32,
                                   math.log(S5_DT_MIN), math.log(S5_DT_MAX))
    s5_b_re = nrm((N_A, S5_GROUPS, S5_STATE, S5_GROUP), (2.0 * S5_GROUP) ** -0.5)
    s5_b_im = nrm((N_A, S5_GROUPS, S5_STATE, S5_GROUP), (2.0 * S5_GROUP) ** -0.5)
    s5_c_re = nrm((N_A, S5_GROUPS, S5_GROUP, S5_STATE), (2.0 / S5_STATE) ** 0.5)
    s5_c_im = nrm((N_A, S5_GROUPS, S5_GROUP, S5_STATE), (2.0 / S5_STATE) ** 0.5)
    s5_d = nrm((N_A, S5_WIDTH), 1.0)
    s5_w_glu = nrm((N_A, S5_WIDTH, 2 * d), S5_WIDTH ** -0.5)

    att_w_qkv = nrm((N_B, d, 3 * d), d ** -0.5)
    att_w_o = nrm((N_B, d, d), d ** -0.5)
    att_rel_bias = nrm((N_B, ATT_HEADS, 2 * MAX_REL + 1), 0.2)

    lru_w_in = nrm((N_C, d, 2 * LRU_WIDTH), d ** -0.5)
    lru_conv_w = nrm((N_C, CONV_W, LRU_WIDTH), CONV_W ** -0.5)
    lru_conv_b = nrm((N_C, LRU_WIDTH), 0.01)
    lru_w_a = nrm((N_C, LRU_BLOCKS, LRU_BLOCK_W, LRU_BLOCK_W), LRU_BLOCK_W ** -0.5)
    lru_b_a = nrm((N_C, LRU_WIDTH), 0.01)
    lru_w_x = nrm((N_C, LRU_BLOCKS, LRU_BLOCK_W, LRU_BLOCK_W), LRU_BLOCK_W ** -0.5)
    lru_b_x = nrm((N_C, LRU_WIDTH), 0.01)
    a_c = jax.random.uniform(nk(), (N_C, LRU_WIDTH), jnp.float32, 0.9, 0.999)
    p_base = a_c ** (1.0 / LRU_C)
    lru_lam = jnp.log(p_base) - jnp.log1p(-p_base)
    lru_w_out = nrm((N_C, LRU_WIDTH, d), LRU_WIDTH ** -0.5)

    mem_w_q = nrm((DEPTH, d, d), d ** -0.5)
    mem_w_kv = nrm((DEPTH, d, 2 * d), d ** -0.5)
    mem_w_o = nrm((DEPTH, d, d), d ** -0.5)

    ffn_w_gu = nrm((N_DENSE, d, 2 * FF_DENSE), d ** -0.5)
    ffn_w_down = nrm((N_DENSE, FF_DENSE, d), FF_DENSE ** -0.5)

    moe_w_router = nrm((N_MOE, d, N_EXPERTS), d ** -0.5)
    moe_w_gu = nrm((N_MOE, N_EXPERTS, d, 2 * FF_EXPERT), d ** -0.5)
    moe_w_down = nrm((N_MOE, N_EXPERTS, FF_EXPERT, d), FF_EXPERT ** -0.5)

    return {
        "x": x, "mem": mem,
        "norm_mix": norm_mix, "norm_mem": norm_mem, "norm_ffn": norm_ffn,
        "mem_norm": mem_norm, "final_norm": final_norm,
        "s5_w_in": s5_w_in, "s5_lam_re": s5_lam_re, "s5_lam_im": s5_lam_im, "s5_log_dt": s5_log_dt,
        "s5_b_re": s5_b_re, "s5_b_im": s5_b_im, "s5_c_re": s5_c_re, "s5_c_im": s5_c_im,
        "s5_d": s5_d, "s5_w_glu": s5_w_glu,
        "att_w_qkv": att_w_qkv, "att_w_o": att_w_o, "att_rel_bias": att_rel_bias,
        "lru_w_in": lru_w_in, "lru_conv_w": lru_conv_w, "lru_conv_b": lru_conv_b,
        "lru_w_a": lru_w_a, "lru_b_a": lru_b_a, "lru_w_x": lru_w_x, "lru_b_x": lru_b_x,
        "lru_lam": lru_lam, "lru_w_out": lru_w_out,
        "mem_w_q": mem_w_q, "mem_w_kv": mem_w_kv, "mem_w_o": mem_w_o,
        "ffn_w_gu": ffn_w_gu, "ffn_w_down": ffn_w_down,
        "moe_w_router": moe_w_router, "moe_w_gu": moe_w_gu, "moe_w_down": moe_w_down,
    }


def reference(x, mem, norm_mix, norm_mem, norm_ffn, mem_norm, final_norm,
              s5_w_in, s5_lam_re, s5_lam_im, s5_log_dt, s5_b_re, s5_b_im, s5_c_re, s5_c_im,
              s5_d, s5_w_glu,
              att_w_qkv, att_w_o, att_rel_bias,
              lru_w_in, lru_conv_w, lru_conv_b, lru_w_a, lru_b_a, lru_w_x, lru_b_x,
              lru_lam, lru_w_out,
              mem_w_q, mem_w_kv, mem_w_o,
              ffn_w_gu, ffn_w_down,
              moe_w_router, moe_w_gu, moe_w_down):
    m = rmsnorm(mem, mem_norm)
    for i in range(DEPTH):
        kind, j = i % N_MIXERS, i // N_MIXERS
        hn = rmsnorm(x, norm_mix[i])
        if kind == 0:
            y = s5_mixer(hn, s5_w_in[j], s5_lam_re[j], s5_lam_im[j], s5_log_dt[j],
                         s5_b_re[j], s5_b_im[j], s5_c_re[j], s5_c_im[j], s5_d[j], s5_w_glu[j])
        elif kind == 1:
            y = chunk_attention(hn, att_w_qkv[j], att_w_o[j], att_rel_bias[j])
        else:
            y = rglru_mixer(hn, lru_w_in[j], lru_conv_w[j], lru_conv_b[j], lru_w_a[j], lru_b_a[j],
                            lru_w_x[j], lru_b_x[j], lru_lam[j], lru_w_out[j])
        x = x + y
        x = x + memory_attention(rmsnorm(x, norm_mem[i]), m, mem_w_q[i], mem_w_kv[i], mem_w_o[i])
        hn = rmsnorm(x, norm_ffn[i])
        if i % 2 == 0:
            y = swiglu(hn, ffn_w_gu[i // 2], ffn_w_down[i // 2])
        else:
            y = moe_swiglu(hn, moe_w_router[i // 2], moe_w_gu[i // 2], moe_w_down[i // 2])
        x = x + y
    return rmsnorm(x, final_norm)
```

```python
import functools
import math

import jax
import jax.numpy as jnp
from jax import lax
from jax.experimental import pallas as pl
from jax.experimental.pallas import tpu as pltpu

F32 = jnp.float32
BF16 = jnp.bfloat16
HIGHEST = lax.Precision.HIGHEST

D_MODEL = 1024
DEPTH = 4
CHUNK = 64
N_MEM = 256
EPS = 1e-6

S5_GROUP = 16
S5_GROUPS = D_MODEL // S5_GROUP
S5_STATE = 64

ATT_HEADS = 16
ATT_HEAD_DIM = D_MODEL // ATT_HEADS
LEFT_CHUNKS = 8
MAX_REL = 128
ATT_TQ = LEFT_CHUNKS * CHUNK
NEG_INF = -1e30

LRU_BLOCKS = 8
LRU_BLOCK_W = D_MODEL // LRU_BLOCKS
CONV_W = 4
LRU_C = 8.0
LRU_T = 512
SUBLANES = 8
LANES = 128

MEM_HEADS = 4
MEM_HEAD_DIM = D_MODEL // MEM_HEADS

N_EXPERTS = 8

VMEM_LIMIT = 56 * 1024 * 1024


def _params(*sem):
    return pltpu.CompilerParams(dimension_semantics=sem, vmem_limit_bytes=VMEM_LIMIT)


def _rms(x, g):
    ms = jnp.mean(x * x, axis=-1, keepdims=True)
    return x * lax.rsqrt(ms + EPS) * g


def _norm_linear_kernel(x_ref, g_ref, w_ref, o_ref, hn_ref):
    @pl.when(pl.program_id(1) == 0)
    def _():
        hn_ref[...] = _rms(x_ref[...], g_ref[...]).astype(BF16)

    o_ref[...] = jnp.dot(hn_ref[...], w_ref[...], preferred_element_type=F32).astype(o_ref.dtype)


def _norm_linear(x, g, w, out_dtype, tm=512, tn=None):
    m, d = x.shape
    n = w.shape[1]
    tn = n if tn is None else tn
    return pl.pallas_call(
        _norm_linear_kernel,
        grid=(m // tm, n // tn),
        in_specs=[
            pl.BlockSpec((tm, d), lambda i, j: (i, 0)),
            pl.BlockSpec((1, d), lambda i, j: (0, 0)),
            pl.BlockSpec((d, tn), lambda i, j: (0, j)),
        ],
        out_specs=pl.BlockSpec((tm, tn), lambda i, j: (i, j)),
        out_shape=jax.ShapeDtypeStruct((m, n), out_dtype),
        scratch_shapes=[pltpu.VMEM((tm, d), BF16)],
        compiler_params=_params("parallel", "arbitrary"),
        name="norm_linear",
    )(x, g.reshape(1, d), w)


def _norm_linear_t_kernel(x_ref, g_ref, wt_ref, o_ref):
    hn = _rms(x_ref[...], g_ref[...]).astype(BF16)
    o_ref[...] = lax.dot_general(
        wt_ref[...], hn, (((1,), (1,)), ((), ())), preferred_element_type=F32
    ).astype(o_ref.dtype)


def _norm_linear_t(x, g, wt, out_dtype, tm=512):
    m, d = x.shape
    n = wt.shape[0]
    return pl.pallas_call(
        _norm_linear_t_kernel,
        grid=(m // tm,),
        in_specs=[
            pl.BlockSpec((tm, d), lambda i: (i, 0)),
            pl.BlockSpec((1, d), lambda i: (0, 0)),
            pl.BlockSpec((n, d), lambda i: (0, 0)),
        ],
        out_specs=pl.BlockSpec((n, tm), lambda i: (0, i)),
        out_shape=jax.ShapeDtypeStruct((n, m), out_dtype),
        compiler_params=_params("parallel"),
        name="norm_linear_t",
    )(x, g.reshape(1, d), wt)


def _linear_residual_kernel(x_ref, y_ref, w_ref, o_ref):
    o_ref[...] = x_ref[...] + jnp.dot(y_ref[...], w_ref[...], preferred_element_type=F32)


def _linear_residual(x, y, w, tm=1024):
    m, d = x.shape
    k = y.shape[1]
    return pl.pallas_call(
        _linear_residual_kernel,
        grid=(m // tm,),
        in_specs=[
            pl.BlockSpec((tm, d), lambda i: (i, 0)),
            pl.BlockSpec((tm, k), lambda i: (i, 0)),
            pl.BlockSpec((k, d), lambda i: (0, 0)),
        ],
        out_specs=pl.BlockSpec((tm, d), lambda i: (i, 0)),
        out_shape=jax.ShapeDtypeStruct((m, d), F32),
        compiler_params=_params("parallel"),
        name="linear_residual",
    )(x, y, w)


def _norm_kernel(x_ref, g_ref, o_ref):
    o_ref[...] = _rms(x_ref[...], g_ref[...])


def _norm(x, g, tm=1024):
    m, d = x.shape
    return pl.pallas_call(
        _norm_kernel,
        grid=(m // tm,),
        in_specs=[pl.BlockSpec((tm, d), lambda i: (i, 0)), pl.BlockSpec((1, d), lambda i: (0, 0))],
        out_specs=pl.BlockSpec((tm, d), lambda i: (i, 0)),
        out_shape=jax.ShapeDtypeStruct((m, d), F32),
        compiler_params=_params("parallel"),
        name="final_norm",
    )(x, g.reshape(1, d))


def _s5_operators(lam_re, lam_im, log_dt, b_re, b_im, c_re, c_im, d_skip, n_chunks):
    g, n, p = S5_GROUPS, S5_STATE, S5_GROUP
    lr = lam_re.astype(F32)
    li = lam_im.astype(F32)
    dt = jnp.exp(log_dt.astype(F32))[:, None]
    mag = jnp.exp(lr * dt)
    ar = mag * jnp.cos(li * dt)
    ai = mag * jnp.sin(li * dt)
    den = lr * lr + li * li
    fr = ((ar - 1.0) * lr + ai * li) / den
    fi = (ai * lr - (ar - 1.0) * li) / den
    bbr = fr[..., None] * b_re - fi[..., None] * b_im
    bbi = fr[..., None] * b_im + fi[..., None] * b_re

    def powers(ks):
        kk = ks.astype(F32)[:, None, None]
        m = jnp.exp(kk * (lr * dt))
        return m * jnp.cos(kk * (li * dt)), m * jnp.sin(kk * (li * dt))

    pr, pi = powers(jnp.arange(CHUNK + 1))
    lcr = c_re[None] * pr[:, :, None, :] - c_im[None] * pi[:, :, None, :]
    lci = c_re[None] * pi[:, :, None, :] + c_im[None] * pr[:, :, None, :]
    kern = (jnp.einsum('tgqn,gnp->gtqp', lcr[:CHUNK], bbr, precision=HIGHEST)
            - jnp.einsum('tgqn,gnp->gtqp', lci[:CHUNK], bbi, precision=HIGHEST))
    t_idx = jnp.arange(CHUNK)
    lag = t_idx[:, None] - t_idx[None, :]
    toe = jnp.take(kern, jnp.clip(lag, 0, CHUNK - 1).reshape(-1), axis=1)
    toe = toe.reshape(g, CHUNK, CHUNK, p, p)
    toe = jnp.where((lag >= 0)[None, :, :, None, None], toe, 0.0)
    toe = toe.transpose(0, 3, 1, 4, 2)
    eye = (jnp.eye(p, dtype=F32)[:, None, :, None] * jnp.eye(CHUNK, dtype=F32)[None, :, None, :])
    toe = toe + d_skip.reshape(g, p)[:, :, None, None, None] * eye[None]
    toe = toe.reshape(g, p * CHUNK, p * CHUNK).astype(BF16)
    rev_r = pr[CHUNK - 1 - t_idx]
    rev_i = pi[CHUNK - 1 - t_idx]
    bm_r = jnp.einsum('sgn,gnp->gnps', rev_r, bbr) - jnp.einsum('sgn,gnp->gnps', rev_i, bbi)
    bm_i = jnp.einsum('sgn,gnp->gnps', rev_r, bbi) + jnp.einsum('sgn,gnp->gnps', rev_i, bbr)
    bm = jnp.concatenate([bm_r, bm_i], axis=1).reshape(g, 2 * n, p * CHUNK).astype(BF16)
    cinj = jnp.concatenate([lcr[1:], -lci[1:]], axis=-1)
    cinj = cinj.transpose(1, 2, 0, 3).reshape(g, p * CHUNK, 2 * n).astype(BF16)
    qr, qi = [pr[CHUNK]], [pi[CHUNK]]
    for _ in range(int(math.log2(n_chunks)) - 1):
        qr, qi = qr + [qr[-1] * qr[-1] - qi[-1] * qi[-1]], qi + [2.0 * qr[-1] * qi[-1]]
    levels = len(qr)
    qr, qi = jnp.stack(qr), jnp.stack(qi)
    m1 = jnp.concatenate([qr, qr], axis=-1)
    m2 = jnp.concatenate([-qi, qi], axis=-1)
    ap = jnp.stack([m1, m2], axis=-1).transpose(1, 2, 0, 3).reshape(g, 2 * n, 2 * levels)
    return toe, bm, cinj, ap


def _ssm_kernel(u_ref, t_ref, bm_ref, cinj_ref, ap_ref, o_ref, *, n_chunks):
    n = S5_STATE
    u = u_ref[0]
    y = jnp.dot(t_ref[0], u, preferred_element_type=F32)
    x = jnp.dot(bm_ref[0], u, preferred_element_type=F32)
    ap = ap_ref[0]
    col = lax.broadcasted_iota(jnp.int32, x.shape, 1) % n_chunks
    for k in range(int(math.log2(n_chunks))):
        sh = 1 << k
        xs = jnp.where(col >= sh, pltpu.roll(x, sh, 1), 0.0)
        xsw = jnp.concatenate([xs[n:], xs[:n]], axis=0)
        x = x + ap[:, 2 * k:2 * k + 1] * xs + ap[:, 2 * k + 1:2 * k + 2] * xsw
    h0 = jnp.where(col >= 1, pltpu.roll(x, 1, 1), 0.0)
    h0_hi = h0.astype(BF16)
    h0_lo = (h0 - h0_hi.astype(F32)).astype(BF16)
    cinj = cinj_ref[0]
    y = y + jnp.dot(cinj, h0_hi, preferred_element_type=F32)
    y = y + jnp.dot(cinj, h0_lo, preferred_element_type=F32)
    o_ref[0] = jax.nn.gelu(y).astype(o_ref.dtype)


def _glu_kernel(vt_ref, w_ref, x_ref, o_ref):
    z = lax.dot_general(vt_ref[...], w_ref[...], (((0,), (0,)), ((), ())),
                        preferred_element_type=F32)
    d = o_ref.shape[1]
    o_ref[...] = x_ref[...] + z[:, :d] * jax.nn.sigmoid(z[:, d:])


def _s5_layer(x, g_norm, w_in, lam_re, lam_im, log_dt, b_re, b_im, c_re, c_im, d_skip, w_glu,
              bsz, seq):
    m, d = x.shape
    n_chunks = seq // CHUNK
    ncol = bsz * n_chunks
    toe, bm, cinj, ap = _s5_operators(lam_re, lam_im, log_dt, b_re, b_im, c_re, c_im, d_skip,
                                      n_chunks)
    ut = _norm_linear_t(x, g_norm, w_in.T.astype(BF16), BF16)
    rows = S5_GROUP * CHUNK
    ut = ut.reshape(S5_GROUPS, rows, ncol)
    vt = pl.pallas_call(
        functools.partial(_ssm_kernel, n_chunks=n_chunks),
        grid=(S5_GROUPS,),
        in_specs=[
            pl.BlockSpec((1, rows, ncol), lambda g: (g, 0, 0)),
            pl.BlockSpec((1, rows, rows), lambda g: (g, 0, 0)),
            pl.BlockSpec((1, 2 * S5_STATE, rows), lambda g: (g, 0, 0)),
            pl.BlockSpec((1, rows, 2 * S5_STATE), lambda g: (g, 0, 0)),
            pl.BlockSpec((1,) + ap.shape[1:], lambda g: (g, 0, 0)),
        ],
        out_specs=pl.BlockSpec((1, rows, ncol), lambda g: (g, 0, 0)),
        out_shape=jax.ShapeDtypeStruct((S5_GROUPS, rows, ncol), BF16),
        compiler_params=_params("parallel"),
        name="s5_ssm",
    )(ut, toe, bm, cinj, ap)
    vt = vt.reshape(d, m)
    tm = 512
    return pl.pallas_call(
        _glu_kernel,
        grid=(m // tm,),
        in_specs=[
            pl.BlockSpec((d, tm), lambda i: (0, i)),
            pl.BlockSpec((d, 2 * d), lambda i: (0, 0)),
            pl.BlockSpec((tm, d), lambda i: (i, 0)),
        ],
        out_specs=pl.BlockSpec((tm, d), lambda i: (i, 0)),
        out_shape=jax.ShapeDtypeStruct((m, d), F32),
        compiler_params=_params("parallel"),
        name="s5_glu",
    )(vt, w_glu.astype(BF16), x)


def _attn_bias(rel_bias):
    r = jnp.arange(ATT_TQ)[:, None]
    j = jnp.arange(2 * ATT_TQ)[None, :]
    dist = jnp.clip(r + ATT_TQ - j, -MAX_REL, MAX_REL) + MAX_REL
    dchunk = (r + ATT_TQ) // CHUNK - j // CHUNK
    valid = (dchunk >= 0) & (dchunk <= LEFT_CHUNKS)
    bias = jnp.take(rel_bias.astype(F32), dist.reshape(-1), axis=1).reshape(ATT_HEADS, ATT_TQ, 2 * ATT_TQ)
    return jnp.where(valid[None], bias, NEG_INF)


def _attn_kernel(q_ref, kp_ref, kc_ref, vp_ref, vc_ref, bias_ref, o_ref):
    first = pl.program_id(2) == 0
    q = q_ref[...]
    k = jnp.concatenate([kp_ref[...], kc_ref[...]], axis=0)
    v = jnp.concatenate([vp_ref[...], vc_ref[...]], axis=0)
    lane = lax.broadcasted_iota(jnp.int32, (1, LANES), 1)
    col = lax.broadcasted_iota(jnp.int32, (ATT_TQ, 2 * ATT_TQ), 1)
    pad = (col < ATT_TQ) & first
    scale = ATT_HEAD_DIM ** -0.5
    acc = jnp.zeros((ATT_TQ, LANES), F32)
    for h in range(LANES // ATT_HEAD_DIM):
        in_head = (lane >= h * ATT_HEAD_DIM) & (lane < (h + 1) * ATT_HEAD_DIM)
        qh = jnp.where(in_head, q, jnp.zeros_like(q))
        s = lax.dot_general(qh, k, (((1,), (1,)), ((), ())), preferred_element_type=F32)
        s = s * scale + bias_ref[h]
        s = jnp.where(pad, NEG_INF, s)
        mx = jnp.max(s, axis=-1, keepdims=True)
        p = jnp.exp(s - mx)
        l = jnp.sum(p, axis=-1, keepdims=True)
        vh = jnp.where(in_head, v, jnp.zeros_like(v))
        acc = acc + jnp.dot(p.astype(BF16), vh, preferred_element_type=F32) / l
    o_ref[...] = acc.astype(o_ref.dtype)


def _attn_layer(x, g_norm, w_qkv, w_o, rel_bias, bsz, seq):
    m, d = x.shape
    qkv = _norm_linear(x, g_norm, w_qkv.astype(BF16), BF16)
    bias = _attn_bias(rel_bias)
    nq = seq // ATT_TQ
    hp = d // LANES
    heads_per = LANES // ATT_HEAD_DIM

    def cur(off):
        return lambda h, b, i: (b * nq + i, off + h)

    def prev(off):
        return lambda h, b, i: (b * nq + jnp.maximum(i - 1, 0), off + h)

    blk = (ATT_TQ, LANES)
    o = pl.pallas_call(
        _attn_kernel,
        grid=(hp, bsz, nq),
        in_specs=[
            pl.BlockSpec(blk, cur(0)),
            pl.BlockSpec(blk, prev(hp)),
            pl.BlockSpec(blk, cur(hp)),
            pl.BlockSpec(blk, prev(2 * hp)),
            pl.BlockSpec(blk, cur(2 * hp)),
            pl.BlockSpec((heads_per, ATT_TQ, 2 * ATT_TQ), lambda h, b, i: (h, 0, 0)),
        ],
        out_specs=pl.BlockSpec(blk, cur(0)),
        out_shape=jax.ShapeDtypeStruct((m, d), BF16),
        compiler_params=_params("parallel", "parallel", "arbitrary"),
        name="chunk_attn",
    )(qkv, qkv, qkv, qkv, qkv, bias)
    return _linear_residual(x, o, w_o.astype(BF16))


def _lru_kernel(gate_ref, xr_ref, cw_ref, cb_ref, wax_ref, ba_ref, bx_ref, lam_ref, o_ref,
                prev_ref, carry_ref, a_s, b_s):
    t, w = xr_ref.shape

    @pl.when(pl.program_id(1) == 0)
    def _():
        prev_ref[...] = jnp.zeros_like(prev_ref)
        carry_ref[...] = jnp.zeros_like(carry_ref)

    xr = xr_ref[...]
    xext = jnp.concatenate([prev_ref[...], xr], axis=0)
    cw = cw_ref[...]
    xc = cb_ref[...] + cw[0:1] * xext[SUBLANES - 3:SUBLANES - 3 + t]
    for k in range(1, CONV_W):
        xc = xc + cw[k:k + 1] * xext[SUBLANES - 3 + k:SUBLANES - 3 + k + t]
    prev_ref[...] = xr[t - SUBLANES:]

    xcb = xc.astype(BF16)
    pre_a, pre_x = [], []
    for blk in range(LRU_BLOCKS):
        pre = jnp.dot(xcb[:, blk * LRU_BLOCK_W:(blk + 1) * LRU_BLOCK_W], wax_ref[blk],
                      preferred_element_type=F32)
        pre_a.append(pre[:, :LRU_BLOCK_W])
        pre_x.append(pre[:, LRU_BLOCK_W:])
    r = jax.nn.sigmoid(jnp.concatenate(pre_a, axis=1) + ba_ref[...])
    ig = jax.nn.sigmoid(jnp.concatenate(pre_x, axis=1) + bx_ref[...])
    z = -lam_ref[...]
    softplus = jnp.maximum(z, 0.0) + jnp.log1p(jnp.exp(-jnp.abs(z)))
    log_a = -LRU_C * r * softplus
    a = jnp.exp(log_a)
    b = jnp.sqrt(1.0 - a * a) * (ig * xc)

    a3 = a.reshape(t // SUBLANES, SUBLANES, w)
    b3 = b.reshape(t // SUBLANES, SUBLANES, w)
    sub = lax.broadcasted_iota(jnp.int32, a3.shape, 1)
    for sh in (1, 2, 4):
        a_sh = jnp.where(sub >= sh, pltpu.roll(a3, sh, 1), 1.0)
        b_sh = jnp.where(sub >= sh, pltpu.roll(b3, sh, 1), 0.0)
        b3 = a3 * b_sh + b3
        a3 = a3 * a_sh
    a_s[...] = a3
    b_s[...] = b3

    def body(j, carry):
        hb = b_s[j] + a_s[j] * carry
        b_s[j] = hb
        return hb[SUBLANES - 1:SUBLANES, :]

    carry = lax.fori_loop(0, t // SUBLANES, body, carry_ref[0:1, :])
    carry_ref[0:1, :] = carry
    h = b_s[...].reshape(t, w)
    o_ref[...] = (jax.nn.gelu(gate_ref[...]) * h).astype(o_ref.dtype)


def _lru_layer(x, g_norm, w_in, conv_w, conv_b, w_a, b_a, w_x, b_x, lam, w_out, bsz, seq):
    m, d = x.shape
    z = _norm_linear(x, g_norm, w_in.astype(BF16), F32)
    wax = jnp.concatenate([w_a, w_x], axis=-1).astype(BF16)
    nt = seq // LRU_T
    row = lambda v: v.reshape(1, d).astype(F32)
    vec = pl.BlockSpec((1, d), lambda b, i: (0, 0))
    y = pl.pallas_call(
        _lru_kernel,
        grid=(bsz, nt),
        in_specs=[
            pl.BlockSpec((LRU_T, d), lambda b, i: (b * nt + i, 0)),
            pl.BlockSpec((LRU_T, d), lambda b, i: (b * nt + i, 1)),
            pl.BlockSpec((CONV_W, d), lambda b, i: (0, 0)),
            vec,
            pl.BlockSpec(wax.shape, lambda b, i: (0, 0, 0)),
            vec, vec, vec,
        ],
        out_specs=pl.BlockSpec((LRU_T, d), lambda b, i: (b * nt + i, 0)),
        out_shape=jax.ShapeDtypeStruct((m, d), BF16),
        scratch_shapes=[
            pltpu.VMEM((SUBLANES, d), F32),
            pltpu.VMEM((SUBLANES, d), F32),
            pltpu.VMEM((LRU_T // SUBLANES, SUBLANES, d), F32),
            pltpu.VMEM((LRU_T // SUBLANES, SUBLANES, d), F32),
        ],
        compiler_params=_params("parallel", "arbitrary"),
        name="rglru_scan",
    )(z, z, conv_w.astype(F32), row(conv_b), wax, row(b_a), row(b_x), row(lam))
    return _linear_residual(x, y, w_out.astype(BF16))


def _memattn_kernel(x_ref, g_ref, wq_ref, k_ref, v_ref, wo_ref, o_ref, ocat_ref):
    x = x_ref[...]
    hn = _rms(x, g_ref[...]).astype(BF16)
    q = jnp.dot(hn, wq_ref[...], preferred_element_type=F32) * (MEM_HEAD_DIM ** -0.5)
    q = q.astype(BF16)
    for h in range(MEM_HEADS):
        sl = slice(h * MEM_HEAD_DIM, (h + 1) * MEM_HEAD_DIM)
        s = lax.dot_general(q[:, sl], k_ref[:, sl], (((1,), (1,)), ((), ())),
                            preferred_element_type=F32)
        mx = jnp.max(s, axis=-1, keepdims=True)
        p = jnp.exp(s - mx)
        l = jnp.sum(p, axis=-1, keepdims=True)
        o = jnp.dot(p.astype(BF16), v_ref[:, sl], preferred_element_type=F32) / l
        ocat_ref[:, sl] = o.astype(BF16)
    o_ref[...] = x + jnp.dot(ocat_ref[...], wo_ref[...], preferred_element_type=F32)


def _memattn_layer(x, g_norm, w_q, kv_all, layer, w_o, bsz, seq, tm=512):
    m, d = x.shape
    nt = seq // tm
    wspec = pl.BlockSpec((d, d), lambda b, i: (0, 0))
    return pl.pallas_call(
        _memattn_kernel,
        grid=(bsz, nt),
        in_specs=[
            pl.BlockSpec((tm, d), lambda b, i: (b * nt + i, 0)),
            pl.BlockSpec((1, d), lambda b, i: (0, 0)),
            wspec,
            pl.BlockSpec((N_MEM, d), lambda b, i: (b, 2 * layer)),
            pl.BlockSpec((N_MEM, d), lambda b, i: (b, 2 * layer + 1)),
            wspec,
        ],
        out_specs=pl.BlockSpec((tm, d), lambda b, i: (b * nt + i, 0)),
        out_shape=jax.ShapeDtypeStruct((m, d), F32),
        scratch_shapes=[pltpu.VMEM((tm, d), BF16)],
        compiler_params=_params("parallel", "parallel"),
        name="mem_attn",
    )(x, g_norm.reshape(1, d), w_q.astype(BF16), kv_all, kv_all, w_o.astype(BF16))


def _ffn_kernel(x_ref, g_ref, c_ref, wg_ref, wu_ref, wd_ref, o_ref, hn_ref, acc_ref, *, weighted):
    e = pl.program_id(1)
    f = pl.program_id(2)

    @pl.when((e == 0) & (f == 0))
    def _():
        x = x_ref[...]
        hn_ref[...] = _rms(x, g_ref[...]).astype(BF16)
        acc_ref[...] = x

    hn = hn_ref[...]
    gate = jnp.dot(hn, wg_ref[0], preferred_element_type=F32)
    up = jnp.dot(hn, wu_ref[0], preferred_element_type=F32)
    act = gate * jax.nn.sigmoid(gate) * up
    if weighted:
        lane = lax.broadcasted_iota(jnp.int32, c_ref.shape, 1)
        act = act * jnp.sum(jnp.where(lane == e, c_ref[...], 0.0), axis=1, keepdims=True)
    acc_ref[...] += jnp.dot(act.astype(BF16), wd_ref[0], preferred_element_type=F32)

    @pl.when((e == pl.num_programs(1) - 1) & (f == pl.num_programs(2) - 1))
    def _():
        o_ref[...] = acc_ref[...]


def _ffn_layer(x, g_norm, combine, w_gu, w_down, tf, tm=512):
    m, d = x.shape
    n_e, ff, _ = w_down.shape
    nf = ff // tf
    weighted = combine is not None
    if combine is None:
        combine = jnp.ones((SUBLANES, LANES), F32)
        cspec = pl.BlockSpec((SUBLANES, LANES), lambda i, e, f: (0, 0))
    else:
        cspec = pl.BlockSpec((tm, LANES), lambda i, e, f: (i, 0))
    return pl.pallas_call(
        functools.partial(_ffn_kernel, weighted=weighted),
        grid=(m // tm, n_e, nf),
        in_specs=[
            pl.BlockSpec((tm, d), lambda i, e, f: (i, 0)),
            pl.BlockSpec((1, d), lambda i, e, f: (0, 0)),
            cspec,
            pl.BlockSpec((1, d, tf), lambda i, e, f: (e, 0, f)),
            pl.BlockSpec((1, d, tf), lambda i, e, f: (e, 0, nf + f)),
            pl.BlockSpec((1, tf, d), lambda i, e, f: (e, f, 0)),
        ],
        out_specs=pl.BlockSpec((tm, d), lambda i, e, f: (i, 0)),
        out_shape=jax.ShapeDtypeStruct((m, d), F32),
        scratch_shapes=[pltpu.VMEM((tm, d), BF16), pltpu.VMEM((tm, d), F32)],
        compiler_params=_params("parallel", "arbitrary", "arbitrary"),
        name="ffn",
    )(x, g_norm.reshape(1, d), combine, w_gu, w_gu, w_down)


def _router_kernel(x_ref, g_ref, wr_ref, c_ref):
    hn = _rms(x_ref[...], g_ref[...])
    logits = jnp.dot(hn, wr_ref[...], preferred_element_type=F32, precision=HIGHEST)
    lane = lax.broadcasted_iota(jnp.int32, logits.shape, 1)
    lg = jnp.where(lane < N_EXPERTS, logits, -jnp.inf)
    m1 = jnp.max(lg, axis=1, keepdims=True)
    i1 = jnp.min(jnp.where(lg == m1, lane, LANES), axis=1, keepdims=True)
    lg2 = jnp.where(lane == i1, -jnp.inf, lg)
    m2 = jnp.max(lg2, axis=1, keepdims=True)
    i2 = jnp.min(jnp.where(lg2 == m2, lane, LANES), axis=1, keepdims=True)
    e2 = jnp.exp(m2 - m1)
    g1 = 1.0 / (1.0 + e2)
    g2 = e2 / (1.0 + e2)
    c_ref[...] = jnp.where(lane == i1, g1, 0.0) + jnp.where(lane == i2, g2, 0.0)


def _router(x, g_norm, w_router, tm=1024):
    m, d = x.shape
    wr = jnp.zeros((d, LANES), F32).at[:, :N_EXPERTS].set(w_router.astype(F32))
    return pl.pallas_call(
        _router_kernel,
        grid=(m // tm,),
        in_specs=[
            pl.BlockSpec((tm, d), lambda i: (i, 0)),
            pl.BlockSpec((1, d), lambda i: (0, 0)),
            pl.BlockSpec((d, LANES), lambda i: (0, 0)),
        ],
        out_specs=pl.BlockSpec((tm, LANES), lambda i: (i, 0)),
        out_shape=jax.ShapeDtypeStruct((m, LANES), F32),
        compiler_params=_params("parallel"),
        name="router",
    )(x, g_norm.reshape(1, d), wr)


def _to_frame_major(x, bsz, seq):
    d = x.shape[-1]
    return x.reshape(bsz * (seq // CHUNK), CHUNK, d).transpose(1, 0, 2).reshape(bsz * seq, d)


def _from_frame_major(x, bsz, seq):
    d = x.shape[-1]
    return x.reshape(CHUNK, bsz * (seq // CHUNK), d).transpose(1, 0, 2).reshape(bsz * seq, d)


def kernel(x, mem, norm_mix, norm_mem, norm_ffn, mem_norm, final_norm, s5_w_in, s5_lam_re, s5_lam_im, s5_log_dt, s5_b_re, s5_b_im, s5_c_re, s5_c_im, s5_d, s5_w_glu, att_w_qkv, att_w_o, att_rel_bias, lru_w_in, lru_conv_w, lru_conv_b, lru_w_a, lru_b_a, lru_w_x, lru_b_x, lru_lam, lru_w_out, mem_w_q, mem_w_kv, mem_w_o, ffn_w_gu, ffn_w_down, moe_w_router, moe_w_gu, moe_w_down):
    bsz, seq, d = x.shape
    x = x.reshape(bsz * seq, d).astype(F32)

    w_kv_all = jnp.concatenate([mem_w_kv[i] for i in range(DEPTH)], axis=1).astype(BF16)
    kv_all = _norm_linear(mem.reshape(bsz * N_MEM, d).astype(F32), mem_norm, w_kv_all, BF16,
                          tm=bsz * N_MEM, tn=2 * d)

    for i in range(DEPTH):
        kind, j = i % 3, i // 3
        if kind == 0:
            xp = _to_frame_major(x, bsz, seq)
            xp = _s5_layer(xp, norm_mix[i], s5_w_in[j], s5_lam_re[j], s5_lam_im[j], s5_log_dt[j],
                           s5_b_re[j], s5_b_im[j], s5_c_re[j], s5_c_im[j], s5_d[j], s5_w_glu[j],
                           bsz, seq)
            x = _from_frame_major(xp, bsz, seq)
        elif kind == 1:
            x = _attn_layer(x, norm_mix[i], att_w_qkv[j], att_w_o[j], att_rel_bias[j], bsz, seq)
        else:
            x = _lru_layer(x, norm_mix[i], lru_w_in[j], lru_conv_w[j], lru_conv_b[j], lru_w_a[j],
                           lru_b_a[j], lru_w_x[j], lru_b_x[j], lru_lam[j], lru_w_out[j], bsz, seq)
        x = _memattn_layer(x, norm_mem[i], mem_w_q[i], kv_all, i, mem_w_o[i], bsz, seq)
        if i % 2 == 0:
            x = _ffn_layer(x, norm_ffn[i], None, ffn_w_gu[i // 2][None].astype(BF16),
                           ffn_w_down[i // 2][None].astype(BF16), tf=1408)
        else:
            combine = _router(x, norm_ffn[i], moe_w_router[i // 2])
            x = _ffn_layer(x, norm_ffn[i], combine, moe_w_gu[i // 2].astype(BF16),
                           moe_w_down[i // 2].astype(BF16), tf=1024)
    return _norm(x, final_norm).reshape(bsz, seq, d)
```

```python
import functools
import math

import jax
import jax.numpy as jnp
from jax import lax
from jax.experimental import pallas as pl
from jax.experimental.pallas import tpu as pltpu

F32 = jnp.float32
BF16 = jnp.bfloat16
HIGHEST = lax.Precision.HIGHEST

D_MODEL = 1024
DEPTH = 4
CHUNK = 64
N_MEM = 256
EPS = 1e-6

S5_GROUP = 16
S5_GROUPS = D_MODEL // S5_GROUP
S5_STATE = 64

ATT_HEADS = 16
ATT_HEAD_DIM = D_MODEL // ATT_HEADS
LEFT_CHUNKS = 8
MAX_REL = 128
ATT_TQ = 4 * CHUNK
ATT_LEFT = LEFT_CHUNKS * CHUNK
ATT_TK = ATT_LEFT + ATT_TQ
NEG_INF = -1e30

LRU_BLOCKS = 8
LRU_BLOCK_W = D_MODEL // LRU_BLOCKS
CONV_W = 4
LRU_C = 8.0
LRU_T = 512
SUBLANES = 8
LANES = 128

MEM_HEADS = 4
MEM_HEAD_DIM = D_MODEL // MEM_HEADS

N_EXPERTS = 8

VMEM_LIMIT = 56 * 1024 * 1024


def _params(*sem):
    return pltpu.CompilerParams(dimension_semantics=sem, vmem_limit_bytes=VMEM_LIMIT)


def _rms(x, g):
    ms = jnp.mean(x * x, axis=-1, keepdims=True)
    return x * lax.rsqrt(ms + EPS) * g


def _norm_linear_kernel(x_ref, g_ref, w_ref, o_ref, hn_ref):
    @pl.when(pl.program_id(1) == 0)
    def _():
        hn_ref[...] = _rms(x_ref[...], g_ref[...]).astype(BF16)

    o_ref[...] = jnp.dot(hn_ref[...], w_ref[...], preferred_element_type=F32).astype(o_ref.dtype)


def _norm_linear(x, g, w, out_dtype, tm=512, tn=None):
    m, d = x.shape
    n = w.shape[1]
    tn = n if tn is None else tn
    return pl.pallas_call(
        _norm_linear_kernel,
        grid=(m // tm, n // tn),
        in_specs=[
            pl.BlockSpec((tm, d), lambda i, j: (i, 0)),
            pl.BlockSpec((1, d), lambda i, j: (0, 0)),
            pl.BlockSpec((d, tn), lambda i, j: (0, j)),
        ],
        out_specs=pl.BlockSpec((tm, tn), lambda i, j: (i, j)),
        out_shape=jax.ShapeDtypeStruct((m, n), out_dtype),
        scratch_shapes=[pltpu.VMEM((tm, d), BF16)],
        compiler_params=_params("parallel", "arbitrary"),
        name="norm_linear",
    )(x, g.reshape(1, d), w)


def _norm_linear_t_kernel(x_ref, g_ref, wt_ref, o_ref):
    hn = _rms(x_ref[...], g_ref[...]).astype(BF16)
    o_ref[...] = lax.dot_general(
        wt_ref[...], hn, (((1,), (1,)), ((), ())), preferred_element_type=F32
    ).astype(o_ref.dtype)


def _norm_linear_t(x, g, wt, out_dtype, tm=512):
    m, d = x.shape
    n = wt.shape[0]
    return pl.pallas_call(
        _norm_linear_t_kernel,
        grid=(m // tm,),
        in_specs=[
            pl.BlockSpec((tm, d), lambda i: (i, 0)),
            pl.BlockSpec((1, d), lambda i: (0, 0)),
            pl.BlockSpec((n, d), lambda i: (0, 0)),
        ],
        out_specs=pl.BlockSpec((n, tm), lambda i: (0, i)),
        out_shape=jax.ShapeDtypeStruct((n, m), out_dtype),
        compiler_params=_params("parallel"),
        name="norm_linear_t",
    )(x, g.reshape(1, d), wt)


def _linear_residual_kernel(x_ref, y_ref, w_ref, o_ref):
    o_ref[...] = x_ref[...] + jnp.dot(y_ref[...], w_ref[...], preferred_element_type=F32)


def _linear_residual(x, y, w, tm=1024):
    m, d = x.shape
    k = y.shape[1]
    return pl.pallas_call(
        _linear_residual_kernel,
        grid=(m // tm,),
        in_specs=[
            pl.BlockSpec((tm, d), lambda i: (i, 0)),
            pl.BlockSpec((tm, k), lambda i: (i, 0)),
            pl.BlockSpec((k, d), lambda i: (0, 0)),
        ],
        out_specs=pl.BlockSpec((tm, d), lambda i: (i, 0)),
        out_shape=jax.ShapeDtypeStruct((m, d), F32),
        compiler_params=_params("parallel"),
        name="linear_residual",
    )(x, y, w)


def _norm_kernel(x_ref, g_ref, o_ref):
    o_ref[...] = _rms(x_ref[...], g_ref[...])


def _norm(x, g, tm=1024):
    m, d = x.shape
    return pl.pallas_call(
        _norm_kernel,
        grid=(m // tm,),
        in_specs=[pl.BlockSpec((tm, d), lambda i: (i, 0)), pl.BlockSpec((1, d), lambda i: (0, 0))],
        out_specs=pl.BlockSpec((tm, d), lambda i: (i, 0)),
        out_shape=jax.ShapeDtypeStruct((m, d), F32),
        compiler_params=_params("parallel"),
        name="final_norm",
    )(x, g.reshape(1, d))


def _s5_operators(lam_re, lam_im, log_dt, b_re, b_im, c_re, c_im, d_skip, n_chunks):
    g, n, p = S5_GROUPS, S5_STATE, S5_GROUP
    lr = lam_re.astype(F32)
    li = lam_im.astype(F32)
    dt = jnp.exp(log_dt.astype(F32))[:, None]
    mag = jnp.exp(lr * dt)
    ar = mag * jnp.cos(li * dt)
    ai = mag * jnp.sin(li * dt)
    den = lr * lr + li * li
    fr = ((ar - 1.0) * lr + ai * li) / den
    fi = (ai * lr - (ar - 1.0) * li) / den
    bbr = fr[..., None] * b_re - fi[..., None] * b_im
    bbi = fr[..., None] * b_im + fi[..., None] * b_re

    def powers(ks):
        kk = ks.astype(F32)[:, None, None]
        m = jnp.exp(kk * (lr * dt))
        return m * jnp.cos(kk * (li * dt)), m * jnp.sin(kk * (li * dt))

    pr, pi = powers(jnp.arange(CHUNK + 1))
    lcr = c_re[None] * pr[:, :, None, :] - c_im[None] * pi[:, :, None, :]
    lci = c_re[None] * pi[:, :, None, :] + c_im[None] * pr[:, :, None, :]
    kern = (jnp.einsum('tgqn,gnp->gtqp', lcr[:CHUNK], bbr, precision=HIGHEST)
            - jnp.einsum('tgqn,gnp->gtqp', lci[:CHUNK], bbi, precision=HIGHEST))
    t_idx = jnp.arange(CHUNK)
    kern = kern.at[:, 0].add(d_skip.reshape(g, p)[:, :, None] * jnp.eye(p, dtype=F32))
    kern = kern.transpose(0, 2, 3, 1)
    w = jnp.concatenate([kern[..., :1], jnp.zeros((g, p, p, CHUNK), F32), kern[..., :0:-1]], axis=-1)
    toe = jnp.broadcast_to(w[..., None, :], (g, p, p, CHUNK, 2 * CHUNK))
    toe = toe.reshape(g, p, p, 2 * CHUNK * CHUNK)[..., :CHUNK * (2 * CHUNK - 1)]
    toe = toe.reshape(g, p, p, CHUNK, 2 * CHUNK - 1)[..., :CHUNK]
    toe = toe.transpose(0, 1, 3, 2, 4).reshape(g, p * CHUNK, p * CHUNK).astype(BF16)
    rev_r = pr[CHUNK - 1 - t_idx]
    rev_i = pi[CHUNK - 1 - t_idx]
    bm_r = jnp.einsum('sgn,gnp->gnps', rev_r, bbr) - jnp.einsum('sgn,gnp->gnps', rev_i, bbi)
    bm_i = jnp.einsum('sgn,gnp->gnps', rev_r, bbi) + jnp.einsum('sgn,gnp->gnps', rev_i, bbr)
    bm = jnp.concatenate([bm_r, bm_i], axis=1).reshape(g, 2 * n, p * CHUNK).astype(BF16)
    cinj = jnp.concatenate([lcr[1:], -lci[1:]], axis=-1)
    cinj = cinj.transpose(1, 2, 0, 3).reshape(g, p * CHUNK, 2 * n).astype(BF16)
    qr, qi = [pr[CHUNK]], [pi[CHUNK]]
    for _ in range(int(math.log2(n_chunks)) - 1):
        qr, qi = qr + [qr[-1] * qr[-1] - qi[-1] * qi[-1]], qi + [2.0 * qr[-1] * qi[-1]]
    levels = len(qr)
    qr, qi = jnp.stack(qr), jnp.stack(qi)
    m1 = jnp.concatenate([qr, qr], axis=-1)
    m2 = jnp.concatenate([-qi, qi], axis=-1)
    ap = jnp.stack([m1, m2], axis=-1).transpose(1, 2, 0, 3).reshape(g, 2 * n, 2 * levels)
    return toe, bm, cinj, ap


def _ssm_kernel(u_ref, t_ref, bm_ref, cinj_ref, ap_ref, o_ref, *, n_chunks):
    n = S5_STATE
    u = u_ref[0]
    y = jnp.dot(t_ref[0], u, preferred_element_type=F32)
    x = jnp.dot(bm_ref[0], u, preferred_element_type=F32)
    ap = ap_ref[0]
    col = lax.broadcasted_iota(jnp.int32, x.shape, 1) % n_chunks
    for k in range(int(math.log2(n_chunks))):
        sh = 1 << k
        xs = jnp.where(col >= sh, pltpu.roll(x, sh, 1), 0.0)
        xsw = jnp.concatenate([xs[n:], xs[:n]], axis=0)
        x = x + ap[:, 2 * k:2 * k + 1] * xs + ap[:, 2 * k + 1:2 * k + 2] * xsw
    h0 = jnp.where(col >= 1, pltpu.roll(x, 1, 1), 0.0)
    h0_hi = h0.astype(BF16)
    h0_lo = (h0 - h0_hi.astype(F32)).astype(BF16)
    cinj = cinj_ref[0]
    y = y + jnp.dot(cinj, h0_hi, preferred_element_type=F32)
    y = y + jnp.dot(cinj, h0_lo, preferred_element_type=F32)
    o_ref[0] = jax.nn.gelu(y).astype(o_ref.dtype)


def _glu_kernel(vt_ref, w_ref, x_ref, o_ref):
    z = lax.dot_general(vt_ref[...], w_ref[...], (((0,), (0,)), ((), ())),
                        preferred_element_type=F32)
    d = o_ref.shape[1]
    o_ref[...] = x_ref[...] + z[:, :d] * jax.nn.sigmoid(z[:, d:])


def _s5_layer(x, g_norm, w_in, lam_re, lam_im, log_dt, b_re, b_im, c_re, c_im, d_skip, w_glu,
              bsz, seq):
    m, d = x.shape
    n_chunks = seq // CHUNK
    ncol = bsz * n_chunks
    toe, bm, cinj, ap = _s5_operators(lam_re, lam_im, log_dt, b_re, b_im, c_re, c_im, d_skip,
                                      n_chunks)
    ut = _norm_linear_t(x, g_norm, w_in.T.astype(BF16), BF16)
    rows = S5_GROUP * CHUNK
    ut = ut.reshape(S5_GROUPS, rows, ncol)
    vt = pl.pallas_call(
        functools.partial(_ssm_kernel, n_chunks=n_chunks),
        grid=(S5_GROUPS,),
        in_specs=[
            pl.BlockSpec((1, rows, ncol), lambda g: (g, 0, 0)),
            pl.BlockSpec((1, rows, rows), lambda g: (g, 0, 0)),
            pl.BlockSpec((1, 2 * S5_STATE, rows), lambda g: (g, 0, 0)),
            pl.BlockSpec((1, rows, 2 * S5_STATE), lambda g: (g, 0, 0)),
            pl.BlockSpec((1,) + ap.shape[1:], lambda g: (g, 0, 0)),
        ],
        out_specs=pl.BlockSpec((1, rows, ncol), lambda g: (g, 0, 0)),
        out_shape=jax.ShapeDtypeStruct((S5_GROUPS, rows, ncol), BF16),
        compiler_params=_params("parallel"),
        name="s5_ssm",
    )(ut, toe, bm, cinj, ap)
    vt = vt.reshape(d, m)
    tm = 512
    return pl.pallas_call(
        _glu_kernel,
        grid=(m // tm,),
        in_specs=[
            pl.BlockSpec((d, tm), lambda i: (0, i)),
            pl.BlockSpec((d, 2 * d), lambda i: (0, 0)),
            pl.BlockSpec((tm, d), lambda i: (i, 0)),
        ],
        out_specs=pl.BlockSpec((tm, d), lambda i: (i, 0)),
        out_shape=jax.ShapeDtypeStruct((m, d), F32),
        compiler_params=_params("parallel"),
        name="s5_glu",
    )(vt, w_glu.astype(BF16), x)


def _attn_bias(rel_bias):
    r = jnp.arange(ATT_TQ)[:, None]
    j = jnp.arange(ATT_TK)[None, :]
    dchunk = (r + ATT_LEFT) // CHUNK - j // CHUNK
    valid = (dchunk >= 0) & (dchunk <= LEFT_CHUNKS)
    period = ATT_TQ + ATT_TK
    k = jnp.arange(period)
    dist = jnp.clip(ATT_TK - 1 - k, -MAX_REL, MAX_REL) + MAX_REL
    u = jnp.take(rel_bias.astype(F32), dist, axis=1)
    han = jnp.broadcast_to(u[:, None, :], (ATT_HEADS, ATT_TQ + 1, period))
    han = han.reshape(ATT_HEADS, (ATT_TQ + 1) * period)[:, :ATT_TQ * (period + 1)]
    han = han.reshape(ATT_HEADS, ATT_TQ, period + 1)[:, ::-1, :ATT_TK]
    return jnp.where(valid[None], han, NEG_INF)


def _attn_kernel(q_ref, k0_ref, k1_ref, k2_ref, v0_ref, v1_ref, v2_ref, bias_ref, o_ref):
    i = pl.program_id(2)
    q = q_ref[...]
    k = jnp.concatenate([k0_ref[...], k1_ref[...], k2_ref[...]], axis=0)
    v = jnp.concatenate([v0_ref[...], v1_ref[...], v2_ref[...]], axis=0)
    lane = lax.broadcasted_iota(jnp.int32, (1, LANES), 1)
    col = lax.broadcasted_iota(jnp.int32, (ATT_TQ, ATT_TK), 1)
    pad = col < ATT_LEFT - i * ATT_TQ
    scale = ATT_HEAD_DIM ** -0.5
    acc = jnp.zeros((ATT_TQ, LANES), F32)
    for h in range(LANES // ATT_HEAD_DIM):
        in_head = (lane >= h * ATT_HEAD_DIM) & (lane < (h + 1) * ATT_HEAD_DIM)
        qh = jnp.where(in_head, q, jnp.zeros_like(q))
        s = lax.dot_general(qh, k, (((1,), (1,)), ((), ())), preferred_element_type=F32)
        s = s * scale + bias_ref[h]
        s = jnp.where(pad, NEG_INF, s)
        mx = jnp.max(s, axis=-1, keepdims=True)
        p = jnp.exp(s - mx)
        l = jnp.sum(p, axis=-1, keepdims=True)
        vh = jnp.where(in_head, v, jnp.zeros_like(v))
        acc = acc + jnp.dot(p.astype(BF16), vh, preferred_element_type=F32) / l
    o_ref[...] = acc.astype(o_ref.dtype)


def _attn_layer(x, g_norm, w_qkv, w_o, rel_bias, bsz, seq):
    m, d = x.shape
    qkv = _norm_linear(x, g_norm, w_qkv.astype(BF16), BF16)
    bias = _attn_bias(rel_bias)
    nq = seq // ATT_TQ
    hp = d // LANES
    heads_per = LANES // ATT_HEAD_DIM

    def rows(off, back):
        return lambda h, b, i: (b * nq + jnp.maximum(i - back, 0), off + h)

    blk = (ATT_TQ, LANES)
    nkb = ATT_TK // ATT_TQ
    o = pl.pallas_call(
        _attn_kernel,
        grid=(hp, bsz, nq),
        in_specs=(
            [pl.BlockSpec(blk, rows(0, 0))]
            + [pl.BlockSpec(blk, rows(hp, nkb - 1 - kb)) for kb in range(nkb)]
            + [pl.BlockSpec(blk, rows(2 * hp, nkb - 1 - kb)) for kb in range(nkb)]
            + [pl.BlockSpec((heads_per, ATT_TQ, ATT_TK), lambda h, b, i: (h, 0, 0))]
        ),
        out_specs=pl.BlockSpec(blk, rows(0, 0)),
        out_shape=jax.ShapeDtypeStruct((m, d), BF16),
        compiler_params=_params("parallel", "parallel", "arbitrary"),
        name="chunk_attn",
    )(*([qkv] * (1 + 2 * nkb)), bias)
    return _linear_residual(x, o, w_o.astype(BF16))


def _lru_kernel(gate_ref, xr_ref, cw_ref, cb_ref, wax_ref, ba_ref, bx_ref, lam_ref, o_ref,
                prev_ref, carry_ref, a_s, b_s):
    t, w = xr_ref.shape

    @pl.when(pl.program_id(1) == 0)
    def _():
        prev_ref[...] = jnp.zeros_like(prev_ref)
        carry_ref[...] = jnp.zeros_like(carry_ref)

    xr = xr_ref[...]
    xext = jnp.concatenate([prev_ref[...], xr], axis=0)
    cw = cw_ref[...]
    xc = cb_ref[...] + cw[0:1] * xext[SUBLANES - 3:SUBLANES - 3 + t]
    for k in range(1, CONV_W):
        xc = xc + cw[k:k + 1] * xext[SUBLANES - 3 + k:SUBLANES - 3 + k + t]
    prev_ref[...] = xr[t - SUBLANES:]

    xcb = xc.astype(BF16)
    pre_a, pre_x = [], []
    for blk in range(LRU_BLOCKS):
        pre = jnp.dot(xcb[:, blk * LRU_BLOCK_W:(blk + 1) * LRU_BLOCK_W], wax_ref[blk],
                      preferred_element_type=F32)
        pre_a.append(pre[:, :LRU_BLOCK_W])
        pre_x.append(pre[:, LRU_BLOCK_W:])
    r = jax.nn.sigmoid(jnp.concatenate(pre_a, axis=1) + ba_ref[...])
    ig = jax.nn.sigmoid(jnp.concatenate(pre_x, axis=1) + bx_ref[...])
    z = -lam_ref[...]
    softplus = jnp.maximum(z, 0.0) + jnp.log1p(jnp.exp(-jnp.abs(z)))
    log_a = -LRU_C * r * softplus
    a = jnp.exp(log_a)
    b = jnp.sqrt(1.0 - a * a) * (ig * xc)

    a3 = a.reshape(t // SUBLANES, SUBLANES, w)
    b3 = b.reshape(t // SUBLANES, SUBLANES, w)
    sub = lax.broadcasted_iota(jnp.int32, a3.shape, 1)
    for sh in (1, 2, 4):
        a_sh = jnp.where(sub >= sh, pltpu.roll(a3, sh, 1), 1.0)
        b_sh = jnp.where(sub >= sh, pltpu.roll(b3, sh, 1), 0.0)
        b3 = a3 * b_sh + b3
        a3 = a3 * a_sh
    a_s[...] = a3
    b_s[...] = b3

    def body(j, carry):
        hb = b_s[j] + a_s[j] * carry
        b_s[j] = hb
        return hb[SUBLANES - 1:SUBLANES, :]

    carry = lax.fori_loop(0, t // SUBLANES, body, carry_ref[0:1, :])
    carry_ref[0:1, :] = carry
    h = b_s[...].reshape(t, w)
    o_ref[...] = (jax.nn.gelu(gate_ref[...]) * h).astype(o_ref.dtype)


def _lru_layer(x, g_norm, w_in, conv_w, conv_b, w_a, b_a, w_x, b_x, lam, w_out, bsz, seq):
    m, d = x.shape
    z = _norm_linear(x, g_norm, w_in.astype(BF16), F32)
    wax = jnp.concatenate([w_a, w_x], axis=-1).astype(BF16)
    nt = seq // LRU_T
    row = lambda v: v.reshape(1, d).astype(F32)
    vec = pl.BlockSpec((1, d), lambda b, i: (0, 0))
    y = pl.pallas_call(
        _lru_kernel,
        grid=(bsz, nt),
        in_specs=[
            pl.BlockSpec((LRU_T, d), lambda b, i: (b * nt + i, 0)),
            pl.BlockSpec((LRU_T, d), lambda b, i: (b * nt + i, 1)),
            pl.BlockSpec((CONV_W, d), lambda b, i: (0, 0)),
            vec,
            pl.BlockSpec(wax.shape, lambda b, i: (0, 0, 0)),
            vec, vec, vec,
        ],
        out_specs=pl.BlockSpec((LRU_T, d), lambda b, i: (b * nt + i, 0)),
        out_shape=jax.ShapeDtypeStruct((m, d), BF16),
        scratch_shapes=[
            pltpu.VMEM((SUBLANES, d), F32),
            pltpu.VMEM((SUBLANES, d), F32),
            pltpu.VMEM((LRU_T // SUBLANES, SUBLANES, d), F32),
            pltpu.VMEM((LRU_T // SUBLANES, SUBLANES, d), F32),
        ],
        compiler_params=_params("parallel", "arbitrary"),
        name="rglru_scan",
    )(z, z, conv_w.astype(F32), row(conv_b), wax, row(b_a), row(b_x), row(lam))
    return _linear_residual(x, y, w_out.astype(BF16))


def _memattn_kernel(x_ref, g_ref, wq_ref, k_ref, v_ref, wo_ref, o_ref, ocat_ref):
    x = x_ref[...]
    hn = _rms(x, g_ref[...]).astype(BF16)
    q = jnp.dot(hn, wq_ref[...], preferred_element_type=F32) * (MEM_HEAD_DIM ** -0.5)
    q = q.astype(BF16)
    for h in range(MEM_HEADS):
        sl = slice(h * MEM_HEAD_DIM, (h + 1) * MEM_HEAD_DIM)
        s = lax.dot_general(q[:, sl], k_ref[:, sl], (((1,), (1,)), ((), ())),
                            preferred_element_type=F32)
        mx = jnp.max(s, axis=-1, keepdims=True)
        p = jnp.exp(s - mx)
        l = jnp.sum(p, axis=-1, keepdims=True)
        o = jnp.dot(p.astype(BF16), v_ref[:, sl], preferred_element_type=F32) / l
        ocat_ref[:, sl] = o.astype(BF16)
    o_ref[...] = x + jnp.dot(ocat_ref[...], wo_ref[...], preferred_element_type=F32)


def _memattn_layer(x, g_norm, w_q, kv_all, layer, w_o, bsz, seq, tm=512):
    m, d = x.shape
    nt = seq // tm
    wspec = pl.BlockSpec((d, d), lambda b, i: (0, 0))
    return pl.pallas_call(
        _memattn_kernel,
        grid=(bsz, nt),
        in_specs=[
            pl.BlockSpec((tm, d), lambda b, i: (b * nt + i, 0)),
            pl.BlockSpec((1, d), lambda b, i: (0, 0)),
            wspec,
            pl.BlockSpec((N_MEM, d), lambda b, i: (b, 2 * layer)),
            pl.BlockSpec((N_MEM, d), lambda b, i: (b, 2 * layer + 1)),
            wspec,
        ],
        out_specs=pl.BlockSpec((tm, d), lambda b, i: (b * nt + i, 0)),
        out_shape=jax.ShapeDtypeStruct((m, d), F32),
        scratch_shapes=[pltpu.VMEM((tm, d), BF16)],
        compiler_params=_params("parallel", "parallel"),
        name="mem_attn",
    )(x, g_norm.reshape(1, d), w_q.astype(BF16), kv_all, kv_all, w_o.astype(BF16))


def _ffn_kernel(x_ref, g_ref, wg_ref, wu_ref, wd_ref, o_ref, hn_ref, acc_ref):
    f = pl.program_id(1)

    @pl.when(f == 0)
    def _():
        x = x_ref[...]
        hn_ref[...] = _rms(x, g_ref[...]).astype(BF16)
        acc_ref[...] = x

    hn = hn_ref[...]
    gate = jnp.dot(hn, wg_ref[...], preferred_element_type=F32)
    up = jnp.dot(hn, wu_ref[...], preferred_element_type=F32)
    act = gate * jax.nn.sigmoid(gate) * up
    acc_ref[...] += jnp.dot(act.astype(BF16), wd_ref[...], preferred_element_type=F32)

    @pl.when(f == pl.num_programs(1) - 1)
    def _():
        o_ref[...] = acc_ref[...]


def _ffn_layer(x, g_norm, w_gu, w_down, tf, tm=512):
    m, d = x.shape
    ff = w_down.shape[0]
    nf = ff // tf
    return pl.pallas_call(
        _ffn_kernel,
        grid=(m // tm, nf),
        in_specs=[
            pl.BlockSpec((tm, d), lambda i, f: (i, 0)),
            pl.BlockSpec((1, d), lambda i, f: (0, 0)),
            pl.BlockSpec((d, tf), lambda i, f: (0, f)),
            pl.BlockSpec((d, tf), lambda i, f: (0, nf + f)),
            pl.BlockSpec((tf, d), lambda i, f: (f, 0)),
        ],
        out_specs=pl.BlockSpec((tm, d), lambda i, f: (i, 0)),
        out_shape=jax.ShapeDtypeStruct((m, d), F32),
        scratch_shapes=[pltpu.VMEM((tm, d), BF16), pltpu.VMEM((tm, d), F32)],
        compiler_params=_params("parallel", "arbitrary"),
        name="ffn",
    )(x, g_norm.reshape(1, d), w_gu, w_gu, w_down)


MOE_TM = 512
MOE_TR = 512
MOE_HALF = MOE_TM // 2
MOE_ALIGN = 16
XS_W = D_MODEL + LANES


def _route_kernel(x_ref, g_ref, wrt_ref, hn_ref, ct_ref, st_ref, cnt_ref):
    hn = _rms(x_ref[...], g_ref[...])
    hn_ref[...] = hn.astype(BF16)
    lg = lax.dot_general(wrt_ref[...], hn, (((1,), (1,)), ((), ())),
                         preferred_element_type=F32, precision=HIGHEST)
    idx = lax.broadcasted_iota(jnp.int32, lg.shape, 0)
    m1 = jnp.max(lg, axis=0, keepdims=True)
    i1 = jnp.min(jnp.where(lg == m1, idx, N_EXPERTS), axis=0, keepdims=True)
    oh1 = idx == i1
    lg2 = jnp.where(oh1, -jnp.inf, lg)
    m2 = jnp.max(lg2, axis=0, keepdims=True)
    i2 = jnp.min(jnp.where(lg2 == m2, idx, N_EXPERTS), axis=0, keepdims=True)
    oh2 = idx == i2
    e2 = jnp.exp(m2 - m1)
    g1 = 1.0 / (1.0 + e2)
    g2 = e2 / (1.0 + e2)
    ct_ref[...] = jnp.where(oh1, g1, 0.0) + jnp.where(oh2, g2, 0.0)
    sel = (oh1 | oh2).astype(F32)
    st_ref[...] = sel
    cnt_ref[0] = jnp.broadcast_to(jnp.sum(sel, axis=1, keepdims=True), (N_EXPERTS, LANES))


def _tile_rank(st):
    tm = st.shape[1]
    before = (lax.broadcasted_iota(jnp.int32, (tm, tm), 0)
              < lax.broadcasted_iota(jnp.int32, (tm, tm), 1))
    return jnp.dot(st.astype(BF16), before.astype(BF16), preferred_element_type=F32)


def _one_hot_rows(rank, st, e, half):
    tm = rank.shape[1]
    rows = lax.broadcasted_iota(jnp.int32, (MOE_HALF, tm), 0).astype(F32) + float(half * MOE_HALF)
    hit = (rank[e:e + 1] == rows) & (st[e:e + 1] > 0.0)
    return jnp.where(hit, 1.0, 0.0).astype(BF16)


def _dispatch_kernel(base_ref, n_ref, hn_ref, ct_ref, st_ref, xs_in_ref, xs_ref, stage, sem):
    del xs_in_ref
    i = pl.program_id(0)
    hn = hn_ref[...]
    ct = ct_ref[...]
    st = st_ref[...]
    rank = _tile_rank(st)
    g_hi = ct.astype(BF16)
    g_lo = (ct - g_hi.astype(F32)).astype(BF16)
    gm = jnp.concatenate(
        [g_hi, g_lo, jnp.zeros((LANES - 2 * N_EXPERTS, ct.shape[1]), BF16)], axis=0)

    def copy(e, half):
        row = pl.multiple_of(base_ref[i * N_EXPERTS + e] + half * MOE_HALF, MOE_ALIGN)
        return pltpu.make_async_copy(stage.at[e, half], xs_ref.at[pl.ds(row, MOE_HALF)],
                                     sem.at[e, half])

    def fill(e, half):
        p = _one_hot_rows(rank, st, e, half)
        stage[e, half, :, :D_MODEL] = jnp.dot(p, hn, preferred_element_type=F32).astype(BF16)
        stage[e, half, :, D_MODEL:] = lax.dot_general(
            p, gm, (((1,), (1,)), ((), ())), preferred_element_type=F32).astype(BF16)
        copy(e, half).start()

    for e in range(N_EXPERTS):
        fill(e, 0)
        pl.when(n_ref[i * N_EXPERTS + e] > MOE_HALF)(functools.partial(fill, e, 1))
    for e in range(N_EXPERTS):
        copy(e, 0).wait()
        pl.when(n_ref[i * N_EXPERTS + e] > MOE_HALF)(lambda e=e: copy(e, 1).wait())


def _expert_kernel(be_ref, nu_ref, x_ref, wg_ref, wu_ref, wd_ref, o_ref):
    b = pl.program_id(0)

    @pl.when(b < nu_ref[0])
    def _():
        e = be_ref[b]
        xe = x_ref[...]
        h = xe[:, :D_MODEL]
        ext = xe[:, D_MODEL:].astype(F32)
        lane = lax.broadcasted_iota(jnp.int32, ext.shape, 1)
        g = jnp.sum(jnp.where((lane == e) | (lane == e + N_EXPERTS), ext, 0.0), axis=1, keepdims=True)
        gate = jnp.dot(h, wg_ref[0], preferred_element_type=F32)
        up = jnp.dot(h, wu_ref[0], preferred_element_type=F32)
        act = gate * jax.nn.sigmoid(gate) * up * g
        o_ref[...] = jnp.dot(act.astype(BF16), wd_ref[0], preferred_element_type=F32).astype(o_ref.dtype)

    @pl.when(b >= nu_ref[0])
    def _():
        o_ref[...] = jnp.zeros_like(o_ref)


def _combine_kernel(base_ref, n_ref, x_ref, st_ref, ys_ref, o_ref, ybuf, sem):
    i = pl.program_id(0)

    def copy(e, half):
        row = pl.multiple_of(base_ref[i * N_EXPERTS + e] + half * MOE_HALF, MOE_ALIGN)
        return pltpu.make_async_copy(ys_ref.at[pl.ds(row, MOE_HALF)], ybuf.at[e, half],
                                     sem.at[e, half])

    for e in range(N_EXPERTS):
        copy(e, 0).start()
        pl.when(n_ref[i * N_EXPERTS + e] > MOE_HALF)(lambda e=e: copy(e, 1).start())

    st = st_ref[...]
    rank = _tile_rank(st)
    o_ref[...] = x_ref[...]

    def gather(e, half):
        copy(e, half).wait()
        p = _one_hot_rows(rank, st, e, half)
        o_ref[...] += lax.dot_general(p, ybuf[e, half], (((0,), (0,)), ((), ())),
                                      preferred_element_type=F32)

    for e in range(N_EXPERTS):
        gather(e, 0)
        pl.when(n_ref[i * N_EXPERTS + e] > MOE_HALF)(functools.partial(gather, e, 1))


def _moe_layer(x, g_norm, w_router, w_gu, w_down):
    m, d = x.shape
    ff = w_down.shape[1]
    nt = m // MOE_TM
    i32 = jnp.int32
    hn, ct, st, cnt = pl.pallas_call(
        _route_kernel,
        grid=(nt,),
        in_specs=[
            pl.BlockSpec((MOE_TM, d), lambda i: (i, 0)),
            pl.BlockSpec((1, d), lambda i: (0, 0)),
            pl.BlockSpec((N_EXPERTS, d), lambda i: (0, 0)),
        ],
        out_specs=[
            pl.BlockSpec((MOE_TM, d), lambda i: (i, 0)),
            pl.BlockSpec((N_EXPERTS, MOE_TM), lambda i: (0, i)),
            pl.BlockSpec((N_EXPERTS, MOE_TM), lambda i: (0, i)),
            pl.BlockSpec((1, N_EXPERTS, LANES), lambda i: (i, 0, 0)),
        ],
        out_shape=[
            jax.ShapeDtypeStruct((m, d), BF16),
            jax.ShapeDtypeStruct((N_EXPERTS, m), F32),
            jax.ShapeDtypeStruct((N_EXPERTS, m), F32),
            jax.ShapeDtypeStruct((nt, N_EXPERTS, LANES), F32),
        ],
        compiler_params=_params("parallel"),
        name="moe_route",
    )(x, g_norm.reshape(1, d), w_router.T.astype(F32))

    n = cnt[:, :, 0].astype(i32)
    chunk = (n + MOE_ALIGN - 1) // MOE_ALIGN * MOE_ALIGN
    seg = (jnp.sum(chunk, axis=0) + MOE_HALF + MOE_TR - 1) // MOE_TR * MOE_TR
    seg_off = jnp.cumsum(seg) - seg
    base = (seg_off[None, :] + jnp.cumsum(chunk, axis=0) - chunk).reshape(-1).astype(i32)
    n_flat = n.reshape(-1)
    bound = 2 * m + nt * N_EXPERTS * (MOE_ALIGN - 1) + N_EXPERTS * (MOE_HALF + MOE_TR - 1)
    nb = -(-bound // MOE_TR)
    blk_end = jnp.cumsum(seg // MOE_TR)
    blk_expert = jnp.minimum(jnp.searchsorted(blk_end, jnp.arange(nb, dtype=i32), side='right'),
                             N_EXPERTS - 1).astype(i32)
    n_used = blk_end[-1:].astype(i32)
    rows = nb * MOE_TR

    tile_spec = lambda shape, imap: pl.BlockSpec(shape, imap)
    xs = pl.pallas_call(
        _dispatch_kernel,
        grid_spec=pltpu.PrefetchScalarGridSpec(
            num_scalar_prefetch=2,
            grid=(nt,),
            in_specs=[
                tile_spec((MOE_TM, d), lambda i, b, c: (i, 0)),
                tile_spec((N_EXPERTS, MOE_TM), lambda i, b, c: (0, i)),
                tile_spec((N_EXPERTS, MOE_TM), lambda i, b, c: (0, i)),
                pl.BlockSpec(memory_space=pl.ANY),
            ],
            out_specs=pl.BlockSpec(memory_space=pl.ANY),
            scratch_shapes=[
                pltpu.VMEM((N_EXPERTS, 2, MOE_HALF, XS_W), BF16),
                pltpu.SemaphoreType.DMA((N_EXPERTS, 2)),
            ],
        ),
        out_shape=jax.ShapeDtypeStruct((rows, XS_W), BF16),
        input_output_aliases={5: 0},
        compiler_params=_params("arbitrary"),
        name="moe_dispatch",
    )(base, n_flat, hn, ct, st, jnp.zeros((rows, XS_W), BF16))

    ys = pl.pallas_call(
        _expert_kernel,
        grid_spec=pltpu.PrefetchScalarGridSpec(
            num_scalar_prefetch=2,
            grid=(nb,),
            in_specs=[
                pl.BlockSpec((MOE_TR, XS_W), lambda b, be, nu: (b, 0)),
                pl.BlockSpec((1, d, ff), lambda b, be, nu: (be[b], 0, 0)),
                pl.BlockSpec((1, d, ff), lambda b, be, nu: (be[b], 0, 1)),
                pl.BlockSpec((1, ff, d), lambda b, be, nu: (be[b], 0, 0)),
            ],
            out_specs=pl.BlockSpec((MOE_TR, d), lambda b, be, nu: (b, 0)),
        ),
        out_shape=jax.ShapeDtypeStruct((rows, d), BF16),
        compiler_params=_params("arbitrary"),
        name="moe_experts",
    )(blk_expert, n_used, xs, w_gu, w_gu, w_down)

    return pl.pallas_call(
        _combine_kernel,
        grid_spec=pltpu.PrefetchScalarGridSpec(
            num_scalar_prefetch=2,
            grid=(nt,),
            in_specs=[
                tile_spec((MOE_TM, d), lambda i, b, c: (i, 0)),
                tile_spec((N_EXPERTS, MOE_TM), lambda i, b, c: (0, i)),
                pl.BlockSpec(memory_space=pl.ANY),
            ],
            out_specs=tile_spec((MOE_TM, d), lambda i, b, c: (i, 0)),
            scratch_shapes=[
                pltpu.VMEM((N_EXPERTS, 2, MOE_HALF, d), BF16),
                pltpu.SemaphoreType.DMA((N_EXPERTS, 2)),
            ],
        ),
        out_shape=jax.ShapeDtypeStruct((m, d), F32),
        compiler_params=_params("arbitrary"),
        name="moe_combine",
    )(base, n_flat, x, st, ys)


def _to_frame_major(x, bsz, seq):
    d = x.shape[-1]
    return x.reshape(bsz * (seq // CHUNK), CHUNK, d).transpose(1, 0, 2).reshape(bsz * seq, d)


def _from_frame_major(x, bsz, seq):
    d = x.shape[-1]
    return x.reshape(CHUNK, bsz * (seq // CHUNK), d).transpose(1, 0, 2).reshape(bsz * seq, d)


def kernel(x, mem, norm_mix, norm_mem, norm_ffn, mem_norm, final_norm, s5_w_in, s5_lam_re, s5_lam_im, s5_log_dt, s5_b_re, s5_b_im, s5_c_re, s5_c_im, s5_d, s5_w_glu, att_w_qkv, att_w_o, att_rel_bias, lru_w_in, lru_conv_w, lru_conv_b, lru_w_a, lru_b_a, lru_w_x, lru_b_x, lru_lam, lru_w_out, mem_w_q, mem_w_kv, mem_w_o, ffn_w_gu, ffn_w_down, moe_w_router, moe_w_gu, moe_w_down):
    bsz, seq, d = x.shape
    x = x.reshape(bsz * seq, d).astype(F32)

    w_kv_all = jnp.concatenate([mem_w_kv[i] for i in range(DEPTH)], axis=1).astype(BF16)
    kv_all = _norm_linear(mem.reshape(bsz * N_MEM, d).astype(F32), mem_norm, w_kv_all, BF16,
                          tm=bsz * N_MEM, tn=2 * d)

    for i in range(DEPTH):
        kind, j = i % 3, i // 3
        if kind == 0:
            xp = _to_frame_major(x, bsz, seq)
            xp = _s5_layer(xp, norm_mix[i], s5_w_in[j], s5_lam_re[j], s5_lam_im[j], s5_log_dt[j],
                           s5_b_re[j], s5_b_im[j], s5_c_re[j], s5_c_im[j], s5_d[j], s5_w_glu[j],
                           bsz, seq)
            x = _from_frame_major(xp, bsz, seq)
        elif kind == 1:
            x = _attn_layer(x, norm_mix[i], att_w_qkv[j], att_w_o[j], att_rel_bias[j], bsz, seq)
        else:
            x = _lru_layer(x, norm_mix[i], lru_w_in[j], lru_conv_w[j], lru_conv_b[j], lru_w_a[j],
                           lru_b_a[j], lru_w_x[j], lru_b_x[j], lru_lam[j], lru_w_out[j], bsz, seq)
        x = _memattn_layer(x, norm_mem[i], mem_w_q[i], kv_all, i, mem_w_o[i], bsz, seq)
        if i % 2 == 0:
            x = _ffn_layer(x, norm_ffn[i], ffn_w_gu[i // 2].astype(BF16),
                           ffn_w_down[i // 2].astype(BF16), tf=1408)
        else:
            x = _moe_layer(x, norm_ffn[i], moe_w_router[i // 2], moe_w_gu[i // 2].astype(BF16),
                           moe_w_down[i // 2].astype(BF16))
    return _norm(x, final_norm).reshape(bsz, seq, d)
```

```python
import functools
import math

import jax
import jax.numpy as jnp
from jax import lax
from jax.experimental import pallas as pl
from jax.experimental.pallas import tpu as pltpu

F32 = jnp.float32
BF16 = jnp.bfloat16
HIGHEST = lax.Precision.HIGHEST

D_MODEL = 1024
DEPTH = 4
CHUNK = 64
N_MEM = 256
EPS = 1e-6

S5_GROUP = 16
S5_GROUPS = D_MODEL // S5_GROUP
S5_STATE = 64

ATT_HEADS = 16
ATT_HEAD_DIM = D_MODEL // ATT_HEADS
LEFT_CHUNKS = 8
MAX_REL = 128
ATT_TQ = 4 * CHUNK
ATT_LEFT = LEFT_CHUNKS * CHUNK
ATT_TK = ATT_LEFT + ATT_TQ
NEG_INF = -1e30

LRU_BLOCKS = 8
LRU_BLOCK_W = D_MODEL // LRU_BLOCKS
CONV_W = 4
LRU_C = 8.0
LRU_T = 512
SUBLANES = 8
LANES = 128

MEM_HEADS = 4
MEM_HEAD_DIM = D_MODEL // MEM_HEADS

N_EXPERTS = 8

VMEM_LIMIT = 56 * 1024 * 1024


def _params(*sem):
    return pltpu.CompilerParams(dimension_semantics=sem, vmem_limit_bytes=VMEM_LIMIT)


def _rms(x, g):
    ms = jnp.mean(x * x, axis=-1, keepdims=True)
    return x * lax.rsqrt(ms + EPS) * g


def _norm_linear_kernel(x_ref, g_ref, w_ref, o_ref, hn_ref):
    @pl.when(pl.program_id(1) == 0)
    def _():
        hn_ref[...] = _rms(x_ref[...], g_ref[...]).astype(BF16)

    o_ref[...] = jnp.dot(hn_ref[...], w_ref[...], preferred_element_type=F32).astype(o_ref.dtype)


def _norm_linear(x, g, w, out_dtype, tm=512, tn=None):
    m, d = x.shape
    n = w.shape[1]
    tn = n if tn is None else tn
    return pl.pallas_call(
        _norm_linear_kernel,
        grid=(m // tm, n // tn),
        in_specs=[
            pl.BlockSpec((tm, d), lambda i, j: (i, 0)),
            pl.BlockSpec((1, d), lambda i, j: (0, 0)),
            pl.BlockSpec((d, tn), lambda i, j: (0, j)),
        ],
        out_specs=pl.BlockSpec((tm, tn), lambda i, j: (i, j)),
        out_shape=jax.ShapeDtypeStruct((m, n), out_dtype),
        scratch_shapes=[pltpu.VMEM((tm, d), BF16)],
        compiler_params=_params("parallel", "arbitrary"),
        name="norm_linear",
    )(x, g.reshape(1, d), w)


def _norm_linear_t_kernel(x_ref, g_ref, wt_ref, o_ref):
    hn = _rms(x_ref[...], g_ref[...]).astype(BF16)
    o_ref[...] = lax.dot_general(
        wt_ref[...], hn, (((1,), (1,)), ((), ())), preferred_element_type=F32
    ).astype(o_ref.dtype)


def _norm_linear_t(x, g, wt, out_dtype, tm=512):
    m, d = x.shape
    n = wt.shape[0]
    return pl.pallas_call(
        _norm_linear_t_kernel,
        grid=(m // tm,),
        in_specs=[
            pl.BlockSpec((tm, d), lambda i: (i, 0)),
            pl.BlockSpec((1, d), lambda i: (0, 0)),
            pl.BlockSpec((n, d), lambda i: (0, 0)),
        ],
        out_specs=pl.BlockSpec((n, tm), lambda i: (0, i)),
        out_shape=jax.ShapeDtypeStruct((n, m), out_dtype),
        compiler_params=_params("parallel"),
        name="norm_linear_t",
    )(x, g.reshape(1, d), wt)


def _s5_operators(lam_re, lam_im, log_dt, b_re, b_im, c_re, c_im, d_skip, n_chunks):
    g, n, p = S5_GROUPS, S5_STATE, S5_GROUP
    lr = lam_re.astype(F32)
    li = lam_im.astype(F32)
    dt = jnp.exp(log_dt.astype(F32))[:, None]
    mag = jnp.exp(lr * dt)
    ar = mag * jnp.cos(li * dt)
    ai = mag * jnp.sin(li * dt)
    den = lr * lr + li * li
    fr = ((ar - 1.0) * lr + ai * li) / den
    fi = (ai * lr - (ar - 1.0) * li) / den
    bbr = fr[..., None] * b_re - fi[..., None] * b_im
    bbi = fr[..., None] * b_im + fi[..., None] * b_re

    def powers(ks):
        kk = ks.astype(F32)[:, None, None]
        m = jnp.exp(kk * (lr * dt))
        return m * jnp.cos(kk * (li * dt)), m * jnp.sin(kk * (li * dt))

    pr, pi = powers(jnp.arange(CHUNK + 1))
    lcr = c_re[None] * pr[:, :, None, :] - c_im[None] * pi[:, :, None, :]
    lci = c_re[None] * pi[:, :, None, :] + c_im[None] * pr[:, :, None, :]
    kern = (jnp.einsum('tgqn,gnp->gtqp', lcr[:CHUNK], bbr, precision=HIGHEST)
            - jnp.einsum('tgqn,gnp->gtqp', lci[:CHUNK], bbi, precision=HIGHEST))
    t_idx = jnp.arange(CHUNK)
    kern = kern.at[:, 0].add(d_skip.reshape(g, p)[:, :, None] * jnp.eye(p, dtype=F32))
    rev = kern.transpose(0, 2, 3, 1)[..., ::-1]
    odd = rev[:, :, 1::2]
    even = jnp.roll(rev[:, :, 0::2], -1, axis=2)
    toe = jnp.concatenate([odd, even], axis=-1).reshape(g, p, p * CHUNK)
    rev_r = pr[CHUNK - 1 - t_idx]
    rev_i = pi[CHUNK - 1 - t_idx]
    bm_r = jnp.einsum('sgn,gnp->gnps', rev_r, bbr) - jnp.einsum('sgn,gnp->gnps', rev_i, bbi)
    bm_i = jnp.einsum('sgn,gnp->gnps', rev_r, bbi) + jnp.einsum('sgn,gnp->gnps', rev_i, bbr)
    bm = jnp.concatenate([bm_r, bm_i], axis=1).reshape(g, 2 * n, p * CHUNK).astype(BF16)
    cinj = jnp.concatenate([lcr[1:], -lci[1:]], axis=-1)
    cinj = cinj.transpose(1, 2, 0, 3).reshape(g, p * CHUNK, 2 * n).astype(BF16)
    qr, qi = [pr[CHUNK]], [pi[CHUNK]]
    for _ in range(int(math.log2(n_chunks)) - 1):
        qr, qi = qr + [qr[-1] * qr[-1] - qi[-1] * qi[-1]], qi + [2.0 * qr[-1] * qi[-1]]
    levels = len(qr)
    qr, qi = jnp.stack(qr), jnp.stack(qi)
    m1 = jnp.concatenate([qr, qr], axis=-1)
    m2 = jnp.concatenate([-qi, qi], axis=-1)
    ap = jnp.stack([m1, m2], axis=-1).transpose(1, 2, 0, 3).reshape(g, 2 * n, 2 * levels)
    return toe, bm, cinj, ap


def _ssm_kernel(u_ref, seed_ref, bm_ref, cinj_ref, ap_ref, o_ref, toe_ref, *, n_chunks):
    n = S5_STATE
    u = u_ref[0]
    rows = S5_GROUP * CHUNK
    causal = (lax.broadcasted_iota(jnp.int32, (CHUNK, rows), 1) % CHUNK
              <= lax.broadcasted_iota(jnp.int32, (CHUNK, rows), 0))
    for q in range(S5_GROUP):
        seed = jnp.broadcast_to(seed_ref[0, q:q + 1, :], (CHUNK, rows))
        rolled = pltpu.roll(seed, 1, 1, stride=1, stride_axis=0)
        toe_ref[q * CHUNK:(q + 1) * CHUNK, :] = jnp.where(causal, rolled, 0.0).astype(BF16)
    y = jnp.dot(toe_ref[...], u, preferred_element_type=F32)
    x = jnp.dot(bm_ref[0], u, preferred_element_type=F32)
    ap = ap_ref[0]
    col = lax.broadcasted_iota(jnp.int32, x.shape, 1) % n_chunks
    for k in range(int(math.log2(n_chunks))):
        sh = 1 << k
        xs = jnp.where(col >= sh, pltpu.roll(x, sh, 1), 0.0)
        xsw = jnp.concatenate([xs[n:], xs[:n]], axis=0)
        x = x + ap[:, 2 * k:2 * k + 1] * xs + ap[:, 2 * k + 1:2 * k + 2] * xsw
    h0 = jnp.where(col >= 1, pltpu.roll(x, 1, 1), 0.0)
    h0_hi = h0.astype(BF16)
    h0_lo = (h0 - h0_hi.astype(F32)).astype(BF16)
    cinj = cinj_ref[0]
    y = y + jnp.dot(cinj, h0_hi, preferred_element_type=F32)
    y = y + jnp.dot(cinj, h0_lo, preferred_element_type=F32)
    o_ref[0] = jax.nn.gelu(y).astype(o_ref.dtype)


def _glu_kernel(vt_ref, w_ref, x_ref, o_ref):
    z = lax.dot_general(vt_ref[...], w_ref[...], (((0,), (0,)), ((), ())),
                        preferred_element_type=F32)
    d = o_ref.shape[1]
    o_ref[...] = x_ref[...] + z[:, :d] * jax.nn.sigmoid(z[:, d:])


def _s5_layer(x, g_norm, w_in, lam_re, lam_im, log_dt, b_re, b_im, c_re, c_im, d_skip, w_glu,
              bsz, seq):
    m, d = x.shape
    n_chunks = seq // CHUNK
    ncol = bsz * n_chunks
    toe, bm, cinj, ap = _s5_operators(lam_re, lam_im, log_dt, b_re, b_im, c_re, c_im, d_skip,
                                      n_chunks)
    ut = _norm_linear_t(x, g_norm, w_in.T.astype(BF16), BF16)
    rows = S5_GROUP * CHUNK
    ut = ut.reshape(S5_GROUPS, rows, ncol)
    vt = pl.pallas_call(
        functools.partial(_ssm_kernel, n_chunks=n_chunks),
        grid=(S5_GROUPS,),
        in_specs=[
            pl.BlockSpec((1, rows, ncol), lambda g: (g, 0, 0)),
            pl.BlockSpec((1, S5_GROUP, rows), lambda g: (g, 0, 0)),
            pl.BlockSpec((1, 2 * S5_STATE, rows), lambda g: (g, 0, 0)),
            pl.BlockSpec((1, rows, 2 * S5_STATE), lambda g: (g, 0, 0)),
            pl.BlockSpec((1,) + ap.shape[1:], lambda g: (g, 0, 0)),
        ],
        out_specs=pl.BlockSpec((1, rows, ncol), lambda g: (g, 0, 0)),
        out_shape=jax.ShapeDtypeStruct((S5_GROUPS, rows, ncol), BF16),
        scratch_shapes=[pltpu.VMEM((rows, rows), BF16)],
        compiler_params=_params("parallel"),
        name="s5_ssm",
    )(ut, toe, bm, cinj, ap)
    vt = vt.reshape(d, m)
    tm = 512
    return pl.pallas_call(
        _glu_kernel,
        grid=(m // tm,),
        in_specs=[
            pl.BlockSpec((d, tm), lambda i: (0, i)),
            pl.BlockSpec((d, 2 * d), lambda i: (0, 0)),
            pl.BlockSpec((tm, d), lambda i: (i, 0)),
        ],
        out_specs=pl.BlockSpec((tm, d), lambda i: (i, 0)),
        out_shape=jax.ShapeDtypeStruct((m, d), F32),
        compiler_params=_params("parallel"),
        name="s5_glu",
    )(vt, w_glu.astype(BF16), x)


ATT_PERIOD = ATT_TQ + ATT_TK


def _attn_bias_seed(rel_bias):
    k = jnp.arange(ATT_PERIOD)
    rel = jnp.where(k < ATT_TK, ATT_LEFT - k, ATT_LEFT + ATT_PERIOD - k)
    return jnp.take(rel_bias.astype(F32), jnp.clip(rel, -MAX_REL, MAX_REL) + MAX_REL, axis=1)


def _attn_kernel(x_ref, q_ref, k0_ref, k1_ref, k2_ref, v0_ref, v1_ref, v2_ref, seed_ref, wo_ref,
                 o_ref, bias_ref, ocat_ref):
    i = pl.program_id(1)

    @pl.when((pl.program_id(0) == 0) & (i == 0))
    def _():
        r = lax.broadcasted_iota(jnp.int32, (ATT_TQ, ATT_TK), 0)
        j = lax.broadcasted_iota(jnp.int32, (ATT_TQ, ATT_TK), 1)
        dchunk = (r + ATT_LEFT) // CHUNK - j // CHUNK
        valid = (dchunk >= 0) & (dchunk <= LEFT_CHUNKS)
        for h in range(ATT_HEADS):
            seed = jnp.broadcast_to(seed_ref[h:h + 1, :], (ATT_TQ, ATT_PERIOD))
            rolled = pltpu.roll(seed, 0, 1, stride=1, stride_axis=0)
            bias_ref[h] = jnp.where(valid, rolled[:, :ATT_TK], NEG_INF)

    lane = lax.broadcasted_iota(jnp.int32, (1, LANES), 1)
    col = lax.broadcasted_iota(jnp.int32, (ATT_TQ, ATT_TK), 1)
    pad = col < ATT_LEFT - i * ATT_TQ
    scale = ATT_HEAD_DIM ** -0.5
    heads_per = LANES // ATT_HEAD_DIM
    for hp in range(D_MODEL // LANES):
        sl = slice(hp * LANES, (hp + 1) * LANES)
        q = q_ref[:, sl] * scale
        k = jnp.concatenate([k0_ref[:, sl], k1_ref[:, sl], k2_ref[:, sl]], axis=0)
        v = jnp.concatenate([v0_ref[:, sl], v1_ref[:, sl], v2_ref[:, sl]], axis=0)
        acc = jnp.zeros((ATT_TQ, LANES), F32)
        for h in range(heads_per):
            in_head = (lane >= h * ATT_HEAD_DIM) & (lane < (h + 1) * ATT_HEAD_DIM)
            qh = jnp.where(in_head, q, jnp.zeros_like(q))
            s = lax.dot_general(qh, k, (((1,), (1,)), ((), ())), preferred_element_type=F32)
            s = jnp.where(pad, NEG_INF, s + bias_ref[hp * heads_per + h])
            mx = jnp.max(s, axis=-1, keepdims=True)
            p = jnp.exp(s - mx)
            l = jnp.sum(p, axis=-1, keepdims=True)
            vh = jnp.where(in_head, v, jnp.zeros_like(v))
            acc = acc + jnp.dot(p.astype(BF16), vh, preferred_element_type=F32) / l
        ocat_ref[:, sl] = acc.astype(BF16)
    o_ref[...] = x_ref[...] + jnp.dot(ocat_ref[...], wo_ref[...], preferred_element_type=F32)


def _attn_layer(x, g_norm, w_qkv, w_o, rel_bias, bsz, seq):
    m, d = x.shape
    qkv = _norm_linear(x, g_norm, w_qkv.astype(BF16), BF16)
    nq = seq // ATT_TQ

    def rows(col, back):
        return lambda b, i: (b * nq + jnp.maximum(i - back, 0), col)

    blk = (ATT_TQ, d)
    nkb = ATT_TK // ATT_TQ
    return pl.pallas_call(
        _attn_kernel,
        grid=(bsz, nq),
        in_specs=(
            [pl.BlockSpec(blk, rows(0, 0)), pl.BlockSpec(blk, rows(0, 0))]
            + [pl.BlockSpec(blk, rows(1, nkb - 1 - kb)) for kb in range(nkb)]
            + [pl.BlockSpec(blk, rows(2, nkb - 1 - kb)) for kb in range(nkb)]
            + [pl.BlockSpec((ATT_HEADS, ATT_PERIOD), lambda b, i: (0, 0)),
               pl.BlockSpec((d, d), lambda b, i: (0, 0))]
        ),
        out_specs=pl.BlockSpec(blk, rows(0, 0)),
        out_shape=jax.ShapeDtypeStruct((m, d), F32),
        scratch_shapes=[pltpu.VMEM((ATT_HEADS, ATT_TQ, ATT_TK), F32), pltpu.VMEM((ATT_TQ, d), BF16)],
        compiler_params=_params("arbitrary", "arbitrary"),
        name="chunk_attn",
    )(x, *([qkv] * (1 + 2 * nkb)), _attn_bias_seed(rel_bias), w_o.astype(BF16))


def _lru_kernel(x_ref, g_ref, win_ref, cw_ref, cb_ref, wax_ref, ba_ref, bx_ref, lam_ref, wout_ref,
                o_ref, prev_ref, carry_ref, a_s, b_s):
    t, w = x_ref.shape

    @pl.when(pl.program_id(1) == 0)
    def _():
        prev_ref[...] = jnp.zeros_like(prev_ref)
        carry_ref[...] = jnp.zeros_like(carry_ref)

    x = x_ref[...]
    z = jnp.dot(_rms(x, g_ref[...]).astype(BF16), win_ref[...], preferred_element_type=F32)
    gate = z[:, :w]
    xr = z[:, w:]
    xext = jnp.concatenate([prev_ref[...], xr], axis=0)
    cw = cw_ref[...]
    xc = cb_ref[...] + cw[0:1] * xext[SUBLANES - 3:SUBLANES - 3 + t]
    for k in range(1, CONV_W):
        xc = xc + cw[k:k + 1] * xext[SUBLANES - 3 + k:SUBLANES - 3 + k + t]
    prev_ref[...] = xr[t - SUBLANES:]

    xcb = xc.astype(BF16)
    pre_a, pre_x = [], []
    for blk in range(LRU_BLOCKS):
        pre = jnp.dot(xcb[:, blk * LRU_BLOCK_W:(blk + 1) * LRU_BLOCK_W], wax_ref[blk],
                      preferred_element_type=F32)
        pre_a.append(pre[:, :LRU_BLOCK_W])
        pre_x.append(pre[:, LRU_BLOCK_W:])
    r = jax.nn.sigmoid(jnp.concatenate(pre_a, axis=1) + ba_ref[...])
    ig = jax.nn.sigmoid(jnp.concatenate(pre_x, axis=1) + bx_ref[...])
    z = -lam_ref[...]
    softplus = jnp.maximum(z, 0.0) + jnp.log1p(jnp.exp(-jnp.abs(z)))
    log_a = -LRU_C * r * softplus
    a = jnp.exp(log_a)
    b = jnp.sqrt(1.0 - a * a) * (ig * xc)

    a3 = a.reshape(t // SUBLANES, SUBLANES, w)
    b3 = b.reshape(t // SUBLANES, SUBLANES, w)
    sub = lax.broadcasted_iota(jnp.int32, a3.shape, 1)
    for sh in (1, 2, 4):
        a_sh = jnp.where(sub >= sh, pltpu.roll(a3, sh, 1), 1.0)
        b_sh = jnp.where(sub >= sh, pltpu.roll(b3, sh, 1), 0.0)
        b3 = a3 * b_sh + b3
        a3 = a3 * a_sh
    a_s[...] = a3
    b_s[...] = b3

    def body(j, carry):
        hb = b_s[j] + a_s[j] * carry
        b_s[j] = hb
        return hb[SUBLANES - 1:SUBLANES, :]

    carry = lax.fori_loop(0, t // SUBLANES, body, carry_ref[0:1, :])
    carry_ref[0:1, :] = carry
    h = b_s[...].reshape(t, w)
    y = (jax.nn.gelu(gate) * h).astype(BF16)
    o_ref[...] = x + jnp.dot(y, wout_ref[...], preferred_element_type=F32)


def _lru_layer(x, g_norm, w_in, conv_w, conv_b, w_a, b_a, w_x, b_x, lam, w_out, bsz, seq):
    m, d = x.shape
    wax = jnp.concatenate([w_a, w_x], axis=-1).astype(BF16)
    nt = seq // LRU_T
    row = lambda v: v.reshape(1, d).astype(F32)
    vec = pl.BlockSpec((1, d), lambda b, i: (0, 0))
    return pl.pallas_call(
        _lru_kernel,
        grid=(bsz, nt),
        in_specs=[
            pl.BlockSpec((LRU_T, d), lambda b, i: (b * nt + i, 0)),
            vec,
            pl.BlockSpec((d, 2 * d), lambda b, i: (0, 0)),
            pl.BlockSpec((CONV_W, d), lambda b, i: (0, 0)),
            vec,
            pl.BlockSpec(wax.shape, lambda b, i: (0, 0, 0)),
            vec, vec, vec,
            pl.BlockSpec((d, d), lambda b, i: (0, 0)),
        ],
        out_specs=pl.BlockSpec((LRU_T, d), lambda b, i: (b * nt + i, 0)),
        out_shape=jax.ShapeDtypeStruct((m, d), F32),
        scratch_shapes=[
            pltpu.VMEM((SUBLANES, d), F32),
            pltpu.VMEM((SUBLANES, d), F32),
            pltpu.VMEM((LRU_T // SUBLANES, SUBLANES, d), F32),
            pltpu.VMEM((LRU_T // SUBLANES, SUBLANES, d), F32),
        ],
        compiler_params=_params("parallel", "arbitrary"),
        name="rglru",
    )(x, row(g_norm), w_in.astype(BF16), conv_w.astype(F32), row(conv_b), wax, row(b_a), row(b_x),
      row(lam), w_out.astype(BF16))


def _memattn_kernel(x_ref, g_ref, wq_ref, k_ref, v_ref, wo_ref, o_ref, ocat_ref):
    x = x_ref[...]
    hn = _rms(x, g_ref[...]).astype(BF16)
    q = jnp.dot(hn, wq_ref[...], preferred_element_type=F32) * (MEM_HEAD_DIM ** -0.5)
    q = q.astype(BF16)
    for h in range(MEM_HEADS):
        sl = slice(h * MEM_HEAD_DIM, (h + 1) * MEM_HEAD_DIM)
        s = lax.dot_general(q[:, sl], k_ref[:, sl], (((1,), (1,)), ((), ())),
                            preferred_element_type=F32)
        mx = jnp.max(s, axis=-1, keepdims=True)
        p = jnp.exp(s - mx)
        l = jnp.sum(p, axis=-1, keepdims=True)
        o = jnp.dot(p.astype(BF16), v_ref[:, sl], preferred_element_type=F32) / l
        ocat_ref[:, sl] = o.astype(BF16)
    o_ref[...] = x + jnp.dot(ocat_ref[...], wo_ref[...], preferred_element_type=F32)


def _memattn_layer(x, g_norm, w_q, kv_all, layer, w_o, bsz, seq, tm=512):
    m, d = x.shape
    nt = seq // tm
    wspec = pl.BlockSpec((d, d), lambda b, i: (0, 0))
    return pl.pallas_call(
        _memattn_kernel,
        grid=(bsz, nt),
        in_specs=[
            pl.BlockSpec((tm, d), lambda b, i: (b * nt + i, 0)),
            pl.BlockSpec((1, d), lambda b, i: (0, 0)),
            wspec,
            pl.BlockSpec((N_MEM, d), lambda b, i: (b, 2 * layer)),
            pl.BlockSpec((N_MEM, d), lambda b, i: (b, 2 * layer + 1)),
            wspec,
        ],
        out_specs=pl.BlockSpec((tm, d), lambda b, i: (b * nt + i, 0)),
        out_shape=jax.ShapeDtypeStruct((m, d), F32),
        scratch_shapes=[pltpu.VMEM((tm, d), BF16)],
        compiler_params=_params("parallel", "parallel"),
        name="mem_attn",
    )(x, g_norm.reshape(1, d), w_q.astype(BF16), kv_all, kv_all, w_o.astype(BF16))


def _ffn_kernel(x_ref, g_ref, wg_ref, wu_ref, wd_ref, o_ref, hn_ref, acc_ref):
    f = pl.program_id(1)

    @pl.when(f == 0)
    def _():
        x = x_ref[...]
        hn_ref[...] = _rms(x, g_ref[...]).astype(BF16)
        acc_ref[...] = x

    hn = hn_ref[...]
    gate = jnp.dot(hn, wg_ref[...], preferred_element_type=F32)
    up = jnp.dot(hn, wu_ref[...], preferred_element_type=F32)
    act = gate * jax.nn.sigmoid(gate) * up
    acc_ref[...] += jnp.dot(act.astype(BF16), wd_ref[...], preferred_element_type=F32)

    @pl.when(f == pl.num_programs(1) - 1)
    def _():
        o_ref[...] = acc_ref[...]


def _ffn_layer(x, g_norm, w_gu, w_down, tf, tm=512):
    m, d = x.shape
    ff = w_down.shape[0]
    nf = ff // tf
    return pl.pallas_call(
        _ffn_kernel,
        grid=(m // tm, nf),
        in_specs=[
            pl.BlockSpec((tm, d), lambda i, f: (i, 0)),
            pl.BlockSpec((1, d), lambda i, f: (0, 0)),
            pl.BlockSpec((d, tf), lambda i, f: (0, f)),
            pl.BlockSpec((d, tf), lambda i, f: (0, nf + f)),
            pl.BlockSpec((tf, d), lambda i, f: (f, 0)),
        ],
        out_specs=pl.BlockSpec((tm, d), lambda i, f: (i, 0)),
        out_shape=jax.ShapeDtypeStruct((m, d), F32),
        scratch_shapes=[pltpu.VMEM((tm, d), BF16), pltpu.VMEM((tm, d), F32)],
        compiler_params=_params("parallel", "arbitrary"),
        name="ffn",
    )(x, g_norm.reshape(1, d), w_gu, w_gu, w_down)


MOE_TM = 512
MOE_TR = 512
MOE_HALF = MOE_TM // 2
MOE_ALIGN = 16
XS_W = D_MODEL + LANES


def _route_kernel(x_ref, g_ref, wrt_ref, hn_ref, ct_ref, st_ref, cnt_ref):
    hn = _rms(x_ref[...], g_ref[...])
    hn_ref[...] = hn.astype(BF16)
    lg = lax.dot_general(wrt_ref[...], hn, (((1,), (1,)), ((), ())),
                         preferred_element_type=F32, precision=HIGHEST)
    idx = lax.broadcasted_iota(jnp.int32, lg.shape, 0)
    m1 = jnp.max(lg, axis=0, keepdims=True)
    i1 = jnp.min(jnp.where(lg == m1, idx, N_EXPERTS), axis=0, keepdims=True)
    oh1 = idx == i1
    lg2 = jnp.where(oh1, -jnp.inf, lg)
    m2 = jnp.max(lg2, axis=0, keepdims=True)
    i2 = jnp.min(jnp.where(lg2 == m2, idx, N_EXPERTS), axis=0, keepdims=True)
    oh2 = idx == i2
    e2 = jnp.exp(m2 - m1)
    g1 = 1.0 / (1.0 + e2)
    g2 = e2 / (1.0 + e2)
    ct_ref[...] = jnp.where(oh1, g1, 0.0) + jnp.where(oh2, g2, 0.0)
    sel = (oh1 | oh2).astype(F32)
    st_ref[...] = sel
    cnt_ref[0] = jnp.broadcast_to(jnp.sum(sel, axis=1, keepdims=True), (N_EXPERTS, LANES))


def _tile_rank(st):
    tm = st.shape[1]
    before = (lax.broadcasted_iota(jnp.int32, (tm, tm), 0)
              < lax.broadcasted_iota(jnp.int32, (tm, tm), 1))
    return jnp.dot(st.astype(BF16), before.astype(BF16), preferred_element_type=F32)


def _one_hot_rows(rank, st, e, half):
    tm = rank.shape[1]
    rows = lax.broadcasted_iota(jnp.int32, (MOE_HALF, tm), 0).astype(F32) + float(half * MOE_HALF)
    hit = (rank[e:e + 1] == rows) & (st[e:e + 1] > 0.0)
    return jnp.where(hit, 1.0, 0.0).astype(BF16)


def _dispatch_kernel(base_ref, n_ref, hn_ref, ct_ref, st_ref, xs_in_ref, xs_ref, stage, sem):
    del xs_in_ref
    i = pl.program_id(0)
    hn = hn_ref[...]
    ct = ct_ref[...]
    st = st_ref[...]
    rank = _tile_rank(st)
    g_hi = ct.astype(BF16)
    g_lo = (ct - g_hi.astype(F32)).astype(BF16)
    gm = jnp.concatenate(
        [g_hi, g_lo, jnp.zeros((LANES - 2 * N_EXPERTS, ct.shape[1]), BF16)], axis=0)

    def copy(e, half):
        row = pl.multiple_of(base_ref[i * N_EXPERTS + e] + half * MOE_HALF, MOE_ALIGN)
        return pltpu.make_async_copy(stage.at[e, half], xs_ref.at[pl.ds(row, MOE_HALF)],
                                     sem.at[e, half])

    def fill(e, half):
        p = _one_hot_rows(rank, st, e, half)
        stage[e, half, :, :D_MODEL] = jnp.dot(p, hn, preferred_element_type=F32).astype(BF16)
        stage[e, half, :, D_MODEL:] = lax.dot_general(
            p, gm, (((1,), (1,)), ((), ())), preferred_element_type=F32).astype(BF16)
        copy(e, half).start()

    for e in range(N_EXPERTS):
        fill(e, 0)
        pl.when(n_ref[i * N_EXPERTS + e] > MOE_HALF)(functools.partial(fill, e, 1))
    for e in range(N_EXPERTS):
        copy(e, 0).wait()
        pl.when(n_ref[i * N_EXPERTS + e] > MOE_HALF)(lambda e=e: copy(e, 1).wait())


def _expert_kernel(be_ref, nu_ref, x_ref, wg_ref, wu_ref, wd_ref, o_ref):
    b = pl.program_id(0)

    @pl.when(b < nu_ref[0])
    def _():
        e = be_ref[b]
        xe = x_ref[...]
        h = xe[:, :D_MODEL]
        ext = xe[:, D_MODEL:].astype(F32)
        lane = lax.broadcasted_iota(jnp.int32, ext.shape, 1)
        g = jnp.sum(jnp.where((lane == e) | (lane == e + N_EXPERTS), ext, 0.0), axis=1, keepdims=True)
        gate = jnp.dot(h, wg_ref[0], preferred_element_type=F32)
        up = jnp.dot(h, wu_ref[0], preferred_element_type=F32)
        act = gate * jax.nn.sigmoid(gate) * up * g
        o_ref[...] = jnp.dot(act.astype(BF16), wd_ref[0], preferred_element_type=F32).astype(o_ref.dtype)

    @pl.when(b >= nu_ref[0])
    def _():
        o_ref[...] = jnp.zeros_like(o_ref)


def _combine_kernel(base_ref, n_ref, x_ref, st_ref, ys_ref, gout_ref, o_ref, ybuf, sem, *, out_norm):
    i = pl.program_id(0)

    def copy(e, half):
        row = pl.multiple_of(base_ref[i * N_EXPERTS + e] + half * MOE_HALF, MOE_ALIGN)
        return pltpu.make_async_copy(ys_ref.at[pl.ds(row, MOE_HALF)], ybuf.at[e, half],
                                     sem.at[e, half])

    for e in range(N_EXPERTS):
        copy(e, 0).start()
        pl.when(n_ref[i * N_EXPERTS + e] > MOE_HALF)(lambda e=e: copy(e, 1).start())

    st = st_ref[...]
    rank = _tile_rank(st)
    o_ref[...] = x_ref[...]

    def gather(e, half):
        copy(e, half).wait()
        p = _one_hot_rows(rank, st, e, half)
        o_ref[...] += lax.dot_general(p, ybuf[e, half], (((0,), (0,)), ((), ())),
                                      preferred_element_type=F32)

    for e in range(N_EXPERTS):
        gather(e, 0)
        pl.when(n_ref[i * N_EXPERTS + e] > MOE_HALF)(functools.partial(gather, e, 1))
    if out_norm:
        o_ref[...] = _rms(o_ref[...], gout_ref[...])


def _moe_layer(x, g_norm, w_router, w_gu, w_down, g_out=None):
    m, d = x.shape
    ff = w_down.shape[1]
    nt = m // MOE_TM
    i32 = jnp.int32
    hn, ct, st, cnt = pl.pallas_call(
        _route_kernel,
        grid=(nt,),
        in_specs=[
            pl.BlockSpec((MOE_TM, d), lambda i: (i, 0)),
            pl.BlockSpec((1, d), lambda i: (0, 0)),
            pl.BlockSpec((N_EXPERTS, d), lambda i: (0, 0)),
        ],
        out_specs=[
            pl.BlockSpec((MOE_TM, d), lambda i: (i, 0)),
            pl.BlockSpec((N_EXPERTS, MOE_TM), lambda i: (0, i)),
            pl.BlockSpec((N_EXPERTS, MOE_TM), lambda i: (0, i)),
            pl.BlockSpec((1, N_EXPERTS, LANES), lambda i: (i, 0, 0)),
        ],
        out_shape=[
            jax.ShapeDtypeStruct((m, d), BF16),
            jax.ShapeDtypeStruct((N_EXPERTS, m), F32),
            jax.ShapeDtypeStruct((N_EXPERTS, m), F32),
            jax.ShapeDtypeStruct((nt, N_EXPERTS, LANES), F32),
        ],
        compiler_params=_params("parallel"),
        name="moe_route",
    )(x, g_norm.reshape(1, d), w_router.T.astype(F32))

    n = cnt[:, :, 0].astype(i32)
    chunk = (n + MOE_ALIGN - 1) // MOE_ALIGN * MOE_ALIGN
    seg = (jnp.sum(chunk, axis=0) + MOE_HALF + MOE_TR - 1) // MOE_TR * MOE_TR
    seg_off = jnp.cumsum(seg) - seg
    base = (seg_off[None, :] + jnp.cumsum(chunk, axis=0) - chunk).reshape(-1).astype(i32)
    n_flat = n.reshape(-1)
    bound = 2 * m + nt * N_EXPERTS * (MOE_ALIGN - 1) + N_EXPERTS * (MOE_HALF + MOE_TR - 1)
    nb = -(-bound // MOE_TR)
    blk_end = jnp.cumsum(seg // MOE_TR)
    blk_expert = jnp.minimum(jnp.searchsorted(blk_end, jnp.arange(nb, dtype=i32), side='right'),
                             N_EXPERTS - 1).astype(i32)
    n_used = blk_end[-1:].astype(i32)
    rows = nb * MOE_TR

    tile_spec = lambda shape, imap: pl.BlockSpec(shape, imap)
    xs = pl.pallas_call(
        _dispatch_kernel,
        grid_spec=pltpu.PrefetchScalarGridSpec(
            num_scalar_prefetch=2,
            grid=(nt,),
            in_specs=[
                tile_spec((MOE_TM, d), lambda i, b, c: (i, 0)),
                tile_spec((N_EXPERTS, MOE_TM), lambda i, b, c: (0, i)),
                tile_spec((N_EXPERTS, MOE_TM), lambda i, b, c: (0, i)),
                pl.BlockSpec(memory_space=pl.ANY),
            ],
            out_specs=pl.BlockSpec(memory_space=pl.ANY),
            scratch_shapes=[
                pltpu.VMEM((N_EXPERTS, 2, MOE_HALF, XS_W), BF16),
                pltpu.SemaphoreType.DMA((N_EXPERTS, 2)),
            ],
        ),
        out_shape=jax.ShapeDtypeStruct((rows, XS_W), BF16),
        input_output_aliases={5: 0},
        compiler_params=_params("arbitrary"),
        name="moe_dispatch",
    )(base, n_flat, hn, ct, st, jnp.zeros((rows, XS_W), BF16))

    ys = pl.pallas_call(
        _expert_kernel,
        grid_spec=pltpu.PrefetchScalarGridSpec(
            num_scalar_prefetch=2,
            grid=(nb,),
            in_specs=[
                pl.BlockSpec((MOE_TR, XS_W), lambda b, be, nu: (b, 0)),
                pl.BlockSpec((1, d, ff), lambda b, be, nu: (be[b], 0, 0)),
                pl.BlockSpec((1, d, ff), lambda b, be, nu: (be[b], 0, 1)),
                pl.BlockSpec((1, ff, d), lambda b, be, nu: (be[b], 0, 0)),
            ],
            out_specs=pl.BlockSpec((MOE_TR, d), lambda b, be, nu: (b, 0)),
        ),
        out_shape=jax.ShapeDtypeStruct((rows, d), BF16),
        compiler_params=_params("arbitrary"),
        name="moe_experts",
    )(blk_expert, n_used, xs, w_gu, w_gu, w_down)

    out_norm = g_out is not None
    g_out = jnp.ones((d,), F32) if g_out is None else g_out
    return pl.pallas_call(
        functools.partial(_combine_kernel, out_norm=out_norm),
        grid_spec=pltpu.PrefetchScalarGridSpec(
            num_scalar_prefetch=2,
            grid=(nt,),
            in_specs=[
                tile_spec((MOE_TM, d), lambda i, b, c: (i, 0)),
                tile_spec((N_EXPERTS, MOE_TM), lambda i, b, c: (0, i)),
                pl.BlockSpec(memory_space=pl.ANY),
                tile_spec((1, d), lambda i, b, c: (0, 0)),
            ],
            out_specs=tile_spec((MOE_TM, d), lambda i, b, c: (i, 0)),
            scratch_shapes=[
                pltpu.VMEM((N_EXPERTS, 2, MOE_HALF, d), BF16),
                pltpu.SemaphoreType.DMA((N_EXPERTS, 2)),
            ],
        ),
        out_shape=jax.ShapeDtypeStruct((m, d), F32),
        compiler_params=_params("arbitrary"),
        name="moe_combine",
    )(base, n_flat, x, st, ys, g_out.reshape(1, d).astype(F32))


def _to_frame_major(x, bsz, seq):
    d = x.shape[-1]
    return x.reshape(bsz * (seq // CHUNK), CHUNK, d).transpose(1, 0, 2).reshape(bsz * seq, d)


def _from_frame_major(x, bsz, seq):
    d = x.shape[-1]
    return x.reshape(CHUNK, bsz * (seq // CHUNK), d).transpose(1, 0, 2).reshape(bsz * seq, d)


def kernel(x, mem, norm_mix, norm_mem, norm_ffn, mem_norm, final_norm, s5_w_in, s5_lam_re, s5_lam_im, s5_log_dt, s5_b_re, s5_b_im, s5_c_re, s5_c_im, s5_d, s5_w_glu, att_w_qkv, att_w_o, att_rel_bias, lru_w_in, lru_conv_w, lru_conv_b, lru_w_a, lru_b_a, lru_w_x, lru_b_x, lru_lam, lru_w_out, mem_w_q, mem_w_kv, mem_w_o, ffn_w_gu, ffn_w_down, moe_w_router, moe_w_gu, moe_w_down):
    bsz, seq, d = x.shape
    x = x.reshape(bsz * seq, d).astype(F32)

    w_kv_all = jnp.concatenate([mem_w_kv[i] for i in range(DEPTH)], axis=1).astype(BF16)
    kv_all = _norm_linear(mem.reshape(bsz * N_MEM, d).astype(F32), mem_norm, w_kv_all, BF16,
                          tm=bsz * N_MEM, tn=2 * d)

    for i in range(DEPTH):
        kind, j = i % 3, i // 3
        if kind == 0:
            xp = _to_frame_major(x, bsz, seq)
            xp = _s5_layer(xp, norm_mix[i], s5_w_in[j], s5_lam_re[j], s5_lam_im[j], s5_log_dt[j],
                           s5_b_re[j], s5_b_im[j], s5_c_re[j], s5_c_im[j], s5_d[j], s5_w_glu[j],
                           bsz, seq)
            x = _from_frame_major(xp, bsz, seq)
        elif kind == 1:
            x = _attn_layer(x, norm_mix[i], att_w_qkv[j], att_w_o[j], att_rel_bias[j], bsz, seq)
        else:
            x = _lru_layer(x, norm_mix[i], lru_w_in[j], lru_conv_w[j], lru_conv_b[j], lru_w_a[j],
                           lru_b_a[j], lru_w_x[j], lru_b_x[j], lru_lam[j], lru_w_out[j], bsz, seq)
        x = _memattn_layer(x, norm_mem[i], mem_w_q[i], kv_all, i, mem_w_o[i], bsz, seq)
        if i % 2 == 0:
            x = _ffn_layer(x, norm_ffn[i], ffn_w_gu[i // 2].astype(BF16),
                           ffn_w_down[i // 2].astype(BF16), tf=1408)
        else:
            x = _moe_layer(x, norm_ffn[i], moe_w_router[i // 2], moe_w_gu[i // 2].astype(BF16),
                           moe_w_down[i // 2].astype(BF16),
                           g_out=final_norm if i == DEPTH - 1 else None)
    return x.reshape(bsz, seq, d)
```

```python
import functools
import math

import jax
import jax.numpy as jnp
from jax import lax
from jax.experimental import pallas as pl
from jax.experimental.pallas import tpu as pltpu

F32 = jnp.float32
BF16 = jnp.bfloat16
HIGHEST = lax.Precision.HIGHEST

D_MODEL = 1024
DEPTH = 4
CHUNK = 64
N_MEM = 256
EPS = 1e-6

S5_GROUP = 16
S5_GROUPS = D_MODEL // S5_GROUP
S5_STATE = 64

ATT_HEADS = 16
ATT_HEAD_DIM = D_MODEL // ATT_HEADS
LEFT_CHUNKS = 8
MAX_REL = 128
ATT_TQ = 4 * CHUNK
ATT_LEFT = LEFT_CHUNKS * CHUNK
ATT_TK = ATT_LEFT + ATT_TQ
NEG_INF = -1e30
LOG2E = math.log2(math.e)

LRU_BLOCKS = 8
LRU_BLOCK_W = D_MODEL // LRU_BLOCKS
CONV_W = 4
LRU_C = 8.0
LRU_T = 512
SUBLANES = 8
LANES = 128

MEM_HEADS = 4
MEM_HEAD_DIM = D_MODEL // MEM_HEADS

N_EXPERTS = 8
FFN_TF = 256

VMEM_LIMIT = 56 * 1024 * 1024


def _params(*sem):
    return pltpu.CompilerParams(dimension_semantics=sem, vmem_limit_bytes=VMEM_LIMIT)


def _rms(x, g):
    ms = jnp.mean(x * x, axis=-1, keepdims=True)
    return x * lax.rsqrt(ms + EPS) * g


def _norm_linear_kernel(x_ref, g_ref, w_ref, o_ref, hn_ref):
    @pl.when(pl.program_id(1) == 0)
    def _():
        hn_ref[...] = _rms(x_ref[...], g_ref[...]).astype(BF16)

    o_ref[...] = jnp.dot(hn_ref[...], w_ref[...], preferred_element_type=F32).astype(o_ref.dtype)


def _norm_linear(x, g, w, out_dtype, tm=512, tn=None):
    m, d = x.shape
    n = w.shape[1]
    tn = n if tn is None else tn
    return pl.pallas_call(
        _norm_linear_kernel,
        grid=(m // tm, n // tn),
        in_specs=[
            pl.BlockSpec((tm, d), lambda i, j: (i, 0)),
            pl.BlockSpec((1, d), lambda i, j: (0, 0)),
            pl.BlockSpec((d, tn), lambda i, j: (0, j)),
        ],
        out_specs=pl.BlockSpec((tm, tn), lambda i, j: (i, j)),
        out_shape=jax.ShapeDtypeStruct((m, n), out_dtype),
        scratch_shapes=[pltpu.VMEM((tm, d), BF16)],
        compiler_params=_params("parallel", "arbitrary"),
        name="norm_linear",
    )(x, g.reshape(1, d), w)


def _norm_linear_t_kernel(x_ref, g_ref, wt_ref, o_ref):
    hn = _rms(x_ref[...], g_ref[...]).astype(BF16)
    o_ref[...] = lax.dot_general(
        wt_ref[...], hn, (((1,), (1,)), ((), ())), preferred_element_type=F32
    ).astype(o_ref.dtype)


def _norm_linear_t(x, g, wt, out_dtype, tm=512):
    m, d = x.shape
    n = wt.shape[0]
    return pl.pallas_call(
        _norm_linear_t_kernel,
        grid=(m // tm,),
        in_specs=[
            pl.BlockSpec((tm, d), lambda i: (i, 0)),
            pl.BlockSpec((1, d), lambda i: (0, 0)),
            pl.BlockSpec((n, d), lambda i: (0, 0)),
        ],
        out_specs=pl.BlockSpec((n, tm), lambda i: (0, i)),
        out_shape=jax.ShapeDtypeStruct((n, m), out_dtype),
        compiler_params=_params("parallel"),
        name="norm_linear_t",
    )(x, g.reshape(1, d), wt)


def _s5_operators(lam_re, lam_im, log_dt, b_re, b_im, c_re, c_im, d_skip, n_chunks):
    g, n, p = S5_GROUPS, S5_STATE, S5_GROUP
    lr = lam_re.astype(F32)
    li = lam_im.astype(F32)
    dt = jnp.exp(log_dt.astype(F32))[:, None]
    mag = jnp.exp(lr * dt)
    ar = mag * jnp.cos(li * dt)
    ai = mag * jnp.sin(li * dt)
    den = lr * lr + li * li
    fr = ((ar - 1.0) * lr + ai * li) / den
    fi = (ai * lr - (ar - 1.0) * li) / den
    bbr = fr[..., None] * b_re - fi[..., None] * b_im
    bbi = fr[..., None] * b_im + fi[..., None] * b_re

    kk = jnp.arange(CHUNK + 1, dtype=F32)
    mag_k = jnp.exp((lr * dt)[..., None] * kk)
    pr = mag_k * jnp.cos((li * dt)[..., None] * kk)
    pi = mag_k * jnp.sin((li * dt)[..., None] * kk)
    pr_kn, pi_kn = pr.transpose(0, 2, 1), pi.transpose(0, 2, 1)
    cr, ci = c_re[:, :, None, :], c_im[:, :, None, :]
    lcr = cr * pr_kn[:, None] - ci * pi_kn[:, None]
    lci = cr * pi_kn[:, None] + ci * pr_kn[:, None]
    kern = (jnp.einsum('gqtn,gnp->gqtp', lcr[:, :, :CHUNK], bbr, precision=HIGHEST)
            - jnp.einsum('gqtn,gnp->gqtp', lci[:, :, :CHUNK], bbi, precision=HIGHEST))
    kern = kern.at[:, :, 0].add(d_skip.reshape(g, p)[:, :, None] * jnp.eye(p, dtype=F32))
    rev = kern.transpose(0, 1, 3, 2)[..., ::-1]
    odd = rev[:, :, 1::2]
    even = jnp.roll(rev[:, :, 0::2], -1, axis=2)
    toe = jnp.concatenate([odd, even], axis=-1).reshape(g, p, p * CHUNK)
    rev_r = pr[..., CHUNK - 1::-1][:, :, None, :]
    rev_i = pi[..., CHUNK - 1::-1][:, :, None, :]
    bm_r = rev_r * bbr[..., None] - rev_i * bbi[..., None]
    bm_i = rev_r * bbi[..., None] + rev_i * bbr[..., None]
    bm = jnp.concatenate([bm_r, bm_i], axis=1).reshape(g, 2 * n, p * CHUNK).astype(BF16)
    cinj = jnp.concatenate([lcr[:, :, 1:], -lci[:, :, 1:]], axis=-1)
    cinj = cinj.reshape(g, p * CHUNK, 2 * n).astype(BF16)
    qr, qi = [pr[..., CHUNK]], [pi[..., CHUNK]]
    for _ in range(int(math.log2(n_chunks)) - 1):
        qr, qi = qr + [qr[-1] * qr[-1] - qi[-1] * qi[-1]], qi + [2.0 * qr[-1] * qi[-1]]
    levels = len(qr)
    qr, qi = jnp.stack(qr), jnp.stack(qi)
    m1 = jnp.concatenate([qr, qr], axis=-1)
    m2 = jnp.concatenate([-qi, qi], axis=-1)
    ap = jnp.stack([m1, m2], axis=-1).transpose(1, 2, 0, 3).reshape(g, 2 * n, 2 * levels)
    return toe, bm, cinj, ap


def _ssm_kernel(u_ref, seed_ref, bm_ref, cinj_ref, ap_ref, o_ref, toe_ref, *, n_chunks):
    n = S5_STATE
    u = u_ref[0]
    rows = S5_GROUP * CHUNK
    causal = (lax.broadcasted_iota(jnp.int32, (CHUNK, rows), 1) % CHUNK
              <= lax.broadcasted_iota(jnp.int32, (CHUNK, rows), 0))
    for q in range(S5_GROUP):
        seed = jnp.broadcast_to(seed_ref[0, q:q + 1, :], (CHUNK, rows))
        rolled = pltpu.roll(seed, 1, 1, stride=1, stride_axis=0)
        toe_ref[q * CHUNK:(q + 1) * CHUNK, :] = jnp.where(causal, rolled, 0.0).astype(BF16)
    y = jnp.dot(toe_ref[...], u, preferred_element_type=F32)
    x = jnp.dot(bm_ref[0], u, preferred_element_type=F32)
    ap = ap_ref[0]
    col = lax.broadcasted_iota(jnp.int32, x.shape, 1) % n_chunks
    for k in range(int(math.log2(n_chunks))):
        sh = 1 << k
        xs = jnp.where(col >= sh, pltpu.roll(x, sh, 1), 0.0)
        xsw = jnp.concatenate([xs[n:], xs[:n]], axis=0)
        x = x + ap[:, 2 * k:2 * k + 1] * xs + ap[:, 2 * k + 1:2 * k + 2] * xsw
    h0 = jnp.where(col >= 1, pltpu.roll(x, 1, 1), 0.0)
    h0_hi = h0.astype(BF16)
    h0_lo = (h0 - h0_hi.astype(F32)).astype(BF16)
    cinj = cinj_ref[0]
    y = y + jnp.dot(cinj, h0_hi, preferred_element_type=F32)
    y = y + jnp.dot(cinj, h0_lo, preferred_element_type=F32)
    o_ref[0] = jax.nn.gelu(y).astype(o_ref.dtype)


def _glu_kernel(vt_ref, w_ref, x_ref, o_ref):
    z = lax.dot_general(vt_ref[...], w_ref[...], (((0,), (0,)), ((), ())),
                        preferred_element_type=F32)
    d = o_ref.shape[1]
    o_ref[...] = x_ref[...] + z[:, :d] * jax.nn.sigmoid(z[:, d:])


def _s5_layer(x, g_norm, w_in, lam_re, lam_im, log_dt, b_re, b_im, c_re, c_im, d_skip, w_glu,
              bsz, seq):
    m, d = x.shape
    n_chunks = seq // CHUNK
    ncol = bsz * n_chunks
    toe, bm, cinj, ap = _s5_operators(lam_re, lam_im, log_dt, b_re, b_im, c_re, c_im, d_skip,
                                      n_chunks)
    ut = _norm_linear_t(x, g_norm, w_in.T.astype(BF16), BF16)
    rows = S5_GROUP * CHUNK
    ut = ut.reshape(S5_GROUPS, rows, ncol)
    vt = pl.pallas_call(
        functools.partial(_ssm_kernel, n_chunks=n_chunks),
        grid=(S5_GROUPS,),
        in_specs=[
            pl.BlockSpec((1, rows, ncol), lambda g: (g, 0, 0)),
            pl.BlockSpec((1, S5_GROUP, rows), lambda g: (g, 0, 0)),
            pl.BlockSpec((1, 2 * S5_STATE, rows), lambda g: (g, 0, 0)),
            pl.BlockSpec((1, rows, 2 * S5_STATE), lambda g: (g, 0, 0)),
            pl.BlockSpec((1,) + ap.shape[1:], lambda g: (g, 0, 0)),
        ],
        out_specs=pl.BlockSpec((1, rows, ncol), lambda g: (g, 0, 0)),
        out_shape=jax.ShapeDtypeStruct((S5_GROUPS, rows, ncol), BF16),
        scratch_shapes=[pltpu.VMEM((rows, rows), BF16)],
        compiler_params=_params("parallel"),
        name="s5_ssm",
    )(ut, toe, bm, cinj, ap)
    vt = vt.reshape(d, m)
    tm = 512
    return pl.pallas_call(
        _glu_kernel,
        grid=(m // tm,),
        in_specs=[
            pl.BlockSpec((d, tm), lambda i: (0, i)),
            pl.BlockSpec((d, 2 * d), lambda i: (0, 0)),
            pl.BlockSpec((tm, d), lambda i: (i, 0)),
        ],
        out_specs=pl.BlockSpec((tm, d), lambda i: (i, 0)),
        out_shape=jax.ShapeDtypeStruct((m, d), F32),
        compiler_params=_params("parallel"),
        name="s5_glu",
    )(vt, w_glu.astype(BF16), x)


ATT_PERIOD = ATT_TQ + ATT_TK


def _attn_bias_seed(rel_bias):
    k = jnp.arange(ATT_PERIOD)
    rel = jnp.where(k < ATT_TK, ATT_LEFT - k, ATT_LEFT + ATT_PERIOD - k)
    return jnp.take(rel_bias.astype(F32), jnp.clip(rel, -MAX_REL, MAX_REL) + MAX_REL, axis=1)


def _attn_kernel(x_ref, q_ref, k0_ref, k1_ref, k2_ref, v0_ref, v1_ref, v2_ref, seed_ref, wo_ref,
                 o_ref, bias_ref, ocat_ref):
    i = pl.program_id(1)

    @pl.when((pl.program_id(0) == 0) & (i == 0))
    def _():
        r = lax.broadcasted_iota(jnp.int32, (ATT_TQ, ATT_TK), 0)
        j = lax.broadcasted_iota(jnp.int32, (ATT_TQ, ATT_TK), 1)
        dchunk = (r + ATT_LEFT) // CHUNK - j // CHUNK
        valid = (dchunk >= 0) & (dchunk <= LEFT_CHUNKS)
        for h in range(ATT_HEADS):
            seed = jnp.broadcast_to(seed_ref[h:h + 1, :], (ATT_TQ, ATT_PERIOD))
            rolled = pltpu.roll(seed, 0, 1, stride=1, stride_axis=0)
            bias_ref[h] = jnp.where(valid, rolled[:, :ATT_TK] * LOG2E, NEG_INF)

    lane = lax.broadcasted_iota(jnp.int32, (1, LANES), 1)
    col = lax.broadcasted_iota(jnp.int32, (ATT_TQ, ATT_TK), 1)
    pad = col < ATT_LEFT - i * ATT_TQ
    scale = ATT_HEAD_DIM ** -0.5
    heads_per = LANES // ATT_HEAD_DIM
    for hp in range(D_MODEL // LANES):
        sl = slice(hp * LANES, (hp + 1) * LANES)
        q = q_ref[:, sl] * scale
        k = jnp.concatenate([k0_ref[:, sl], k1_ref[:, sl], k2_ref[:, sl]], axis=0)
        v = jnp.concatenate([v0_ref[:, sl], v1_ref[:, sl], v2_ref[:, sl]], axis=0)
        acc = jnp.zeros((ATT_TQ, LANES), F32)
        for h in range(heads_per):
            in_head = (lane >= h * ATT_HEAD_DIM) & (lane < (h + 1) * ATT_HEAD_DIM)
            qh = jnp.where(in_head, q, jnp.zeros_like(q))
            s = lax.dot_general(qh, k, (((1,), (1,)), ((), ())), preferred_element_type=F32)
            s = jnp.where(pad, NEG_INF, s + bias_ref[hp * heads_per + h])
            p = jnp.exp2(s - jnp.max(s, axis=-1, keepdims=True))
            vh = jnp.where(in_head, v, jnp.ones_like(v))
            pv = jnp.dot(p.astype(BF16), vh, preferred_element_type=F32)
            acc = jnp.where(in_head, pv / pltpu.roll(pv, ATT_HEAD_DIM, 1), acc)
        ocat_ref[:, sl] = acc.astype(BF16)
    o_ref[...] = x_ref[...] + jnp.dot(ocat_ref[...], wo_ref[...], preferred_element_type=F32)


def _attn_layer(x, g_norm, w_qkv, w_o, rel_bias, bsz, seq):
    m, d = x.shape
    w_qkv = jnp.concatenate([w_qkv[:, :d] * LOG2E, w_qkv[:, d:]], axis=1)
    qkv = _norm_linear(x, g_norm, w_qkv.astype(BF16), BF16)
    nq = seq // ATT_TQ

    def rows(col, back):
        return lambda b, i: (b * nq + jnp.maximum(i - back, 0), col)

    blk = (ATT_TQ, d)
    nkb = ATT_TK // ATT_TQ
    return pl.pallas_call(
        _attn_kernel,
        grid=(bsz, nq),
        in_specs=(
            [pl.BlockSpec(blk, rows(0, 0)), pl.BlockSpec(blk, rows(0, 0))]
            + [pl.BlockSpec(blk, rows(1, nkb - 1 - kb)) for kb in range(nkb)]
            + [pl.BlockSpec(blk, rows(2, nkb - 1 - kb)) for kb in range(nkb)]
            + [pl.BlockSpec((ATT_HEADS, ATT_PERIOD), lambda b, i: (0, 0)),
               pl.BlockSpec((d, d), lambda b, i: (0, 0))]
        ),
        out_specs=pl.BlockSpec(blk, rows(0, 0)),
        out_shape=jax.ShapeDtypeStruct((m, d), F32),
        scratch_shapes=[pltpu.VMEM((ATT_HEADS, ATT_TQ, ATT_TK), F32), pltpu.VMEM((ATT_TQ, d), BF16)],
        compiler_params=_params("arbitrary", "arbitrary"),
        name="chunk_attn",
    )(x, *([qkv] * (1 + 2 * nkb)), _attn_bias_seed(rel_bias), w_o.astype(BF16))


def _lru_kernel(x_ref, g_ref, win_ref, cw_ref, cb_ref, wax_ref, ba_ref, bx_ref, lam_ref, wout_ref,
                o_ref, prev_ref, carry_ref, a_s, b_s):
    t, w = x_ref.shape

    @pl.when(pl.program_id(1) == 0)
    def _():
        prev_ref[...] = jnp.zeros_like(prev_ref)
        carry_ref[...] = jnp.zeros_like(carry_ref)

    x = x_ref[...]
    z = jnp.dot(_rms(x, g_ref[...]).astype(BF16), win_ref[...], preferred_element_type=F32)
    gate = z[:, :w]
    xr = z[:, w:]
    xext = jnp.concatenate([prev_ref[...], xr], axis=0)
    cw = cw_ref[...]
    xc = cb_ref[...] + cw[0:1] * xext[SUBLANES - 3:SUBLANES - 3 + t]
    for k in range(1, CONV_W):
        xc = xc + cw[k:k + 1] * xext[SUBLANES - 3 + k:SUBLANES - 3 + k + t]
    prev_ref[...] = xr[t - SUBLANES:]

    xcb = xc.astype(BF16)
    pre_a, pre_x = [], []
    for blk in range(LRU_BLOCKS):
        pre = jnp.dot(xcb[:, blk * LRU_BLOCK_W:(blk + 1) * LRU_BLOCK_W], wax_ref[blk],
                      preferred_element_type=F32)
        pre_a.append(pre[:, :LRU_BLOCK_W])
        pre_x.append(pre[:, LRU_BLOCK_W:])
    r = jax.nn.sigmoid(jnp.concatenate(pre_a, axis=1) + ba_ref[...])
    ig = jax.nn.sigmoid(jnp.concatenate(pre_x, axis=1) + bx_ref[...])
    z = -lam_ref[...]
    softplus = jnp.maximum(z, 0.0) + jnp.log1p(jnp.exp(-jnp.abs(z)))
    log_a = -LRU_C * r * softplus
    a = jnp.exp(log_a)
    b = jnp.sqrt(1.0 - a * a) * (ig * xc)

    a3 = a.reshape(t // SUBLANES, SUBLANES, w)
    b3 = b.reshape(t // SUBLANES, SUBLANES, w)
    sub = lax.broadcasted_iota(jnp.int32, a3.shape, 1)
    for sh in (1, 2, 4):
        a_sh = jnp.where(sub >= sh, pltpu.roll(a3, sh, 1), 1.0)
        b_sh = jnp.where(sub >= sh, pltpu.roll(b3, sh, 1), 0.0)
        b3 = a3 * b_sh + b3
        a3 = a3 * a_sh
    a_s[...] = a3
    b_s[...] = b3

    def body(j, carry):
        hb = b_s[j] + a_s[j] * carry
        b_s[j] = hb
        return hb[SUBLANES - 1:SUBLANES, :]

    carry = lax.fori_loop(0, t // SUBLANES, body, carry_ref[0:1, :])
    carry_ref[0:1, :] = carry
    h = b_s[...].reshape(t, w)
    y = (jax.nn.gelu(gate) * h).astype(BF16)
    o_ref[...] = x + jnp.dot(y, wout_ref[...], preferred_element_type=F32)


def _lru_layer(x, g_norm, w_in, conv_w, conv_b, w_a, b_a, w_x, b_x, lam, w_out, bsz, seq):
    m, d = x.shape
    wax = jnp.concatenate([w_a, w_x], axis=-1).astype(BF16)
    nt = seq // LRU_T
    row = lambda v: v.reshape(1, d).astype(F32)
    vec = pl.BlockSpec((1, d), lambda b, i: (0, 0))
    return pl.pallas_call(
        _lru_kernel,
        grid=(bsz, nt),
        in_specs=[
            pl.BlockSpec((LRU_T, d), lambda b, i: (b * nt + i, 0)),
            vec,
            pl.BlockSpec((d, 2 * d), lambda b, i: (0, 0)),
            pl.BlockSpec((CONV_W, d), lambda b, i: (0, 0)),
            vec,
            pl.BlockSpec(wax.shape, lambda b, i: (0, 0, 0)),
            vec, vec, vec,
            pl.BlockSpec((d, d), lambda b, i: (0, 0)),
        ],
        out_specs=pl.BlockSpec((LRU_T, d), lambda b, i: (b * nt + i, 0)),
        out_shape=jax.ShapeDtypeStruct((m, d), F32),
        scratch_shapes=[
            pltpu.VMEM((SUBLANES, d), F32),
            pltpu.VMEM((SUBLANES, d), F32),
            pltpu.VMEM((LRU_T // SUBLANES, SUBLANES, d), F32),
            pltpu.VMEM((LRU_T // SUBLANES, SUBLANES, d), F32),
        ],
        compiler_params=_params("parallel", "arbitrary"),
        name="rglru",
    )(x, row(g_norm), w_in.astype(BF16), conv_w.astype(F32), row(conv_b), wax, row(b_a), row(b_x),
      row(lam), w_out.astype(BF16))


def _memattn_kernel(x_ref, g_ref, wq_ref, k_ref, v_ref, wo_ref, o_ref, ocat_ref):
    x = x_ref[...]
    hn = _rms(x, g_ref[...]).astype(BF16)
    q = jnp.dot(hn, wq_ref[...], preferred_element_type=F32) * (MEM_HEAD_DIM ** -0.5)
    q = q.astype(BF16)
    for h in range(MEM_HEADS):
        sl = slice(h * MEM_HEAD_DIM, (h + 1) * MEM_HEAD_DIM)
        s = lax.dot_general(q[:, sl], k_ref[:, sl], (((1,), (1,)), ((), ())),
                            preferred_element_type=F32)
        mx = jnp.max(s, axis=-1, keepdims=True)
        p = jnp.exp(s - mx)
        l = jnp.sum(p, axis=-1, keepdims=True)
        o = jnp.dot(p.astype(BF16), v_ref[:, sl], preferred_element_type=F32) / l
        ocat_ref[:, sl] = o.astype(BF16)
    o_ref[...] = x + jnp.dot(ocat_ref[...], wo_ref[...], preferred_element_type=F32)


def _memattn_layer(x, g_norm, w_q, kv_all, layer, w_o, bsz, seq, tm=512):
    m, d = x.shape
    nt = seq // tm
    wspec = pl.BlockSpec((d, d), lambda b, i: (0, 0))
    return pl.pallas_call(
        _memattn_kernel,
        grid=(bsz, nt),
        in_specs=[
            pl.BlockSpec((tm, d), lambda b, i: (b * nt + i, 0)),
            pl.BlockSpec((1, d), lambda b, i: (0, 0)),
            wspec,
            pl.BlockSpec((N_MEM, d), lambda b, i: (b, 2 * layer)),
            pl.BlockSpec((N_MEM, d), lambda b, i: (b, 2 * layer + 1)),
            wspec,
        ],
        out_specs=pl.BlockSpec((tm, d), lambda b, i: (b * nt + i, 0)),
        out_shape=jax.ShapeDtypeStruct((m, d), F32),
        scratch_shapes=[pltpu.VMEM((tm, d), BF16)],
        compiler_params=_params("parallel", "parallel"),
        name="mem_attn",
    )(x, g_norm.reshape(1, d), w_q.astype(BF16), kv_all, kv_all, w_o.astype(BF16))


def _ffn_kernel(x_ref, g_ref, wgu_ref, wd_ref, o_ref):
    ff = wd_ref.shape[0]
    x = x_ref[...]
    hn = _rms(x, g_ref[...]).astype(BF16)
    o_ref[...] = x
    for f in range(ff // FFN_TF):
        lo = f * FFN_TF
        gate = jnp.dot(hn, wgu_ref[:, lo:lo + FFN_TF], preferred_element_type=F32)
        up = jnp.dot(hn, wgu_ref[:, ff + lo:ff + lo + FFN_TF], preferred_element_type=F32)
        act = gate * jax.nn.sigmoid(gate) * up
        o_ref[...] += jnp.dot(act.astype(BF16), wd_ref[lo:lo + FFN_TF, :], preferred_element_type=F32)


def _ffn_layer(x, g_norm, w_gu, w_down, tm=1024):
    m, d = x.shape
    ff = w_down.shape[0]
    resident = lambda shape: pl.BlockSpec(shape, lambda i: (0, 0), pipeline_mode=pl.Buffered(1))
    return pl.pallas_call(
        _ffn_kernel,
        grid=(m // tm,),
        in_specs=[
            pl.BlockSpec((tm, d), lambda i: (i, 0)),
            pl.BlockSpec((1, d), lambda i: (0, 0)),
            resident((d, 2 * ff)),
            resident((ff, d)),
        ],
        out_specs=pl.BlockSpec((tm, d), lambda i: (i, 0)),
        out_shape=jax.ShapeDtypeStruct((m, d), F32),
        compiler_params=_params("parallel"),
        name="ffn",
    )(x, g_norm.reshape(1, d), w_gu, w_down)


MOE_TM = 512
MOE_TR = 512
MOE_HALF = MOE_TM // 2
MOE_ALIGN = 16
XS_W = D_MODEL + LANES


def _route_kernel(x_ref, g_ref, wrt_ref, hn_ref, ct_ref, st_ref, cnt_ref):
    hn = _rms(x_ref[...], g_ref[...])
    hn_ref[...] = hn.astype(BF16)
    lg = lax.dot_general(wrt_ref[...], hn, (((1,), (1,)), ((), ())),
                         preferred_element_type=F32, precision=HIGHEST)
    idx = lax.broadcasted_iota(jnp.int32, lg.shape, 0)
    m1 = jnp.max(lg, axis=0, keepdims=True)
    i1 = jnp.min(jnp.where(lg == m1, idx, N_EXPERTS), axis=0, keepdims=True)
    oh1 = idx == i1
    lg2 = jnp.where(oh1, -jnp.inf, lg)
    m2 = jnp.max(lg2, axis=0, keepdims=True)
    i2 = jnp.min(jnp.where(lg2 == m2, idx, N_EXPERTS), axis=0, keepdims=True)
    oh2 = idx == i2
    e2 = jnp.exp(m2 - m1)
    g1 = 1.0 / (1.0 + e2)
    g2 = e2 / (1.0 + e2)
    ct_ref[...] = jnp.where(oh1, g1, 0.0) + jnp.where(oh2, g2, 0.0)
    sel = (oh1 | oh2).astype(F32)
    st_ref[...] = sel
    cnt_ref[0] = jnp.broadcast_to(jnp.sum(sel, axis=1, keepdims=True), (N_EXPERTS, LANES))


def _tile_rank(st):
    tm = st.shape[1]
    before = (lax.broadcasted_iota(jnp.int32, (tm, tm), 0)
              < lax.broadcasted_iota(jnp.int32, (tm, tm), 1))
    return jnp.dot(st.astype(BF16), before.astype(BF16), preferred_element_type=F32)


def _one_hot_rows(rank, st, e, half):
    tm = rank.shape[1]
    rows = lax.broadcasted_iota(jnp.int32, (MOE_HALF, tm), 0).astype(F32) + float(half * MOE_HALF)
    hit = (rank[e:e + 1] == rows) & (st[e:e + 1] > 0.0)
    return jnp.where(hit, 1.0, 0.0).astype(BF16)


def _one_hot_all(rank, st):
    return jnp.concatenate([_one_hot_rows(rank, st, e, 0) for e in range(N_EXPERTS)], axis=0)


def _dispatch_kernel(base_ref, n_ref, hn_ref, ct_ref, st_ref, xs_init_ref, xs_ref, stage, stage2,
                     sem, sem2):
    del xs_init_ref
    i = pl.program_id(0)
    hn = hn_ref[...]
    ct = ct_ref[...]
    st = st_ref[...]
    rank = _tile_rank(st)
    g_hi = ct.astype(BF16)
    g_lo = (ct - g_hi.astype(F32)).astype(BF16)
    gm = jnp.concatenate(
        [g_hi, g_lo, jnp.zeros((LANES - 2 * N_EXPERTS, ct.shape[1]), BF16)], axis=0)
    nt_dims = (((1,), (1,)), ((), ()))

    def copy(e):
        row = pl.multiple_of(base_ref[i * N_EXPERTS + e], MOE_ALIGN)
        return pltpu.make_async_copy(stage.at[pl.ds(e * MOE_HALF, MOE_HALF)],
                                     xs_ref.at[pl.ds(row, MOE_HALF)], sem.at[e])

    def copy2(e):
        row = pl.multiple_of(base_ref[i * N_EXPERTS + e] + MOE_HALF, MOE_ALIGN)
        return pltpu.make_async_copy(stage2.at[e], xs_ref.at[pl.ds(row, MOE_HALF)], sem2.at[e])

    p = _one_hot_all(rank, st)
    stage[:, :D_MODEL] = jnp.dot(p, hn, preferred_element_type=F32).astype(BF16)
    stage[:, D_MODEL:] = lax.dot_general(p, gm, nt_dims, preferred_element_type=F32).astype(BF16)
    for e in range(N_EXPERTS):
        copy(e).start()

    def fill2(e):
        p2 = _one_hot_rows(rank, st, e, 1)
        stage2[e, :, :D_MODEL] = jnp.dot(p2, hn, preferred_element_type=F32).astype(BF16)
        stage2[e, :, D_MODEL:] = lax.dot_general(p2, gm, nt_dims, preferred_element_type=F32).astype(BF16)
        copy2(e).start()

    for e in range(N_EXPERTS):
        pl.when(n_ref[i * N_EXPERTS + e] > MOE_HALF)(functools.partial(fill2, e))
    for e in range(N_EXPERTS):
        copy(e).wait()
        pl.when(n_ref[i * N_EXPERTS + e] > MOE_HALF)(lambda e=e: copy2(e).wait())


def _expert_kernel(be_ref, nu_ref, x_ref, wg_ref, wu_ref, wd_ref, o_ref):
    b = pl.program_id(0)

    @pl.when(b < nu_ref[0])
    def _():
        e = be_ref[b]
        xe = x_ref[...]
        h = xe[:, :D_MODEL]
        ext = xe[:, D_MODEL:].astype(F32)
        lane = lax.broadcasted_iota(jnp.int32, ext.shape, 1)
        g = jnp.sum(jnp.where((lane == e) | (lane == e + N_EXPERTS), ext, 0.0), axis=1, keepdims=True)
        gate = jnp.dot(h, wg_ref[0], preferred_element_type=F32)
        up = jnp.dot(h, wu_ref[0], preferred_element_type=F32)
        act = gate * jax.nn.sigmoid(gate) * up * g
        o_ref[...] = jnp.dot(act.astype(BF16), wd_ref[0], preferred_element_type=F32).astype(o_ref.dtype)

    @pl.when(b >= nu_ref[0])
    def _():
        o_ref[...] = jnp.zeros_like(o_ref)


def _combine_kernel(base_ref, n_ref, x_ref, st_ref, ys_ref, gout_ref, o_ref, ybuf, ybuf2, sem, sem2,
                    *, out_norm):
    i = pl.program_id(0)
    slot = i % 2

    def copy(step, e):
        row = pl.multiple_of(base_ref[step * N_EXPERTS + e], MOE_ALIGN)
        return pltpu.make_async_copy(ys_ref.at[pl.ds(row, MOE_HALF)],
                                     ybuf.at[step % 2, pl.ds(e * MOE_HALF, MOE_HALF)],
                                     sem.at[step % 2, e])

    def copy2(e):
        row = pl.multiple_of(base_ref[i * N_EXPERTS + e] + MOE_HALF, MOE_ALIGN)
        return pltpu.make_async_copy(ys_ref.at[pl.ds(row, MOE_HALF)], ybuf2.at[e], sem2.at[e])

    def fetch(step):
        for e in range(N_EXPERTS):
            copy(step, e).start()

    pl.when(i == 0)(lambda: fetch(i))
    pl.when(i + 1 < pl.num_programs(0))(lambda: fetch(i + 1))
    for e in range(N_EXPERTS):
        pl.when(n_ref[i * N_EXPERTS + e] > MOE_HALF)(lambda e=e: copy2(e).start())

    st = st_ref[...]
    rank = _tile_rank(st)
    p = _one_hot_all(rank, st)
    for e in range(N_EXPERTS):
        copy(i, e).wait()
    tn_dims = (((0,), (0,)), ((), ()))
    o_ref[...] = x_ref[...] + lax.dot_general(p, ybuf[slot], tn_dims, preferred_element_type=F32)

    def gather2(e):
        copy2(e).wait()
        p2 = _one_hot_rows(rank, st, e, 1)
        o_ref[...] += lax.dot_general(p2, ybuf2[e], tn_dims, preferred_element_type=F32)

    for e in range(N_EXPERTS):
        pl.when(n_ref[i * N_EXPERTS + e] > MOE_HALF)(functools.partial(gather2, e))
    if out_norm:
        o_ref[...] = _rms(o_ref[...], gout_ref[...])


def _moe_layer(x, g_norm, w_router, w_gu, w_down, g_out=None):
    m, d = x.shape
    ff = w_down.shape[1]
    nt = m // MOE_TM
    i32 = jnp.int32
    hn, ct, st, cnt = pl.pallas_call(
        _route_kernel,
        grid=(nt,),
        in_specs=[
            pl.BlockSpec((MOE_TM, d), lambda i: (i, 0)),
            pl.BlockSpec((1, d), lambda i: (0, 0)),
            pl.BlockSpec((N_EXPERTS, d), lambda i: (0, 0)),
        ],
        out_specs=[
            pl.BlockSpec((MOE_TM, d), lambda i: (i, 0)),
            pl.BlockSpec((N_EXPERTS, MOE_TM), lambda i: (0, i)),
            pl.BlockSpec((N_EXPERTS, MOE_TM), lambda i: (0, i)),
            pl.BlockSpec((1, N_EXPERTS, LANES), lambda i: (i, 0, 0)),
        ],
        out_shape=[
            jax.ShapeDtypeStruct((m, d), BF16),
            jax.ShapeDtypeStruct((N_EXPERTS, m), F32),
            jax.ShapeDtypeStruct((N_EXPERTS, m), F32),
            jax.ShapeDtypeStruct((nt, N_EXPERTS, LANES), F32),
        ],
        compiler_params=_params("parallel"),
        name="moe_route",
    )(x, g_norm.reshape(1, d), w_router.T.astype(F32))

    n = cnt[:, :, 0].astype(i32)
    chunk = (n + MOE_ALIGN - 1) // MOE_ALIGN * MOE_ALIGN
    seg = (jnp.sum(chunk, axis=0) + MOE_HALF + MOE_TR - 1) // MOE_TR * MOE_TR
    seg_off = jnp.cumsum(seg) - seg
    base = (seg_off[None, :] + jnp.cumsum(chunk, axis=0) - chunk).reshape(-1).astype(i32)
    n_flat = n.reshape(-1)
    bound = 2 * m + nt * N_EXPERTS * (MOE_ALIGN - 1) + N_EXPERTS * (MOE_HALF + MOE_TR - 1)
    nb = -(-bound // MOE_TR)
    blk_end = jnp.cumsum(seg // MOE_TR)
    blk_expert = jnp.minimum(jnp.searchsorted(blk_end, jnp.arange(nb, dtype=i32), side='right'),
                             N_EXPERTS - 1).astype(i32)
    n_used = blk_end[-1:].astype(i32)
    rows = nb * MOE_TR

    tile_spec = lambda shape, imap: pl.BlockSpec(shape, imap)
    xs = pl.pallas_call(
        _dispatch_kernel,
        grid_spec=pltpu.PrefetchScalarGridSpec(
            num_scalar_prefetch=2,
            grid=(nt,),
            in_specs=[
                tile_spec((MOE_TM, d), lambda i, b, c: (i, 0)),
                tile_spec((N_EXPERTS, MOE_TM), lambda i, b, c: (0, i)),
                tile_spec((N_EXPERTS, MOE_TM), lambda i, b, c: (0, i)),
                pl.BlockSpec(memory_space=pl.ANY),
            ],
            out_specs=pl.BlockSpec(memory_space=pl.ANY),
            scratch_shapes=[
                pltpu.VMEM((N_EXPERTS * MOE_HALF, XS_W), BF16),
                pltpu.VMEM((N_EXPERTS, MOE_HALF, XS_W), BF16),
                pltpu.SemaphoreType.DMA((N_EXPERTS,)),
                pltpu.SemaphoreType.DMA((N_EXPERTS,)),
            ],
        ),
        out_shape=jax.ShapeDtypeStruct((rows, XS_W), BF16),
        input_output_aliases={5: 0},
        compiler_params=_params("arbitrary"),
        name="moe_dispatch",
    )(base, n_flat, hn, ct, st, jnp.zeros((rows, XS_W), BF16))

    ys = pl.pallas_call(
        _expert_kernel,
        grid_spec=pltpu.PrefetchScalarGridSpec(
            num_scalar_prefetch=2,
            grid=(nb,),
            in_specs=[
                pl.BlockSpec((MOE_TR, XS_W), lambda b, be, nu: (b, 0)),
                pl.BlockSpec((1, d, ff), lambda b, be, nu: (be[b], 0, 0)),
                pl.BlockSpec((1, d, ff), lambda b, be, nu: (be[b], 0, 1)),
                pl.BlockSpec((1, ff, d), lambda b, be, nu: (be[b], 0, 0)),
            ],
            out_specs=pl.BlockSpec((MOE_TR, d), lambda b, be, nu: (b, 0)),
        ),
        out_shape=jax.ShapeDtypeStruct((rows, d), BF16),
        compiler_params=_params("arbitrary"),
        name="moe_experts",
    )(blk_expert, n_used, xs, w_gu, w_gu, w_down)

    out_norm = g_out is not None
    g_out = jnp.ones((d,), F32) if g_out is None else g_out
    return pl.pallas_call(
        functools.partial(_combine_kernel, out_norm=out_norm),
        grid_spec=pltpu.PrefetchScalarGridSpec(
            num_scalar_prefetch=2,
            grid=(nt,),
            in_specs=[
                tile_spec((MOE_TM, d), lambda i, b, c: (i, 0)),
                tile_spec((N_EXPERTS, MOE_TM), lambda i, b, c: (0, i)),
                pl.BlockSpec(memory_space=pl.ANY),
                tile_spec((1, d), lambda i, b, c: (0, 0)),
            ],
            out_specs=tile_spec((MOE_TM, d), lambda i, b, c: (i, 0)),
            scratch_shapes=[
                pltpu.VMEM((2, N_EXPERTS * MOE_HALF, d), BF16),
                pltpu.VMEM((N_EXPERTS, MOE_HALF, d), BF16),
                pltpu.SemaphoreType.DMA((2, N_EXPERTS)),
                pltpu.SemaphoreType.DMA((N_EXPERTS,)),
            ],
        ),
        out_shape=jax.ShapeDtypeStruct((m, d), F32),
        compiler_params=_params("arbitrary"),
        name="moe_combine",
    )(base, n_flat, x, st, ys, g_out.reshape(1, d).astype(F32))


def _to_frame_major(x, bsz, seq):
    d = x.shape[-1]
    return x.reshape(bsz * (seq // CHUNK), CHUNK, d).transpose(1, 0, 2).reshape(bsz * seq, d)


def _from_frame_major(x, bsz, seq):
    d = x.shape[-1]
    return x.reshape(CHUNK, bsz * (seq // CHUNK), d).transpose(1, 0, 2).reshape(bsz * seq, d)


def kernel(x, mem, norm_mix, norm_mem, norm_ffn, mem_norm, final_norm, s5_w_in, s5_lam_re, s5_lam_im, s5_log_dt, s5_b_re, s5_b_im, s5_c_re, s5_c_im, s5_d, s5_w_glu, att_w_qkv, att_w_o, att_rel_bias, lru_w_in, lru_conv_w, lru_conv_b, lru_w_a, lru_b_a, lru_w_x, lru_b_x, lru_lam, lru_w_out, mem_w_q, mem_w_kv, mem_w_o, ffn_w_gu, ffn_w_down, moe_w_router, moe_w_gu, moe_w_down):
    bsz, seq, d = x.shape
    x = x.reshape(bsz * seq, d).astype(F32)

    w_kv_all = jnp.concatenate([mem_w_kv[i] for i in range(DEPTH)], axis=1).astype(BF16)
    kv_all = _norm_linear(mem.reshape(bsz * N_MEM, d).astype(F32), mem_norm, w_kv_all, BF16,
                          tm=bsz * N_MEM, tn=2 * d)

    for i in range(DEPTH):
        kind, j = i % 3, i // 3
        if kind == 0:
            xp = _to_frame_major(x, bsz, seq)
            xp = _s5_layer(xp, norm_mix[i], s5_w_in[j], s5_lam_re[j], s5_lam_im[j], s5_log_dt[j],
                           s5_b_re[j], s5_b_im[j], s5_c_re[j], s5_c_im[j], s5_d[j], s5_w_glu[j],
                           bsz, seq)
            x = _from_frame_major(xp, bsz, seq)
        elif kind == 1:
            x = _attn_layer(x, norm_mix[i], att_w_qkv[j], att_w_o[j], att_rel_bias[j], bsz, seq)
        else:
            x = _lru_layer(x, norm_mix[i], lru_w_in[j], lru_conv_w[j], lru_conv_b[j], lru_w_a[j],
                           lru_b_a[j], lru_w_x[j], lru_b_x[j], lru_lam[j], lru_w_out[j], bsz, seq)
        x = _memattn_layer(x, norm_mem[i], mem_w_q[i], kv_all, i, mem_w_o[i], bsz, seq)
        if i % 2 == 0:
            x = _ffn_layer(x, norm_ffn[i], ffn_w_gu[i // 2].astype(BF16),
                           ffn_w_down[i // 2].astype(BF16))
        else:
            x = _moe_layer(x, norm_ffn[i], moe_w_router[i // 2], moe_w_gu[i // 2].astype(BF16),
                           moe_w_down[i // 2].astype(BF16),
                           g_out=final_norm if i == DEPTH - 1 else None)
    return x.reshape(bsz, seq, d)
```

```python
import functools
import math

import jax
import jax.numpy as jnp
from jax import lax
from jax.experimental import pallas as pl
from jax.experimental.pallas import tpu as pltpu

F32 = jnp.float32
BF16 = jnp.bfloat16
HIGHEST = lax.Precision.HIGHEST

D_MODEL = 1024
DEPTH = 4
CHUNK = 64
N_MEM = 256
EPS = 1e-6

S5_GROUP = 16
S5_GROUPS = D_MODEL // S5_GROUP
S5_STATE = 64

ATT_HEADS = 16
ATT_HEAD_DIM = D_MODEL // ATT_HEADS
LEFT_CHUNKS = 8
MAX_REL = 128
ATT_TQ = 4 * CHUNK
ATT_LEFT = LEFT_CHUNKS * CHUNK
ATT_TK = ATT_LEFT + ATT_TQ
NEG_INF = -1e30
LOG2E = math.log2(math.e)

LRU_BLOCKS = 8
LRU_BLOCK_W = D_MODEL // LRU_BLOCKS
CONV_W = 4
LRU_C = 8.0
LRU_T = 512
SUBLANES = 8
LANES = 128

MEM_HEADS = 4
MEM_HEAD_DIM = D_MODEL // MEM_HEADS

N_EXPERTS = 8
FFN_TF = 256

VMEM_LIMIT = 56 * 1024 * 1024


def _params(*sem):
    return pltpu.CompilerParams(dimension_semantics=sem, vmem_limit_bytes=VMEM_LIMIT)


def _rms(x, g):
    ms = jnp.mean(x * x, axis=-1, keepdims=True)
    return x * lax.rsqrt(ms + EPS) * g


def _norm_linear_kernel(x_ref, g_ref, w_ref, o_ref, hn_ref):
    @pl.when(pl.program_id(1) == 0)
    def _():
        hn_ref[...] = _rms(x_ref[...], g_ref[...]).astype(BF16)

    o_ref[...] = jnp.dot(hn_ref[...], w_ref[...], preferred_element_type=F32).astype(o_ref.dtype)


def _norm_linear(x, g, w, out_dtype, tm=512, tn=None):
    m, d = x.shape
    n = w.shape[1]
    tn = n if tn is None else tn
    return pl.pallas_call(
        _norm_linear_kernel,
        grid=(m // tm, n // tn),
        in_specs=[
            pl.BlockSpec((tm, d), lambda i, j: (i, 0)),
            pl.BlockSpec((1, d), lambda i, j: (0, 0)),
            pl.BlockSpec((d, tn), lambda i, j: (0, j)),
        ],
        out_specs=pl.BlockSpec((tm, tn), lambda i, j: (i, j)),
        out_shape=jax.ShapeDtypeStruct((m, n), out_dtype),
        scratch_shapes=[pltpu.VMEM((tm, d), BF16)],
        compiler_params=_params("parallel", "arbitrary"),
        name="norm_linear",
    )(x, g.reshape(1, d), w)


def _s5_operators(lam_re, lam_im, log_dt, b_re, b_im, c_re, c_im, d_skip, n_chunks):
    g, n, p = S5_GROUPS, S5_STATE, S5_GROUP
    lr = lam_re.astype(F32)
    li = lam_im.astype(F32)
    dt = jnp.exp(log_dt.astype(F32))[:, None]
    mag = jnp.exp(lr * dt)
    ar = mag * jnp.cos(li * dt)
    ai = mag * jnp.sin(li * dt)
    den = lr * lr + li * li
    fr = ((ar - 1.0) * lr + ai * li) / den
    fi = (ai * lr - (ar - 1.0) * li) / den
    bbr = fr[..., None] * b_re - fi[..., None] * b_im
    bbi = fr[..., None] * b_im + fi[..., None] * b_re
    bbt = jnp.concatenate([bbr, bbi], axis=1).transpose(0, 2, 1)

    kk = jnp.arange(CHUNK + 1, dtype=F32)
    mag_k = jnp.exp((lr * dt)[..., None] * kk)
    pr = mag_k * jnp.cos((li * dt)[..., None] * kk)
    pi = mag_k * jnp.sin((li * dt)[..., None] * kk)
    crt = c_re.transpose(0, 2, 1)[..., None]
    cit = c_im.transpose(0, 2, 1)[..., None]

    def c_powers(lo):
        r = pr[:, :, None, lo:lo + CHUNK]
        i = pi[:, :, None, lo:lo + CHUNK]
        re = crt * r - cit * i
        im = crt * i + cit * r
        return jnp.concatenate([re, -im], axis=1).reshape(g, 2 * n, p * CHUNK)

    lk = c_powers(0)
    cinjt = c_powers(1).astype(BF16)
    lag0 = (jnp.arange(CHUNK) == 0).astype(F32)
    dmat = (d_skip.reshape(g, 1, p, 1) * jnp.eye(p, dtype=F32)[None, :, :, None] * lag0)
    dmat = dmat.reshape(g, p, p * CHUNK)
    rev_r = pr[..., CHUNK - 1::-1][:, :, None, :]
    rev_i = pi[..., CHUNK - 1::-1][:, :, None, :]
    bm_r = rev_r * bbr[..., None] - rev_i * bbi[..., None]
    bm_i = rev_r * bbi[..., None] + rev_i * bbr[..., None]
    bm = jnp.concatenate([bm_r, bm_i], axis=1).reshape(g, 2 * n, p * CHUNK).astype(BF16)
    qr, qi = [pr[..., CHUNK]], [pi[..., CHUNK]]
    for _ in range(int(math.log2(n_chunks)) - 1):
        qr, qi = qr + [qr[-1] * qr[-1] - qi[-1] * qi[-1]], qi + [2.0 * qr[-1] * qi[-1]]
    levels = len(qr)
    qr, qi = jnp.stack(qr), jnp.stack(qi)
    m1 = jnp.concatenate([qr, qr], axis=-1)
    m2 = jnp.concatenate([-qi, qi], axis=-1)
    ap = jnp.stack([m1, m2], axis=-1).transpose(1, 2, 0, 3).reshape(g, 2 * n, 2 * levels)
    return bbt, lk, dmat, bm, cinjt, ap


def _s5_in_kernel(x3_ref, g_ref, wt_ref, u3_ref, xbuf, ubuf, sem_in, sem_out):
    s = pl.program_id(0)
    last = pl.num_programs(0) - 1
    slot = s % 2

    def load(step):
        return pltpu.make_async_copy(x3_ref.at[pl.ds(0, x3_ref.shape[0]), step], xbuf.at[step % 2],
                                     sem_in.at[step % 2])

    def store(step):
        return pltpu.make_async_copy(ubuf.at[step % 2], u3_ref.at[pl.ds(0, u3_ref.shape[0]), step],
                                     sem_out.at[step % 2])

    pl.when(s == 0)(lambda: load(s).start())
    pl.when(s < last)(lambda: load(s + 1).start())
    load(s).wait()
    pl.when(s >= 2)(lambda: store(s - 2).wait())
    hn = _rms(xbuf[slot], g_ref[...]).astype(BF16)
    ubuf[slot] = lax.dot_general(wt_ref[...], hn, (((1,), (1,)), ((), ())), preferred_element_type=F32)
    store(s).start()

    @pl.when(s == last)
    def _():
        store(s - 1).wait()
        store(s).wait()


def _ssm_kernel(u_ref, bbt_ref, lk_ref, dmat_ref, bm_ref, cinjt_ref, ap_ref, o_ref, toet_ref, *, n_chunks):
    n = S5_STATE
    rows = S5_GROUP * CHUNK
    ncol = u_ref.shape[-1]
    u = u_ref[0].reshape(rows, ncol).astype(BF16)
    kern = jnp.dot(bbt_ref[0], lk_ref[0], preferred_element_type=F32, precision=HIGHEST) + dmat_ref[0]
    causal = (lax.broadcasted_iota(jnp.int32, (CHUNK, rows), 1) % CHUNK
              >= lax.broadcasted_iota(jnp.int32, (CHUNK, rows), 0))
    for p in range(S5_GROUP):
        base = jnp.broadcast_to(kern[p:p + 1, :], (CHUNK, rows))
        rolled = pltpu.roll(base, 0, 1, stride=1, stride_axis=0)
        toet_ref[p * CHUNK:(p + 1) * CHUNK, :] = jnp.where(causal, rolled, 0.0).astype(BF16)
    tn_dims = (((0,), (0,)), ((), ()))
    y = lax.dot_general(toet_ref[...], u, tn_dims, preferred_element_type=F32)
    x = jnp.dot(bm_ref[0], u, preferred_element_type=F32)
    ap = ap_ref[0]
    col = lax.broadcasted_iota(jnp.int32, x.shape, 1) % n_chunks
    for k in range(int(math.log2(n_chunks))):
        sh = 1 << k
        xs = jnp.where(col >= sh, pltpu.roll(x, sh, 1), 0.0)
        xsw = jnp.concatenate([xs[n:], xs[:n]], axis=0)
        x = x + ap[:, 2 * k:2 * k + 1] * xs + ap[:, 2 * k + 1:2 * k + 2] * xsw
    h0 = jnp.where(col >= 1, pltpu.roll(x, 1, 1), 0.0)
    h0_hi = h0.astype(BF16)
    h0_lo = (h0 - h0_hi.astype(F32)).astype(BF16)
    cinjt = cinjt_ref[0]
    y = y + lax.dot_general(cinjt, h0_hi, tn_dims, preferred_element_type=F32)
    y = y + lax.dot_general(cinjt, h0_lo, tn_dims, preferred_element_type=F32)
    o_ref[0] = jax.nn.gelu(y).reshape(S5_GROUP, CHUNK, ncol)


def _s5_out_kernel(v3_ref, x3_ref, w_ref, o3_ref, vbuf, xbuf, obuf, sem_v, sem_x, sem_o):
    s = pl.program_id(0)
    last = pl.num_programs(0) - 1
    slot = s % 2

    def load_v(step):
        return pltpu.make_async_copy(v3_ref.at[pl.ds(0, v3_ref.shape[0]), step], vbuf.at[step % 2],
                                     sem_v.at[step % 2])

    def load_x(step):
        return pltpu.make_async_copy(x3_ref.at[pl.ds(0, x3_ref.shape[0]), step], xbuf.at[step % 2],
                                     sem_x.at[step % 2])

    def store(step):
        return pltpu.make_async_copy(obuf.at[step % 2], o3_ref.at[pl.ds(0, o3_ref.shape[0]), step],
                                     sem_o.at[step % 2])

    def load(step):
        load_v(step).start()
        load_x(step).start()

    pl.when(s == 0)(lambda: load(s))
    pl.when(s < last)(lambda: load(s + 1))
    load_v(s).wait()
    load_x(s).wait()
    pl.when(s >= 2)(lambda: store(s - 2).wait())
    z = lax.dot_general(vbuf[slot].astype(BF16), w_ref[...], (((0,), (0,)), ((), ())),
                        preferred_element_type=F32)
    d = xbuf.shape[-1]
    obuf[slot] = xbuf[slot] + z[:, :d] * jax.nn.sigmoid(z[:, d:])
    store(s).start()

    @pl.when(s == last)
    def _():
        store(s - 1).wait()
        store(s).wait()


def _s5_layer(x, g_norm, w_in, lam_re, lam_im, log_dt, b_re, b_im, c_re, c_im, d_skip, w_glu,
              bsz, seq):
    m, d = x.shape
    n_chunks = seq // CHUNK
    ncol = bsz * n_chunks
    rows = S5_GROUP * CHUNK
    bbt, lk, dmat, bm, cinjt, ap = _s5_operators(lam_re, lam_im, log_dt, b_re, b_im, c_re, c_im,
                                                 d_skip, n_chunks)
    x3 = x.reshape(ncol, CHUNK, d)
    hbm = pl.BlockSpec(memory_space=pl.ANY)
    u3 = pl.pallas_call(
        _s5_in_kernel,
        grid=(CHUNK,),
        in_specs=[hbm, pl.BlockSpec((1, d), lambda s: (0, 0)), pl.BlockSpec((d, d), lambda s: (0, 0))],
        out_specs=hbm,
        out_shape=jax.ShapeDtypeStruct((d, CHUNK, ncol), F32),
        scratch_shapes=[
            pltpu.VMEM((2, ncol, d), F32),
            pltpu.VMEM((2, d, ncol), F32),
            pltpu.SemaphoreType.DMA((2,)),
            pltpu.SemaphoreType.DMA((2,)),
        ],
        compiler_params=_params("arbitrary"),
        name="s5_in",
    )(x3, g_norm.reshape(1, d), w_in.T.astype(BF16))
    blk4 = (1, S5_GROUP, CHUNK, ncol)
    per_group = lambda a: pl.BlockSpec((1,) + a.shape[1:], lambda g: (g, 0, 0))
    v4 = pl.pallas_call(
        functools.partial(_ssm_kernel, n_chunks=n_chunks),
        grid=(S5_GROUPS,),
        in_specs=[pl.BlockSpec(blk4, lambda g: (g, 0, 0, 0))]
        + [per_group(a) for a in (bbt, lk, dmat, bm, cinjt, ap)],
        out_specs=pl.BlockSpec(blk4, lambda g: (g, 0, 0, 0)),
        out_shape=jax.ShapeDtypeStruct((S5_GROUPS, S5_GROUP, CHUNK, ncol), F32),
        scratch_shapes=[pltpu.VMEM((rows, rows), BF16)],
        compiler_params=_params("parallel"),
        name="s5_ssm",
    )(u3.reshape(S5_GROUPS, S5_GROUP, CHUNK, ncol), bbt, lk, dmat, bm, cinjt, ap)
    o3 = pl.pallas_call(
        _s5_out_kernel,
        grid=(CHUNK,),
        in_specs=[hbm, hbm, pl.BlockSpec((d, 2 * d), lambda s: (0, 0))],
        out_specs=hbm,
        out_shape=jax.ShapeDtypeStruct((ncol, CHUNK, d), F32),
        scratch_shapes=[
            pltpu.VMEM((2, d, ncol), F32),
            pltpu.VMEM((2, ncol, d), F32),
            pltpu.VMEM((2, ncol, d), F32),
            pltpu.SemaphoreType.DMA((2,)),
            pltpu.SemaphoreType.DMA((2,)),
            pltpu.SemaphoreType.DMA((2,)),
        ],
        compiler_params=_params("arbitrary"),
        name="s5_out",
    )(v4.reshape(d, CHUNK, ncol), x3, w_glu.astype(BF16))
    return o3.reshape(m, d)


ATT_PERIOD = ATT_TQ + ATT_TK


def _attn_bias_seed(rel_bias):
    k = jnp.arange(ATT_PERIOD)
    rel = jnp.where(k < ATT_TK, ATT_LEFT - k, ATT_LEFT + ATT_PERIOD - k)
    return jnp.take(rel_bias.astype(F32), jnp.clip(rel, -MAX_REL, MAX_REL) + MAX_REL, axis=1)


def _attn_kernel(x_ref, q_ref, k0_ref, k1_ref, k2_ref, v0_ref, v1_ref, v2_ref, seed_ref, wo_ref,
                 o_ref, bias_ref, ocat_ref):
    i = pl.program_id(1)

    @pl.when((pl.program_id(0) == 0) & (i == 0))
    def _():
        r = lax.broadcasted_iota(jnp.int32, (ATT_TQ, ATT_TK), 0)
        j = lax.broadcasted_iota(jnp.int32, (ATT_TQ, ATT_TK), 1)
        dchunk = (r + ATT_LEFT) // CHUNK - j // CHUNK
        valid = (dchunk >= 0) & (dchunk <= LEFT_CHUNKS)
        for h in range(ATT_HEADS):
            seed = jnp.broadcast_to(seed_ref[h:h + 1, :], (ATT_TQ, ATT_PERIOD))
            rolled = pltpu.roll(seed, 0, 1, stride=1, stride_axis=0)
            bias_ref[h] = jnp.where(valid, rolled[:, :ATT_TK] * LOG2E, NEG_INF)

    lane = lax.broadcasted_iota(jnp.int32, (1, LANES), 1)
    col = lax.broadcasted_iota(jnp.int32, (ATT_TQ, ATT_TK), 1)
    pad = col < ATT_LEFT - i * ATT_TQ
    scale = ATT_HEAD_DIM ** -0.5
    heads_per = LANES // ATT_HEAD_DIM
    for hp in range(D_MODEL // LANES):
        sl = slice(hp * LANES, (hp + 1) * LANES)
        q = q_ref[:, sl] * scale
        k = jnp.concatenate([k0_ref[:, sl], k1_ref[:, sl], k2_ref[:, sl]], axis=0)
        v = jnp.concatenate([v0_ref[:, sl], v1_ref[:, sl], v2_ref[:, sl]], axis=0)
        acc = jnp.zeros((ATT_TQ, LANES), F32)
        for h in range(heads_per):
            in_head = (lane >= h * ATT_HEAD_DIM) & (lane < (h + 1) * ATT_HEAD_DIM)
            qh = jnp.where(in_head, q, jnp.zeros_like(q))
            s = lax.dot_general(qh, k, (((1,), (1,)), ((), ())), preferred_element_type=F32)
            s = jnp.where(pad, NEG_INF, s + bias_ref[hp * heads_per + h])
            p = jnp.exp2(s - jnp.max(s, axis=-1, keepdims=True))
            vh = jnp.where(in_head, v, jnp.ones_like(v))
            pv = jnp.dot(p.astype(BF16), vh, preferred_element_type=F32)
            acc = jnp.where(in_head, pv / pltpu.roll(pv, ATT_HEAD_DIM, 1), acc)
        ocat_ref[:, sl] = acc.astype(BF16)
    o_ref[...] = x_ref[...] + jnp.dot(ocat_ref[...], wo_ref[...], preferred_element_type=F32)


def _attn_layer(x, g_norm, w_qkv, w_o, rel_bias, bsz, seq):
    m, d = x.shape
    w_qkv = jnp.concatenate([w_qkv[:, :d] * LOG2E, w_qkv[:, d:]], axis=1)
    qkv = _norm_linear(x, g_norm, w_qkv.astype(BF16), BF16)
    nq = seq // ATT_TQ

    def rows(col, back):
        return lambda b, i: (b * nq + jnp.maximum(i - back, 0), col)

    blk = (ATT_TQ, d)
    nkb = ATT_TK // ATT_TQ
    return pl.pallas_call(
        _attn_kernel,
        grid=(bsz, nq),
        in_specs=(
            [pl.BlockSpec(blk, rows(0, 0)), pl.BlockSpec(blk, rows(0, 0))]
            + [pl.BlockSpec(blk, rows(1, nkb - 1 - kb)) for kb in range(nkb)]
            + [pl.BlockSpec(blk, rows(2, nkb - 1 - kb)) for kb in range(nkb)]
            + [pl.BlockSpec((ATT_HEADS, ATT_PERIOD), lambda b, i: (0, 0)),
               pl.BlockSpec((d, d), lambda b, i: (0, 0))]
        ),
        out_specs=pl.BlockSpec(blk, rows(0, 0)),
        out_shape=jax.ShapeDtypeStruct((m, d), F32),
        scratch_shapes=[pltpu.VMEM((ATT_HEADS, ATT_TQ, ATT_TK), F32), pltpu.VMEM((ATT_TQ, d), BF16)],
        compiler_params=_params("arbitrary", "arbitrary"),
        name="chunk_attn",
    )(x, *([qkv] * (1 + 2 * nkb)), _attn_bias_seed(rel_bias), w_o.astype(BF16))


def _lru_kernel(x_ref, g_ref, win_ref, cw_ref, cb_ref, wax_ref, ba_ref, bx_ref, lam_ref, wout_ref,
                o_ref, prev_ref, carry_ref, a_s, b_s):
    t, w = x_ref.shape

    @pl.when(pl.program_id(1) == 0)
    def _():
        prev_ref[...] = jnp.zeros_like(prev_ref)
        carry_ref[...] = jnp.zeros_like(carry_ref)

    x = x_ref[...]
    z = jnp.dot(_rms(x, g_ref[...]).astype(BF16), win_ref[...], preferred_element_type=F32)
    gate = z[:, :w]
    xr = z[:, w:]
    xext = jnp.concatenate([prev_ref[...], xr], axis=0)
    cw = cw_ref[...]
    xc = cb_ref[...] + cw[0:1] * xext[SUBLANES - 3:SUBLANES - 3 + t]
    for k in range(1, CONV_W):
        xc = xc + cw[k:k + 1] * xext[SUBLANES - 3 + k:SUBLANES - 3 + k + t]
    prev_ref[...] = xr[t - SUBLANES:]

    xcb = xc.astype(BF16)
    pre_a, pre_x = [], []
    for blk in range(LRU_BLOCKS):
        pre = jnp.dot(xcb[:, blk * LRU_BLOCK_W:(blk + 1) * LRU_BLOCK_W], wax_ref[blk],
                      preferred_element_type=F32)
        pre_a.append(pre[:, :LRU_BLOCK_W])
        pre_x.append(pre[:, LRU_BLOCK_W:])
    r = jax.nn.sigmoid(jnp.concatenate(pre_a, axis=1) + ba_ref[...])
    ig = jax.nn.sigmoid(jnp.concatenate(pre_x, axis=1) + bx_ref[...])
    z = -lam_ref[...]
    softplus = jnp.maximum(z, 0.0) + jnp.log1p(jnp.exp(-jnp.abs(z)))
    log_a = -LRU_C * r * softplus
    a = jnp.exp(log_a)
    b = jnp.sqrt(1.0 - a * a) * (ig * xc)

    a3 = a.reshape(t // SUBLANES, SUBLANES, w)
    b3 = b.reshape(t // SUBLANES, SUBLANES, w)
    sub = lax.broadcasted_iota(jnp.int32, a3.shape, 1)
    for sh in (1, 2, 4):
        a_sh = jnp.where(sub >= sh, pltpu.roll(a3, sh, 1), 1.0)
        b_sh = jnp.where(sub >= sh, pltpu.roll(b3, sh, 1), 0.0)
        b3 = a3 * b_sh + b3
        a3 = a3 * a_sh
    a_s[...] = a3
    b_s[...] = b3

    def body(j, carry):
        hb = b_s[j] + a_s[j] * carry
        b_s[j] = hb
        return hb[SUBLANES - 1:SUBLANES, :]

    carry = lax.fori_loop(0, t // SUBLANES, body, carry_ref[0:1, :])
    carry_ref[0:1, :] = carry
    h = b_s[...].reshape(t, w)
    y = (jax.nn.gelu(gate) * h).astype(BF16)
    o_ref[...] = x + jnp.dot(y, wout_ref[...], preferred_element_type=F32)


def _lru_layer(x, g_norm, w_in, conv_w, conv_b, w_a, b_a, w_x, b_x, lam, w_out, bsz, seq):
    m, d = x.shape
    wax = jnp.concatenate([w_a, w_x], axis=-1).astype(BF16)
    nt = seq // LRU_T
    row = lambda v: v.reshape(1, d).astype(F32)
    vec = pl.BlockSpec((1, d), lambda b, i: (0, 0))
    return pl.pallas_call(
        _lru_kernel,
        grid=(bsz, nt),
        in_specs=[
            pl.BlockSpec((LRU_T, d), lambda b, i: (b * nt + i, 0)),
            vec,
            pl.BlockSpec((d, 2 * d), lambda b, i: (0, 0)),
            pl.BlockSpec((CONV_W, d), lambda b, i: (0, 0)),
            vec,
            pl.BlockSpec(wax.shape, lambda b, i: (0, 0, 0)),
            vec, vec, vec,
            pl.BlockSpec((d, d), lambda b, i: (0, 0)),
        ],
        out_specs=pl.BlockSpec((LRU_T, d), lambda b, i: (b * nt + i, 0)),
        out_shape=jax.ShapeDtypeStruct((m, d), F32),
        scratch_shapes=[
            pltpu.VMEM((SUBLANES, d), F32),
            pltpu.VMEM((SUBLANES, d), F32),
            pltpu.VMEM((LRU_T // SUBLANES, SUBLANES, d), F32),
            pltpu.VMEM((LRU_T // SUBLANES, SUBLANES, d), F32),
        ],
        compiler_params=_params("parallel", "arbitrary"),
        name="rglru",
    )(x, row(g_norm), w_in.astype(BF16), conv_w.astype(F32), row(conv_b), wax, row(b_a), row(b_x),
      row(lam), w_out.astype(BF16))


def _memattn_kernel(x_ref, g_ref, wq_ref, k_ref, v_ref, wo_ref, o_ref, ocat_ref):
    x = x_ref[...]
    hn = _rms(x, g_ref[...]).astype(BF16)
    q = jnp.dot(hn, wq_ref[...], preferred_element_type=F32) * (MEM_HEAD_DIM ** -0.5)
    q = q.astype(BF16)
    for h in range(MEM_HEADS):
        sl = slice(h * MEM_HEAD_DIM, (h + 1) * MEM_HEAD_DIM)
        s = lax.dot_general(q[:, sl], k_ref[:, sl], (((1,), (1,)), ((), ())),
                            preferred_element_type=F32)
        mx = jnp.max(s, axis=-1, keepdims=True)
        p = jnp.exp(s - mx)
        l = jnp.sum(p, axis=-1, keepdims=True)
        o = jnp.dot(p.astype(BF16), v_ref[:, sl], preferred_element_type=F32) / l
        ocat_ref[:, sl] = o.astype(BF16)
    o_ref[...] = x + jnp.dot(ocat_ref[...], wo_ref[...], preferred_element_type=F32)


def _memattn_layer(x, g_norm, w_q, kv_all, layer, w_o, bsz, seq, tm=512):
    m, d = x.shape
    nt = seq // tm
    wspec = pl.BlockSpec((None, d, d), lambda b, i: (layer, 0, 0))
    return pl.pallas_call(
        _memattn_kernel,
        grid=(bsz, nt),
        in_specs=[
            pl.BlockSpec((tm, d), lambda b, i: (b * nt + i, 0)),
            pl.BlockSpec((1, d), lambda b, i: (0, 0)),
            wspec,
            pl.BlockSpec((N_MEM, d), lambda b, i: (b, 2 * layer)),
            pl.BlockSpec((N_MEM, d), lambda b, i: (b, 2 * layer + 1)),
            wspec,
        ],
        out_specs=pl.BlockSpec((tm, d), lambda b, i: (b * nt + i, 0)),
        out_shape=jax.ShapeDtypeStruct((m, d), F32),
        scratch_shapes=[pltpu.VMEM((tm, d), BF16)],
        compiler_params=_params("parallel", "parallel"),
        name="mem_attn",
    )(x, g_norm.reshape(1, d), w_q, kv_all, kv_all, w_o)


def _ffn_kernel(x_ref, g_ref, wgu_ref, wd_ref, o_ref):
    ff = wd_ref.shape[0]
    x = x_ref[...]
    hn = _rms(x, g_ref[...]).astype(BF16)
    o_ref[...] = x
    for f in range(ff // FFN_TF):
        lo = f * FFN_TF
        gate = jnp.dot(hn, wgu_ref[:, lo:lo + FFN_TF], preferred_element_type=F32)
        up = jnp.dot(hn, wgu_ref[:, ff + lo:ff + lo + FFN_TF], preferred_element_type=F32)
        act = gate * jax.nn.sigmoid(gate) * up
        o_ref[...] += jnp.dot(act.astype(BF16), wd_ref[lo:lo + FFN_TF, :], preferred_element_type=F32)


def _ffn_layer(x, g_norm, w_gu, w_down, layer, tm=1024):
    m, d = x.shape
    ff = w_down.shape[1]
    resident = lambda shape: pl.BlockSpec((None,) + shape, lambda i: (layer, 0, 0),
                                          pipeline_mode=pl.Buffered(1))
    return pl.pallas_call(
        _ffn_kernel,
        grid=(m // tm,),
        in_specs=[
            pl.BlockSpec((tm, d), lambda i: (i, 0)),
            pl.BlockSpec((1, d), lambda i: (0, 0)),
            resident((d, 2 * ff)),
            resident((ff, d)),
        ],
        out_specs=pl.BlockSpec((tm, d), lambda i: (i, 0)),
        out_shape=jax.ShapeDtypeStruct((m, d), F32),
        compiler_params=_params("parallel"),
        name="ffn",
    )(x, g_norm.reshape(1, d), w_gu, w_down)


MOE_TM = 512
MOE_TR = 512
MOE_HALF = MOE_TM // 2
MOE_ALIGN = 16
MOE_TF = 512
XS_W = D_MODEL + LANES


def _route_kernel(x_ref, g_ref, wrt_ref, hn_ref, ct_ref, st_ref, cnt_ref):
    hn = _rms(x_ref[...], g_ref[...])
    hn_ref[...] = hn.astype(BF16)
    lg = lax.dot_general(wrt_ref[...], hn, (((1,), (1,)), ((), ())),
                         preferred_element_type=F32, precision=HIGHEST)
    idx = lax.broadcasted_iota(jnp.int32, lg.shape, 0)
    m1 = jnp.max(lg, axis=0, keepdims=True)
    i1 = jnp.min(jnp.where(lg == m1, idx, N_EXPERTS), axis=0, keepdims=True)
    oh1 = idx == i1
    lg2 = jnp.where(oh1, -jnp.inf, lg)
    m2 = jnp.max(lg2, axis=0, keepdims=True)
    i2 = jnp.min(jnp.where(lg2 == m2, idx, N_EXPERTS), axis=0, keepdims=True)
    oh2 = idx == i2
    e2 = jnp.exp(m2 - m1)
    g1 = 1.0 / (1.0 + e2)
    g2 = e2 / (1.0 + e2)
    ct_ref[...] = jnp.where(oh1, g1, 0.0) + jnp.where(oh2, g2, 0.0)
    sel = (oh1 | oh2).astype(F32)
    st_ref[...] = sel
    cnt_ref[0] = jnp.broadcast_to(jnp.sum(sel, axis=1, keepdims=True), (N_EXPERTS, LANES))


def _tile_rank(st):
    tm = st.shape[1]
    before = (lax.broadcasted_iota(jnp.int32, (tm, tm), 0)
              < lax.broadcasted_iota(jnp.int32, (tm, tm), 1))
    return jnp.dot(st.astype(BF16), before.astype(BF16), preferred_element_type=F32)


def _one_hot_rows(rank, st, e, half):
    tm = rank.shape[1]
    rows = lax.broadcasted_iota(jnp.int32, (MOE_HALF, tm), 0).astype(F32) + float(half * MOE_HALF)
    hit = (rank[e:e + 1] == rows) & (st[e:e + 1] > 0.0)
    return jnp.where(hit, 1.0, 0.0).astype(BF16)


def _one_hot_all(rank, st):
    return jnp.concatenate([_one_hot_rows(rank, st, e, 0) for e in range(N_EXPERTS)], axis=0)


def _dispatch_kernel(base_ref, n_ref, hn_ref, ct_ref, st_ref, xs_init_ref, xs_ref, stage, stage2,
                     sem, sem2):
    del xs_init_ref
    i = pl.program_id(0)
    last = pl.num_programs(0) - 1
    slot = i % 2
    hn = hn_ref[...]
    ct = ct_ref[...]
    st = st_ref[...]
    rank = _tile_rank(st)
    g_hi = ct.astype(BF16)
    g_lo = (ct - g_hi.astype(F32)).astype(BF16)
    gm = jnp.concatenate(
        [g_hi, g_lo, jnp.zeros((LANES - 2 * N_EXPERTS, ct.shape[1]), BF16)], axis=0)
    nt_dims = (((1,), (1,)), ((), ()))

    def copy(step, e):
        row = pl.multiple_of(base_ref[step * N_EXPERTS + e], MOE_ALIGN)
        return pltpu.make_async_copy(stage.at[step % 2, pl.ds(e * MOE_HALF, MOE_HALF)],
                                     xs_ref.at[pl.ds(row, MOE_HALF)], sem.at[step % 2, e])

    def copy2(e):
        row = pl.multiple_of(base_ref[i * N_EXPERTS + e] + MOE_HALF, MOE_ALIGN)
        return pltpu.make_async_copy(stage2.at[e], xs_ref.at[pl.ds(row, MOE_HALF)], sem2.at[e])

    def drain(step):
        for e in range(N_EXPERTS):
            copy(step, e).wait()

    p = _one_hot_all(rank, st)
    stage[slot, :, :D_MODEL] = jnp.dot(p, hn, preferred_element_type=F32).astype(BF16)
    stage[slot, :, D_MODEL:] = lax.dot_general(p, gm, nt_dims, preferred_element_type=F32).astype(BF16)
    pl.when(i >= 1)(lambda: drain(i - 1))
    for e in range(N_EXPERTS):
        copy(i, e).start()

    def fill2(e):
        p2 = _one_hot_rows(rank, st, e, 1)
        stage2[e, :, :D_MODEL] = jnp.dot(p2, hn, preferred_element_type=F32).astype(BF16)
        stage2[e, :, D_MODEL:] = lax.dot_general(p2, gm, nt_dims, preferred_element_type=F32).astype(BF16)
        copy2(e).start()

    for e in range(N_EXPERTS):
        pl.when(n_ref[i * N_EXPERTS + e] > MOE_HALF)(functools.partial(fill2, e))
    for e in range(N_EXPERTS):
        pl.when(n_ref[i * N_EXPERTS + e] > MOE_HALF)(lambda e=e: copy2(e).wait())

    pl.when(i == last)(lambda: drain(i))


def _expert_kernel(be_ref, nu_ref, x_ref, wg_ref, wu_ref, wd_ref, o_ref, acc_ref):
    b = pl.program_id(0)

    @pl.when(b < nu_ref[0])
    def _():
        e = be_ref[b]
        xe = x_ref[...]
        h = xe[:, :D_MODEL]
        ext = xe[:, D_MODEL:].astype(F32)
        lane = lax.broadcasted_iota(jnp.int32, ext.shape, 1)
        g = jnp.sum(jnp.where((lane == e) | (lane == e + N_EXPERTS), ext, 0.0), axis=1, keepdims=True)
        ff = wd_ref.shape[1]
        for c in range(ff // MOE_TF):
            lo = c * MOE_TF
            gate = jnp.dot(h, wg_ref[0, :, lo:lo + MOE_TF], preferred_element_type=F32)
            up = jnp.dot(h, wu_ref[0, :, lo:lo + MOE_TF], preferred_element_type=F32)
            act = (gate * jax.nn.sigmoid(gate) * up * g).astype(BF16)
            part = jnp.dot(act, wd_ref[0, lo:lo + MOE_TF, :], preferred_element_type=F32)
            if c == 0:
                acc_ref[...] = part
            else:
                acc_ref[...] += part
        o_ref[...] = acc_ref[...].astype(o_ref.dtype)

    @pl.when(b >= nu_ref[0])
    def _():
        o_ref[...] = jnp.zeros_like(o_ref)


def _combine_kernel(base_ref, n_ref, x_ref, st_ref, ys_ref, gout_ref, o_ref, ybuf, ybuf2, sem, sem2,
                    *, out_norm):
    i = pl.program_id(0)
    slot = i % 2

    def copy(step, e):
        row = pl.multiple_of(base_ref[step * N_EXPERTS + e], MOE_ALIGN)
        return pltpu.make_async_copy(ys_ref.at[pl.ds(row, MOE_HALF)],
                                     ybuf.at[step % 2, pl.ds(e * MOE_HALF, MOE_HALF)],
                                     sem.at[step % 2, e])

    def copy2(e):
        row = pl.multiple_of(base_ref[i * N_EXPERTS + e] + MOE_HALF, MOE_ALIGN)
        return pltpu.make_async_copy(ys_ref.at[pl.ds(row, MOE_HALF)], ybuf2.at[e], sem2.at[e])

    def fetch(step):
        for e in range(N_EXPERTS):
            copy(step, e).start()

    pl.when(i == 0)(lambda: fetch(i))
    pl.when(i + 1 < pl.num_programs(0))(lambda: fetch(i + 1))
    for e in range(N_EXPERTS):
        pl.when(n_ref[i * N_EXPERTS + e] > MOE_HALF)(lambda e=e: copy2(e).start())

    st = st_ref[...]
    rank = _tile_rank(st)
    p = _one_hot_all(rank, st)
    for e in range(N_EXPERTS):
        copy(i, e).wait()
    tn_dims = (((0,), (0,)), ((), ()))
    o_ref[...] = x_ref[...] + lax.dot_general(p, ybuf[slot], tn_dims, preferred_element_type=F32)

    def gather2(e):
        copy2(e).wait()
        p2 = _one_hot_rows(rank, st, e, 1)
        o_ref[...] += lax.dot_general(p2, ybuf2[e], tn_dims, preferred_element_type=F32)

    for e in range(N_EXPERTS):
        pl.when(n_ref[i * N_EXPERTS + e] > MOE_HALF)(functools.partial(gather2, e))
    if out_norm:
        o_ref[...] = _rms(o_ref[...], gout_ref[...])


def _moe_layer(x, g_norm, w_router, w_gu, w_down, layer, g_out=None):
    m, d = x.shape
    ff = w_down.shape[2]
    nt = m // MOE_TM
    i32 = jnp.int32
    hn, ct, st, cnt = pl.pallas_call(
        _route_kernel,
        grid=(nt,),
        in_specs=[
            pl.BlockSpec((MOE_TM, d), lambda i: (i, 0)),
            pl.BlockSpec((1, d), lambda i: (0, 0)),
            pl.BlockSpec((N_EXPERTS, d), lambda i: (0, 0)),
        ],
        out_specs=[
            pl.BlockSpec((MOE_TM, d), lambda i: (i, 0)),
            pl.BlockSpec((N_EXPERTS, MOE_TM), lambda i: (0, i)),
            pl.BlockSpec((N_EXPERTS, MOE_TM), lambda i: (0, i)),
            pl.BlockSpec((1, N_EXPERTS, LANES), lambda i: (i, 0, 0)),
        ],
        out_shape=[
            jax.ShapeDtypeStruct((m, d), BF16),
            jax.ShapeDtypeStruct((N_EXPERTS, m), F32),
            jax.ShapeDtypeStruct((N_EXPERTS, m), F32),
            jax.ShapeDtypeStruct((nt, N_EXPERTS, LANES), F32),
        ],
        compiler_params=_params("parallel"),
        name="moe_route",
    )(x, g_norm.reshape(1, d), w_router.T.astype(F32))

    n = cnt[:, :, 0].astype(i32)
    chunk = (n + MOE_ALIGN - 1) // MOE_ALIGN * MOE_ALIGN
    seg = (jnp.sum(chunk, axis=0) + MOE_HALF + MOE_TR - 1) // MOE_TR * MOE_TR
    seg_off = jnp.cumsum(seg) - seg
    base = (seg_off[None, :] + jnp.cumsum(chunk, axis=0) - chunk).reshape(-1).astype(i32)
    n_flat = n.reshape(-1)
    bound = 2 * m + nt * N_EXPERTS * (MOE_ALIGN - 1) + N_EXPERTS * (MOE_HALF + MOE_TR - 1)
    nb = -(-bound // MOE_TR)
    blk_end = jnp.cumsum(seg // MOE_TR)
    blk_expert = jnp.minimum(jnp.searchsorted(blk_end, jnp.arange(nb, dtype=i32), side='right'),
                             N_EXPERTS - 1).astype(i32)
    n_used = blk_end[-1:].astype(i32)
    rows = nb * MOE_TR

    tile_spec = lambda shape, imap: pl.BlockSpec(shape, imap)
    xs = pl.pallas_call(
        _dispatch_kernel,
        grid_spec=pltpu.PrefetchScalarGridSpec(
            num_scalar_prefetch=2,
            grid=(nt,),
            in_specs=[
                tile_spec((MOE_TM, d), lambda i, b, c: (i, 0)),
                tile_spec((N_EXPERTS, MOE_TM), lambda i, b, c: (0, i)),
                tile_spec((N_EXPERTS, MOE_TM), lambda i, b, c: (0, i)),
                pl.BlockSpec(memory_space=pl.ANY),
            ],
            out_specs=pl.BlockSpec(memory_space=pl.ANY),
            scratch_shapes=[
                pltpu.VMEM((2, N_EXPERTS * MOE_HALF, XS_W), BF16),
                pltpu.VMEM((N_EXPERTS, MOE_HALF, XS_W), BF16),
                pltpu.SemaphoreType.DMA((2, N_EXPERTS)),
                pltpu.SemaphoreType.DMA((N_EXPERTS,)),
            ],
        ),
        out_shape=jax.ShapeDtypeStruct((rows, XS_W), BF16),
        input_output_aliases={5: 0},
        compiler_params=_params("arbitrary"),
        name="moe_dispatch",
    )(base, n_flat, hn, ct, st, jnp.zeros((rows, XS_W), BF16))

    ys = pl.pallas_call(
        _expert_kernel,
        grid_spec=pltpu.PrefetchScalarGridSpec(
            num_scalar_prefetch=2,
            grid=(nb,),
            in_specs=[
                pl.BlockSpec((MOE_TR, XS_W), lambda b, be, nu: (b, 0)),
                pl.BlockSpec((None, 1, d, ff), lambda b, be, nu: (layer, be[b], 0, 0)),
                pl.BlockSpec((None, 1, d, ff), lambda b, be, nu: (layer, be[b], 0, 1)),
                pl.BlockSpec((None, 1, ff, d), lambda b, be, nu: (layer, be[b], 0, 0)),
            ],
            out_specs=pl.BlockSpec((MOE_TR, d), lambda b, be, nu: (b, 0)),
            scratch_shapes=[pltpu.VMEM((MOE_TR, d), F32)],
        ),
        out_shape=jax.ShapeDtypeStruct((rows, d), BF16),
        compiler_params=_params("arbitrary"),
        name="moe_experts",
    )(blk_expert, n_used, xs, w_gu, w_gu, w_down)

    out_norm = g_out is not None
    g_out = jnp.ones((d,), F32) if g_out is None else g_out
    return pl.pallas_call(
        functools.partial(_combine_kernel, out_norm=out_norm),
        grid_spec=pltpu.PrefetchScalarGridSpec(
            num_scalar_prefetch=2,
            grid=(nt,),
            in_specs=[
                tile_spec((MOE_TM, d), lambda i, b, c: (i, 0)),
                tile_spec((N_EXPERTS, MOE_TM), lambda i, b, c: (0, i)),
                pl.BlockSpec(memory_space=pl.ANY),
                tile_spec((1, d), lambda i, b, c: (0, 0)),
            ],
            out_specs=tile_spec((MOE_TM, d), lambda i, b, c: (i, 0)),
            scratch_shapes=[
                pltpu.VMEM((2, N_EXPERTS * MOE_HALF, d), BF16),
                pltpu.VMEM((N_EXPERTS, MOE_HALF, d), BF16),
                pltpu.SemaphoreType.DMA((2, N_EXPERTS)),
                pltpu.SemaphoreType.DMA((N_EXPERTS,)),
            ],
        ),
        out_shape=jax.ShapeDtypeStruct((m, d), F32),
        compiler_params=_params("arbitrary"),
        name="moe_combine",
    )(base, n_flat, x, st, ys, g_out.reshape(1, d).astype(F32))


def kernel(x, mem, norm_mix, norm_mem, norm_ffn, mem_norm, final_norm, s5_w_in, s5_lam_re, s5_lam_im, s5_log_dt, s5_b_re, s5_b_im, s5_c_re, s5_c_im, s5_d, s5_w_glu, att_w_qkv, att_w_o, att_rel_bias, lru_w_in, lru_conv_w, lru_conv_b, lru_w_a, lru_b_a, lru_w_x, lru_b_x, lru_lam, lru_w_out, mem_w_q, mem_w_kv, mem_w_o, ffn_w_gu, ffn_w_down, moe_w_router, moe_w_gu, moe_w_down):
    bsz, seq, d = x.shape
    x = x.reshape(bsz * seq, d).astype(F32)

    w_kv_all = jnp.concatenate([mem_w_kv[i] for i in range(DEPTH)], axis=1).astype(BF16)
    kv_all = _norm_linear(mem.reshape(bsz * N_MEM, d).astype(F32), mem_norm, w_kv_all, BF16,
                          tm=bsz * N_MEM, tn=2 * d)

    mem_wq, mem_wo = mem_w_q.astype(BF16), mem_w_o.astype(BF16)
    ffn_wgu, ffn_wd = ffn_w_gu.astype(BF16), ffn_w_down.astype(BF16)
    moe_wgu, moe_wd = moe_w_gu.astype(BF16), moe_w_down.astype(BF16)
    for i in range(DEPTH):
        kind, j = i % 3, i // 3
        if kind == 0:
            x = _s5_layer(x, norm_mix[i], s5_w_in[j], s5_lam_re[j], s5_lam_im[j], s5_log_dt[j],
                          s5_b_re[j], s5_b_im[j], s5_c_re[j], s5_c_im[j], s5_d[j], s5_w_glu[j],
                          bsz, seq)
        elif kind == 1:
            x = _attn_layer(x, norm_mix[i], att_w_qkv[j], att_w_o[j], att_rel_bias[j], bsz, seq)
        else:
            x = _lru_layer(x, norm_mix[i], lru_w_in[j], lru_conv_w[j], lru_conv_b[j], lru_w_a[j],
                           lru_b_a[j], lru_w_x[j], lru_b_x[j], lru_lam[j], lru_w_out[j], bsz, seq)
        x = _memattn_layer(x, norm_mem[i], mem_wq, kv_all, i, mem_wo, bsz, seq)
        if i % 2 == 0:
            x = _ffn_layer(x, norm_ffn[i], ffn_wgu, ffn_wd, i // 2)
        else:
            x = _moe_layer(x, norm_ffn[i], moe_w_router[i // 2], moe_wgu, moe_wd, i // 2,
                           g_out=final_norm if i == DEPTH - 1 else None)
    return x.reshape(bsz, seq, d)
```

```python
import functools
import math

import jax
import jax.numpy as jnp
from jax import lax
from jax.experimental import pallas as pl
from jax.experimental.pallas import tpu as pltpu

F32 = jnp.float32
BF16 = jnp.bfloat16
HIGHEST = lax.Precision.HIGHEST

D_MODEL = 1024
DEPTH = 4
CHUNK = 64
N_MEM = 256
EPS = 1e-6

S5_GROUP = 16
S5_GROUPS = D_MODEL // S5_GROUP
S5_STATE = 64

ATT_HEADS = 16
ATT_HEAD_DIM = D_MODEL // ATT_HEADS
LEFT_CHUNKS = 8
MAX_REL = 128
ATT_TQ = 4 * CHUNK
ATT_LEFT = LEFT_CHUNKS * CHUNK
ATT_TK = ATT_LEFT + ATT_TQ
NEG_INF = -1e30
LOG2E = math.log2(math.e)

LRU_BLOCKS = 8
LRU_BLOCK_W = D_MODEL // LRU_BLOCKS
CONV_W = 4
LRU_C = 8.0
LRU_T = 512
SUBLANES = 8
LANES = 128

MEM_HEADS = 4
MEM_HEAD_DIM = D_MODEL // MEM_HEADS

N_EXPERTS = 8
FFN_TF = 256

VMEM_LIMIT = 56 * 1024 * 1024


def _params(*sem):
    return pltpu.CompilerParams(dimension_semantics=sem, vmem_limit_bytes=VMEM_LIMIT)


def _rms(x, g):
    ms = jnp.mean(x * x, axis=-1, keepdims=True)
    return x * lax.rsqrt(ms + EPS) * g


def _norm_linear_kernel(x_ref, g_ref, w_ref, o_ref, hn_ref):
    @pl.when(pl.program_id(1) == 0)
    def _():
        hn_ref[...] = _rms(x_ref[...], g_ref[...]).astype(BF16)

    o_ref[...] = jnp.dot(hn_ref[...], w_ref[...], preferred_element_type=F32).astype(o_ref.dtype)


def _norm_linear(x, g, w, out_dtype, tm=512, tn=None):
    m, d = x.shape
    n = w.shape[1]
    tn = n if tn is None else tn
    return pl.pallas_call(
        _norm_linear_kernel,
        grid=(m // tm, n // tn),
        in_specs=[
            pl.BlockSpec((tm, d), lambda i, j: (i, 0)),
            pl.BlockSpec((1, d), lambda i, j: (0, 0)),
            pl.BlockSpec((d, tn), lambda i, j: (0, j)),
        ],
        out_specs=pl.BlockSpec((tm, tn), lambda i, j: (i, j)),
        out_shape=jax.ShapeDtypeStruct((m, n), out_dtype),
        scratch_shapes=[pltpu.VMEM((tm, d), BF16)],
        compiler_params=_params("parallel", "arbitrary"),
        name="norm_linear",
    )(x, g.reshape(1, d), w)


def _s5_operators(lam_re, lam_im, log_dt, b_re, b_im, c_re, c_im, d_skip, n_chunks):
    g, n, p = S5_GROUPS, S5_STATE, S5_GROUP
    lr = lam_re.astype(F32)
    li = lam_im.astype(F32)
    dt = jnp.exp(log_dt.astype(F32))[:, None]
    mag = jnp.exp(lr * dt)
    ar = mag * jnp.cos(li * dt)
    ai = mag * jnp.sin(li * dt)
    den = lr * lr + li * li
    fr = ((ar - 1.0) * lr + ai * li) / den
    fi = (ai * lr - (ar - 1.0) * li) / den
    bbr = fr[..., None] * b_re - fi[..., None] * b_im
    bbi = fr[..., None] * b_im + fi[..., None] * b_re
    bbt = jnp.concatenate([bbr, bbi], axis=1).transpose(0, 2, 1)

    kk = jnp.arange(CHUNK + 1, dtype=F32)
    mag_k = jnp.exp((lr * dt)[..., None] * kk)
    pr = mag_k * jnp.cos((li * dt)[..., None] * kk)
    pi = mag_k * jnp.sin((li * dt)[..., None] * kk)
    pk = jnp.concatenate([pr[..., :CHUNK], pi[..., :CHUNK], pr[..., 1:], pi[..., 1:],
                          pr[..., CHUNK - 1::-1], pi[..., CHUNK - 1::-1]], axis=-1)
    cb = jnp.concatenate([c_re.transpose(0, 2, 1), c_im.transpose(0, 2, 1), bbr, bbi], axis=-1)
    dq = jnp.broadcast_to(d_skip.reshape(g, p, 1), (g, p, LANES))
    qr, qi = [pr[..., CHUNK]], [pi[..., CHUNK]]
    for _ in range(int(math.log2(n_chunks)) - 1):
        qr, qi = qr + [qr[-1] * qr[-1] - qi[-1] * qi[-1]], qi + [2.0 * qr[-1] * qi[-1]]
    levels = len(qr)
    qr, qi = jnp.stack(qr), jnp.stack(qi)
    m1 = jnp.concatenate([qr, qr], axis=-1)
    m2 = jnp.concatenate([-qi, qi], axis=-1)
    ap = jnp.stack([m1, m2], axis=-1).transpose(1, 2, 0, 3).reshape(g, 2 * n, 2 * levels)
    return bbt, pk, cb, dq, ap


def _s5_in_kernel(x3_ref, g_ref, wt_ref, u3_ref, xbuf, ubuf, sem_in, sem_out):
    s = pl.program_id(0)
    last = pl.num_programs(0) - 1
    slot = s % 2

    def load(step):
        return pltpu.make_async_copy(x3_ref.at[pl.ds(0, x3_ref.shape[0]), step], xbuf.at[step % 2],
                                     sem_in.at[step % 2])

    def store(step):
        return pltpu.make_async_copy(ubuf.at[step % 2], u3_ref.at[pl.ds(0, u3_ref.shape[0]), step],
                                     sem_out.at[step % 2])

    pl.when(s == 0)(lambda: load(s).start())
    pl.when(s < last)(lambda: load(s + 1).start())
    load(s).wait()
    pl.when(s >= 2)(lambda: store(s - 2).wait())
    hn = _rms(xbuf[slot], g_ref[...]).astype(BF16)
    ubuf[slot] = lax.dot_general(wt_ref[...], hn, (((1,), (1,)), ((), ())), preferred_element_type=F32)
    store(s).start()

    @pl.when(s == last)
    def _():
        store(s - 1).wait()
        store(s).wait()


def _tile_lanes(a, times):
    a2 = jnp.concatenate([a, a], axis=1)
    return jnp.concatenate([a2] * (times // 2), axis=1)


def _spread_lanes(a):
    r, c = a.shape
    low = lax.broadcasted_iota(jnp.int32, (r, LANES), 1) < CHUNK
    pairs = [jnp.where(low, jnp.broadcast_to(a[:, 2 * j:2 * j + 1], (r, LANES)),
                       jnp.broadcast_to(a[:, 2 * j + 1:2 * j + 2], (r, LANES))) for j in range(c // 2)]
    return jnp.concatenate(pairs, axis=1)


def _ssm_kernel(u_ref, bbt_ref, pk_ref, cb_ref, dq_ref, ap_ref, o_ref, toet_ref, *, n_chunks):
    n = S5_STATE
    rows = S5_GROUP * CHUNK
    ncol = u_ref.shape[-1]
    u = u_ref[0].reshape(rows, ncol).astype(BF16)

    pk = pk_ref[0]
    cb = cb_ref[0]
    cr, ci, br, bi = [_spread_lanes(cb[:, S5_GROUP * k:S5_GROUP * (k + 1)]) for k in range(4)]
    p0r, p0i, p1r, p1i, rr, ri = [_tile_lanes(pk[:, CHUNK * k:CHUNK * (k + 1)], S5_GROUP) for k in range(6)]
    lk = jnp.concatenate([cr * p0r - ci * p0i, -(cr * p0i + ci * p0r)], axis=0)
    cinjt = jnp.concatenate([cr * p1r - ci * p1i, -(cr * p1i + ci * p1r)], axis=0).astype(BF16)
    bm = jnp.concatenate([rr * br - ri * bi, rr * bi + ri * br], axis=0).astype(BF16)

    kern = jnp.dot(bbt_ref[0], lk, preferred_element_type=F32, precision=HIGHEST)
    lane = lax.broadcasted_iota(jnp.int32, (S5_GROUP, rows), 1)
    own_lag0 = lane == lax.broadcasted_iota(jnp.int32, (S5_GROUP, rows), 0) * CHUNK
    kern = kern + jnp.where(own_lag0, jnp.concatenate([dq_ref[0]] * (rows // LANES), axis=1), 0.0)
    causal = (lax.broadcasted_iota(jnp.int32, (CHUNK, rows), 1) % CHUNK
              >= lax.broadcasted_iota(jnp.int32, (CHUNK, rows), 0))
    for p in range(S5_GROUP):
        base = jnp.broadcast_to(kern[p:p + 1, :], (CHUNK, rows))
        rolled = pltpu.roll(base, 0, 1, stride=1, stride_axis=0)
        toet_ref[p * CHUNK:(p + 1) * CHUNK, :] = jnp.where(causal, rolled, 0.0).astype(BF16)
    tn_dims = (((0,), (0,)), ((), ()))
    y = lax.dot_general(toet_ref[...], u, tn_dims, preferred_element_type=F32)
    x = jnp.dot(bm, u, preferred_element_type=F32)
    ap = ap_ref[0]
    col = lax.broadcasted_iota(jnp.int32, x.shape, 1) % n_chunks
    for k in range(int(math.log2(n_chunks))):
        sh = 1 << k
        xs = jnp.where(col >= sh, pltpu.roll(x, sh, 1), 0.0)
        xsw = jnp.concatenate([xs[n:], xs[:n]], axis=0)
        x = x + ap[:, 2 * k:2 * k + 1] * xs + ap[:, 2 * k + 1:2 * k + 2] * xsw
    h0 = jnp.where(col >= 1, pltpu.roll(x, 1, 1), 0.0)
    h0_hi = h0.astype(BF16)
    h0_lo = (h0 - h0_hi.astype(F32)).astype(BF16)
    y = y + lax.dot_general(cinjt, h0_hi, tn_dims, preferred_element_type=F32)
    y = y + lax.dot_general(cinjt, h0_lo, tn_dims, preferred_element_type=F32)
    o_ref[0] = jax.nn.gelu(y).reshape(S5_GROUP, CHUNK, ncol)


def _s5_out_kernel(v3_ref, x3_ref, w_ref, o3_ref, vbuf, xbuf, obuf, sem_v, sem_x, sem_o):
    s = pl.program_id(0)
    last = pl.num_programs(0) - 1
    slot = s % 2

    def load_v(step):
        return pltpu.make_async_copy(v3_ref.at[pl.ds(0, v3_ref.shape[0]), step], vbuf.at[step % 2],
                                     sem_v.at[step % 2])

    def load_x(step):
        return pltpu.make_async_copy(x3_ref.at[pl.ds(0, x3_ref.shape[0]), step], xbuf.at[step % 2],
                                     sem_x.at[step % 2])

    def store(step):
        return pltpu.make_async_copy(obuf.at[step % 2], o3_ref.at[pl.ds(0, o3_ref.shape[0]), step],
                                     sem_o.at[step % 2])

    def load(step):
        load_v(step).start()
        load_x(step).start()

    pl.when(s == 0)(lambda: load(s))
    pl.when(s < last)(lambda: load(s + 1))
    load_v(s).wait()
    load_x(s).wait()
    pl.when(s >= 2)(lambda: store(s - 2).wait())
    z = lax.dot_general(vbuf[slot].astype(BF16), w_ref[...], (((0,), (0,)), ((), ())),
                        preferred_element_type=F32)
    d = xbuf.shape[-1]
    obuf[slot] = xbuf[slot] + z[:, :d] * jax.nn.sigmoid(z[:, d:])
    store(s).start()

    @pl.when(s == last)
    def _():
        store(s - 1).wait()
        store(s).wait()


def _s5_layer(x, g_norm, w_in, lam_re, lam_im, log_dt, b_re, b_im, c_re, c_im, d_skip, w_glu,
              bsz, seq):
    m, d = x.shape
    n_chunks = seq // CHUNK
    ncol = bsz * n_chunks
    rows = S5_GROUP * CHUNK
    bbt, pk, cb, dq, ap = _s5_operators(lam_re, lam_im, log_dt, b_re, b_im, c_re, c_im, d_skip,
                                        n_chunks)
    x3 = x.reshape(ncol, CHUNK, d)
    hbm = pl.BlockSpec(memory_space=pl.ANY)
    u3 = pl.pallas_call(
        _s5_in_kernel,
        grid=(CHUNK,),
        in_specs=[hbm, pl.BlockSpec((1, d), lambda s: (0, 0)), pl.BlockSpec((d, d), lambda s: (0, 0))],
        out_specs=hbm,
        out_shape=jax.ShapeDtypeStruct((d, CHUNK, ncol), F32),
        scratch_shapes=[
            pltpu.VMEM((2, ncol, d), F32),
            pltpu.VMEM((2, d, ncol), F32),
            pltpu.SemaphoreType.DMA((2,)),
            pltpu.SemaphoreType.DMA((2,)),
        ],
        compiler_params=_params("arbitrary"),
        name="s5_in",
    )(x3, g_norm.reshape(1, d), w_in.T.astype(BF16))
    blk4 = (1, S5_GROUP, CHUNK, ncol)
    per_group = lambda a: pl.BlockSpec((1,) + a.shape[1:], lambda g: (g, 0, 0))
    v4 = pl.pallas_call(
        functools.partial(_ssm_kernel, n_chunks=n_chunks),
        grid=(S5_GROUPS,),
        in_specs=[pl.BlockSpec(blk4, lambda g: (g, 0, 0, 0))]
        + [per_group(a) for a in (bbt, pk, cb, dq, ap)],
        out_specs=pl.BlockSpec(blk4, lambda g: (g, 0, 0, 0)),
        out_shape=jax.ShapeDtypeStruct((S5_GROUPS, S5_GROUP, CHUNK, ncol), F32),
        scratch_shapes=[pltpu.VMEM((rows, rows), BF16)],
        compiler_params=_params("parallel"),
        name="s5_ssm",
    )(u3.reshape(S5_GROUPS, S5_GROUP, CHUNK, ncol), bbt, pk, cb, dq, ap)
    o3 = pl.pallas_call(
        _s5_out_kernel,
        grid=(CHUNK,),
        in_specs=[hbm, hbm, pl.BlockSpec((d, 2 * d), lambda s: (0, 0))],
        out_specs=hbm,
        out_shape=jax.ShapeDtypeStruct((ncol, CHUNK, d), F32),
        scratch_shapes=[
            pltpu.VMEM((2, d, ncol), F32),
            pltpu.VMEM((2, ncol, d), F32),
            pltpu.VMEM((2, ncol, d), F32),
            pltpu.SemaphoreType.DMA((2,)),
            pltpu.SemaphoreType.DMA((2,)),
            pltpu.SemaphoreType.DMA((2,)),
        ],
        compiler_params=_params("arbitrary"),
        name="s5_out",
    )(v4.reshape(d, CHUNK, ncol), x3, w_glu.astype(BF16))
    return o3.reshape(m, d)


ATT_PERIOD = ATT_TQ + ATT_TK


def _attn_bias_seed(rel_bias):
    k = jnp.arange(ATT_PERIOD)
    rel = jnp.where(k < ATT_TK, ATT_LEFT - k, ATT_LEFT + ATT_PERIOD - k)
    return jnp.take(rel_bias.astype(F32), jnp.clip(rel, -MAX_REL, MAX_REL) + MAX_REL, axis=1)


def _attn_kernel(x_ref, q_ref, k0_ref, k1_ref, k2_ref, v0_ref, v1_ref, v2_ref, seed_ref, wo_ref,
                 o_ref, bias_ref, ocat_ref):
    i = pl.program_id(1)

    @pl.when((pl.program_id(0) == 0) & (i == 0))
    def _():
        r = lax.broadcasted_iota(jnp.int32, (ATT_TQ, ATT_TK), 0)
        j = lax.broadcasted_iota(jnp.int32, (ATT_TQ, ATT_TK), 1)
        dchunk = (r + ATT_LEFT) // CHUNK - j // CHUNK
        valid = (dchunk >= 0) & (dchunk <= LEFT_CHUNKS)
        for h in range(ATT_HEADS):
            seed = jnp.broadcast_to(seed_ref[h:h + 1, :], (ATT_TQ, ATT_PERIOD))
            rolled = pltpu.roll(seed, 0, 1, stride=1, stride_axis=0)
            bias_ref[h] = jnp.where(valid, rolled[:, :ATT_TK] * LOG2E, NEG_INF)

    lane = lax.broadcasted_iota(jnp.int32, (1, LANES), 1)
    col = lax.broadcasted_iota(jnp.int32, (ATT_TQ, ATT_TK), 1)
    pad = col < ATT_LEFT - i * ATT_TQ
    scale = ATT_HEAD_DIM ** -0.5
    heads_per = LANES // ATT_HEAD_DIM
    for hp in range(D_MODEL // LANES):
        sl = slice(hp * LANES, (hp + 1) * LANES)
        q = q_ref[:, sl] * scale
        k = jnp.concatenate([k0_ref[:, sl], k1_ref[:, sl], k2_ref[:, sl]], axis=0)
        v = jnp.concatenate([v0_ref[:, sl], v1_ref[:, sl], v2_ref[:, sl]], axis=0)
        acc = jnp.zeros((ATT_TQ, LANES), F32)
        for h in range(heads_per):
            in_head = (lane >= h * ATT_HEAD_DIM) & (lane < (h + 1) * ATT_HEAD_DIM)
            qh = jnp.where(in_head, q, jnp.zeros_like(q))
            s = lax.dot_general(qh, k, (((1,), (1,)), ((), ())), preferred_element_type=F32)
            s = jnp.where(pad, NEG_INF, s + bias_ref[hp * heads_per + h])
            p = jnp.exp2(s - jnp.max(s, axis=-1, keepdims=True))
            vh = jnp.where(in_head, v, jnp.ones_like(v))
            pv = jnp.dot(p.astype(BF16), vh, preferred_element_type=F32)
            acc = jnp.where(in_head, pv / pltpu.roll(pv, ATT_HEAD_DIM, 1), acc)
        ocat_ref[:, sl] = acc.astype(BF16)
    o_ref[...] = x_ref[...] + jnp.dot(ocat_ref[...], wo_ref[...], preferred_element_type=F32)


def _attn_layer(x, g_norm, w_qkv, w_o, rel_bias, bsz, seq):
    m, d = x.shape
    w_qkv = jnp.concatenate([w_qkv[:, :d] * LOG2E, w_qkv[:, d:]], axis=1)
    qkv = _norm_linear(x, g_norm, w_qkv.astype(BF16), BF16)
    nq = seq // ATT_TQ

    def rows(col, back):
        return lambda b, i: (b * nq + jnp.maximum(i - back, 0), col)

    blk = (ATT_TQ, d)
    nkb = ATT_TK // ATT_TQ
    return pl.pallas_call(
        _attn_kernel,
        grid=(bsz, nq),
        in_specs=(
            [pl.BlockSpec(blk, rows(0, 0)), pl.BlockSpec(blk, rows(0, 0))]
            + [pl.BlockSpec(blk, rows(1, nkb - 1 - kb)) for kb in range(nkb)]
            + [pl.BlockSpec(blk, rows(2, nkb - 1 - kb)) for kb in range(nkb)]
            + [pl.BlockSpec((ATT_HEADS, ATT_PERIOD), lambda b, i: (0, 0)),
               pl.BlockSpec((d, d), lambda b, i: (0, 0))]
        ),
        out_specs=pl.BlockSpec(blk, rows(0, 0)),
        out_shape=jax.ShapeDtypeStruct((m, d), F32),
        scratch_shapes=[pltpu.VMEM((ATT_HEADS, ATT_TQ, ATT_TK), F32), pltpu.VMEM((ATT_TQ, d), BF16)],
        compiler_params=_params("arbitrary", "arbitrary"),
        name="chunk_attn",
    )(x, *([qkv] * (1 + 2 * nkb)), _attn_bias_seed(rel_bias), w_o.astype(BF16))


def _lru_kernel(x_ref, g_ref, win_ref, cw_ref, cb_ref, wax_ref, ba_ref, bx_ref, lam_ref, wout_ref,
                o_ref, prev_ref, carry_ref, a_s, b_s):
    t, w = x_ref.shape

    @pl.when(pl.program_id(1) == 0)
    def _():
        prev_ref[...] = jnp.zeros_like(prev_ref)
        carry_ref[...] = jnp.zeros_like(carry_ref)

    x = x_ref[...]
    z = jnp.dot(_rms(x, g_ref[...]).astype(BF16), win_ref[...], preferred_element_type=F32)
    gate = z[:, :w]
    xr = z[:, w:]
    xext = jnp.concatenate([prev_ref[...], xr], axis=0)
    cw = cw_ref[...]
    xc = cb_ref[...] + cw[0:1] * xext[SUBLANES - 3:SUBLANES - 3 + t]
    for k in range(1, CONV_W):
        xc = xc + cw[k:k + 1] * xext[SUBLANES - 3 + k:SUBLANES - 3 + k + t]
    prev_ref[...] = xr[t - SUBLANES:]

    xcb = xc.astype(BF16)
    pre_a, pre_x = [], []
    for blk in range(LRU_BLOCKS):
        pre = jnp.dot(xcb[:, blk * LRU_BLOCK_W:(blk + 1) * LRU_BLOCK_W], wax_ref[blk],
                      preferred_element_type=F32)
        pre_a.append(pre[:, :LRU_BLOCK_W])
        pre_x.append(pre[:, LRU_BLOCK_W:])
    r = jax.nn.sigmoid(jnp.concatenate(pre_a, axis=1) + ba_ref[...])
    ig = jax.nn.sigmoid(jnp.concatenate(pre_x, axis=1) + bx_ref[...])
    z = -lam_ref[...]
    softplus = jnp.maximum(z, 0.0) + jnp.log1p(jnp.exp(-jnp.abs(z)))
    log_a = -LRU_C * r * softplus
    a = jnp.exp(log_a)
    b = jnp.sqrt(1.0 - a * a) * (ig * xc)

    a3 = a.reshape(t // SUBLANES, SUBLANES, w)
    b3 = b.reshape(t // SUBLANES, SUBLANES, w)
    sub = lax.broadcasted_iota(jnp.int32, a3.shape, 1)
    for sh in (1, 2, 4):
        a_sh = jnp.where(sub >= sh, pltpu.roll(a3, sh, 1), 1.0)
        b_sh = jnp.where(sub >= sh, pltpu.roll(b3, sh, 1), 0.0)
        b3 = a3 * b_sh + b3
        a3 = a3 * a_sh
    a_s[...] = a3
    b_s[...] = b3

    def body(j, carry):
        hb = b_s[j] + a_s[j] * carry
        b_s[j] = hb
        return hb[SUBLANES - 1:SUBLANES, :]

    carry = lax.fori_loop(0, t // SUBLANES, body, carry_ref[0:1, :])
    carry_ref[0:1, :] = carry
    h = b_s[...].reshape(t, w)
    y = (jax.nn.gelu(gate) * h).astype(BF16)
    o_ref[...] = x + jnp.dot(y, wout_ref[...], preferred_element_type=F32)


def _lru_layer(x, g_norm, w_in, conv_w, conv_b, w_a, b_a, w_x, b_x, lam, w_out, bsz, seq):
    m, d = x.shape
    wax = jnp.concatenate([w_a, w_x], axis=-1).astype(BF16)
    nt = seq // LRU_T
    row = lambda v: v.reshape(1, d).astype(F32)
    vec = pl.BlockSpec((1, d), lambda b, i: (0, 0))
    return pl.pallas_call(
        _lru_kernel,
        grid=(bsz, nt),
        in_specs=[
            pl.BlockSpec((LRU_T, d), lambda b, i: (b * nt + i, 0)),
            vec,
            pl.BlockSpec((d, 2 * d), lambda b, i: (0, 0)),
            pl.BlockSpec((CONV_W, d), lambda b, i: (0, 0)),
            vec,
            pl.BlockSpec(wax.shape, lambda b, i: (0, 0, 0)),
            vec, vec, vec,
            pl.BlockSpec((d, d), lambda b, i: (0, 0)),
        ],
        out_specs=pl.BlockSpec((LRU_T, d), lambda b, i: (b * nt + i, 0)),
        out_shape=jax.ShapeDtypeStruct((m, d), F32),
        scratch_shapes=[
            pltpu.VMEM((SUBLANES, d), F32),
            pltpu.VMEM((SUBLANES, d), F32),
            pltpu.VMEM((LRU_T // SUBLANES, SUBLANES, d), F32),
            pltpu.VMEM((LRU_T // SUBLANES, SUBLANES, d), F32),
        ],
        compiler_params=_params("parallel", "arbitrary"),
        name="rglru",
    )(x, row(g_norm), w_in.astype(BF16), conv_w.astype(F32), row(conv_b), wax, row(b_a), row(b_x),
      row(lam), w_out.astype(BF16))


def _memattn(x, g_ref, wq_ref, k_ref, v_ref, wo_ref, ocat_ref):
    hn = _rms(x, g_ref[...]).astype(BF16)
    q = jnp.dot(hn, wq_ref[...], preferred_element_type=F32) * (MEM_HEAD_DIM ** -0.5)
    q = q.astype(BF16)
    for h in range(MEM_HEADS):
        sl = slice(h * MEM_HEAD_DIM, (h + 1) * MEM_HEAD_DIM)
        s = lax.dot_general(q[:, sl], k_ref[:, sl], (((1,), (1,)), ((), ())),
                            preferred_element_type=F32)
        mx = jnp.max(s, axis=-1, keepdims=True)
        p = jnp.exp(s - mx)
        l = jnp.sum(p, axis=-1, keepdims=True)
        o = jnp.dot(p.astype(BF16), v_ref[:, sl], preferred_element_type=F32) / l
        ocat_ref[:, sl] = o.astype(BF16)
    return x + jnp.dot(ocat_ref[...], wo_ref[...], preferred_element_type=F32)


def _mem_ffn_kernel(x_ref, g_ref, wq_ref, k_ref, v_ref, wo_ref, g2_ref, wgu_ref, wd_ref, o_ref, ocat_ref):
    ff = wd_ref.shape[0]
    x1 = _memattn(x_ref[...], g_ref, wq_ref, k_ref, v_ref, wo_ref, ocat_ref)
    hn = _rms(x1, g2_ref[...]).astype(BF16)
    o_ref[...] = x1
    for f in range(ff // FFN_TF):
        lo = f * FFN_TF
        gate = jnp.dot(hn, wgu_ref[:, lo:lo + FFN_TF], preferred_element_type=F32)
        up = jnp.dot(hn, wgu_ref[:, ff + lo:ff + lo + FFN_TF], preferred_element_type=F32)
        act = gate * jax.nn.sigmoid(gate) * up
        o_ref[...] += jnp.dot(act.astype(BF16), wd_ref[lo:lo + FFN_TF, :], preferred_element_type=F32)


def _mem_route_kernel(x_ref, g_ref, wq_ref, k_ref, v_ref, wo_ref, g2_ref, wrt_ref,
                      o_ref, hn_ref, ct_ref, st_ref, cnt_ref, ocat_ref):
    x1 = _memattn(x_ref[...], g_ref, wq_ref, k_ref, v_ref, wo_ref, ocat_ref)
    o_ref[...] = x1
    hn = _rms(x1, g2_ref[...])
    hn_ref[...] = hn.astype(BF16)
    lg = lax.dot_general(wrt_ref[...], hn, (((1,), (1,)), ((), ())),
                         preferred_element_type=F32, precision=HIGHEST)
    idx = lax.broadcasted_iota(jnp.int32, lg.shape, 0)
    m1 = jnp.max(lg, axis=0, keepdims=True)
    i1 = jnp.min(jnp.where(lg == m1, idx, N_EXPERTS), axis=0, keepdims=True)
    oh1 = idx == i1
    lg2 = jnp.where(oh1, -jnp.inf, lg)
    m2 = jnp.max(lg2, axis=0, keepdims=True)
    i2 = jnp.min(jnp.where(lg2 == m2, idx, N_EXPERTS), axis=0, keepdims=True)
    oh2 = idx == i2
    e2 = jnp.exp(m2 - m1)
    g1 = 1.0 / (1.0 + e2)
    g2 = e2 / (1.0 + e2)
    ct_ref[...] = jnp.where(oh1, g1, 0.0) + jnp.where(oh2, g2, 0.0)
    sel = (oh1 | oh2).astype(F32)
    st_ref[...] = sel
    cnt_ref[0] = jnp.broadcast_to(jnp.sum(sel, axis=1, keepdims=True), (N_EXPERTS, LANES))


def _mem_specs(d, nt, tm, layer):
    wspec = pl.BlockSpec((None, d, d), lambda b, i: (layer, 0, 0))
    return [
        pl.BlockSpec((tm, d), lambda b, i: (b * nt + i, 0)),
        pl.BlockSpec((1, d), lambda b, i: (0, 0)),
        wspec,
        pl.BlockSpec((N_MEM, d), lambda b, i: (b, 2 * layer)),
        pl.BlockSpec((N_MEM, d), lambda b, i: (b, 2 * layer + 1)),
        wspec,
    ]


def _mem_ffn_layer(x, g_mem, w_q, kv_all, layer, w_o, g_ffn, w_gu, w_down, ffn_layer, bsz, seq, tm=1024):
    m, d = x.shape
    nt = seq // tm
    ff = w_down.shape[1]
    resident = lambda shape: pl.BlockSpec((None,) + shape, lambda b, i: (ffn_layer, 0, 0),
                                          pipeline_mode=pl.Buffered(1))
    return pl.pallas_call(
        _mem_ffn_kernel,
        grid=(bsz, nt),
        in_specs=_mem_specs(d, nt, tm, layer) + [
            pl.BlockSpec((1, d), lambda b, i: (0, 0)),
            resident((d, 2 * ff)),
            resident((ff, d)),
        ],
        out_specs=pl.BlockSpec((tm, d), lambda b, i: (b * nt + i, 0)),
        out_shape=jax.ShapeDtypeStruct((m, d), F32),
        scratch_shapes=[pltpu.VMEM((tm, d), BF16)],
        compiler_params=_params("parallel", "parallel"),
        name="mem_ffn",
    )(x, g_mem.reshape(1, d), w_q, kv_all, kv_all, w_o, g_ffn.reshape(1, d), w_gu, w_down)


MOE_TM = 512
MOE_TR = 512
MOE_HALF = MOE_TM // 2
MOE_ALIGN = 16
MOE_TF = 512
XS_W = D_MODEL + LANES


def _tile_rank(st):
    tm = st.shape[1]
    before = (lax.broadcasted_iota(jnp.int32, (tm, tm), 0)
              < lax.broadcasted_iota(jnp.int32, (tm, tm), 1))
    return jnp.dot(st.astype(BF16), before.astype(BF16), preferred_element_type=F32)


def _one_hot_rows(rank, st, e, half):
    tm = rank.shape[1]
    rows = lax.broadcasted_iota(jnp.int32, (MOE_HALF, tm), 0).astype(F32) + float(half * MOE_HALF)
    hit = (rank[e:e + 1] == rows) & (st[e:e + 1] > 0.0)
    return jnp.where(hit, 1.0, 0.0).astype(BF16)


def _one_hot_all(rank, st):
    return jnp.concatenate([_one_hot_rows(rank, st, e, 0) for e in range(N_EXPERTS)], axis=0)


def _dispatch_kernel(base_ref, n_ref, hn_ref, ct_ref, st_ref, xs_init_ref, xs_ref, stage, stage2,
                     sem, sem2):
    del xs_init_ref
    i = pl.program_id(0)
    last = pl.num_programs(0) - 1
    slot = i % 2
    hn = hn_ref[...]
    ct = ct_ref[...]
    st = st_ref[...]
    rank = _tile_rank(st)
    g_hi = ct.astype(BF16)
    g_lo = (ct - g_hi.astype(F32)).astype(BF16)
    gm = jnp.concatenate(
        [g_hi, g_lo, jnp.zeros((LANES - 2 * N_EXPERTS, ct.shape[1]), BF16)], axis=0)
    nt_dims = (((1,), (1,)), ((), ()))

    def copy(step, e):
        row = pl.multiple_of(base_ref[step * N_EXPERTS + e], MOE_ALIGN)
        return pltpu.make_async_copy(stage.at[step % 2, pl.ds(e * MOE_HALF, MOE_HALF)],
                                     xs_ref.at[pl.ds(row, MOE_HALF)], sem.at[step % 2, e])

    def copy2(e):
        row = pl.multiple_of(base_ref[i * N_EXPERTS + e] + MOE_HALF, MOE_ALIGN)
        return pltpu.make_async_copy(stage2.at[e], xs_ref.at[pl.ds(row, MOE_HALF)], sem2.at[e])

    def drain(step):
        for e in range(N_EXPERTS):
            copy(step, e).wait()

    p = _one_hot_all(rank, st)
    stage[slot, :, :D_MODEL] = jnp.dot(p, hn, preferred_element_type=F32).astype(BF16)
    stage[slot, :, D_MODEL:] = lax.dot_general(p, gm, nt_dims, preferred_element_type=F32).astype(BF16)
    pl.when(i >= 1)(lambda: drain(i - 1))
    for e in range(N_EXPERTS):
        copy(i, e).start()

    def fill2(e):
        p2 = _one_hot_rows(rank, st, e, 1)
        stage2[e, :, :D_MODEL] = jnp.dot(p2, hn, preferred_element_type=F32).astype(BF16)
        stage2[e, :, D_MODEL:] = lax.dot_general(p2, gm, nt_dims, preferred_element_type=F32).astype(BF16)
        copy2(e).start()

    for e in range(N_EXPERTS):
        pl.when(n_ref[i * N_EXPERTS + e] > MOE_HALF)(functools.partial(fill2, e))
    for e in range(N_EXPERTS):
        pl.when(n_ref[i * N_EXPERTS + e] > MOE_HALF)(lambda e=e: copy2(e).wait())

    pl.when(i == last)(lambda: drain(i))


def _expert_kernel(be_ref, nu_ref, x_ref, wg_ref, wu_ref, wd_ref, o_ref, acc_ref):
    b = pl.program_id(0)

    @pl.when(b < nu_ref[0])
    def _():
        e = be_ref[b]
        xe = x_ref[...]
        h = xe[:, :D_MODEL]
        ext = xe[:, D_MODEL:].astype(F32)
        lane = lax.broadcasted_iota(jnp.int32, ext.shape, 1)
        g = jnp.sum(jnp.where((lane == e) | (lane == e + N_EXPERTS), ext, 0.0), axis=1, keepdims=True)
        ff = wd_ref.shape[1]
        for c in range(ff // MOE_TF):
            lo = c * MOE_TF
            gate = jnp.dot(h, wg_ref[0, :, lo:lo + MOE_TF], preferred_element_type=F32)
            up = jnp.dot(h, wu_ref[0, :, lo:lo + MOE_TF], preferred_element_type=F32)
            act = (gate * jax.nn.sigmoid(gate) * up * g).astype(BF16)
            part = jnp.dot(act, wd_ref[0, lo:lo + MOE_TF, :], preferred_element_type=F32)
            if c == 0:
                acc_ref[...] = part
            else:
                acc_ref[...] += part
        o_ref[...] = acc_ref[...].astype(o_ref.dtype)

    @pl.when(b >= nu_ref[0])
    def _():
        o_ref[...] = jnp.zeros_like(o_ref)


def _combine_kernel(base_ref, n_ref, x_ref, st_ref, ys_ref, gout_ref, o_ref, ybuf, ybuf2, sem, sem2,
                    *, out_norm):
    i = pl.program_id(0)
    slot = i % 2

    def copy(step, e):
        row = pl.multiple_of(base_ref[step * N_EXPERTS + e], MOE_ALIGN)
        return pltpu.make_async_copy(ys_ref.at[pl.ds(row, MOE_HALF)],
                                     ybuf.at[step % 2, pl.ds(e * MOE_HALF, MOE_HALF)],
                                     sem.at[step % 2, e])

    def copy2(e):
        row = pl.multiple_of(base_ref[i * N_EXPERTS + e] + MOE_HALF, MOE_ALIGN)
        return pltpu.make_async_copy(ys_ref.at[pl.ds(row, MOE_HALF)], ybuf2.at[e], sem2.at[e])

    def fetch(step):
        for e in range(N_EXPERTS):
            copy(step, e).start()

    pl.when(i == 0)(lambda: fetch(i))
    pl.when(i + 1 < pl.num_programs(0))(lambda: fetch(i + 1))
    for e in range(N_EXPERTS):
        pl.when(n_ref[i * N_EXPERTS + e] > MOE_HALF)(lambda e=e: copy2(e).start())

    st = st_ref[...]
    rank = _tile_rank(st)
    p = _one_hot_all(rank, st)
    for e in range(N_EXPERTS):
        copy(i, e).wait()
    tn_dims = (((0,), (0,)), ((), ()))
    o_ref[...] = x_ref[...] + lax.dot_general(p, ybuf[slot], tn_dims, preferred_element_type=F32)

    def gather2(e):
        copy2(e).wait()
        p2 = _one_hot_rows(rank, st, e, 1)
        o_ref[...] += lax.dot_general(p2, ybuf2[e], tn_dims, preferred_element_type=F32)

    for e in range(N_EXPERTS):
        pl.when(n_ref[i * N_EXPERTS + e] > MOE_HALF)(functools.partial(gather2, e))
    if out_norm:
        o_ref[...] = _rms(o_ref[...], gout_ref[...])


def _moe_layer(x, g_mem, w_q, kv_all, layer, w_o, g_ffn, w_router, w_gu, w_down, moe_layer, bsz, seq,
               g_out=None):
    m, d = x.shape
    ff = w_down.shape[2]
    nt = m // MOE_TM
    ntb = seq // MOE_TM
    i32 = jnp.int32
    x, hn, ct, st, cnt = pl.pallas_call(
        _mem_route_kernel,
        grid=(bsz, ntb),
        in_specs=_mem_specs(d, ntb, MOE_TM, layer) + [
            pl.BlockSpec((1, d), lambda b, i: (0, 0)),
            pl.BlockSpec((N_EXPERTS, d), lambda b, i: (0, 0)),
        ],
        out_specs=[
            pl.BlockSpec((MOE_TM, d), lambda b, i: (b * ntb + i, 0)),
            pl.BlockSpec((MOE_TM, d), lambda b, i: (b * ntb + i, 0)),
            pl.BlockSpec((N_EXPERTS, MOE_TM), lambda b, i: (0, b * ntb + i)),
            pl.BlockSpec((N_EXPERTS, MOE_TM), lambda b, i: (0, b * ntb + i)),
            pl.BlockSpec((1, N_EXPERTS, LANES), lambda b, i: (b * ntb + i, 0, 0)),
        ],
        out_shape=[
            jax.ShapeDtypeStruct((m, d), F32),
            jax.ShapeDtypeStruct((m, d), BF16),
            jax.ShapeDtypeStruct((N_EXPERTS, m), F32),
            jax.ShapeDtypeStruct((N_EXPERTS, m), F32),
            jax.ShapeDtypeStruct((nt, N_EXPERTS, LANES), F32),
        ],
        scratch_shapes=[pltpu.VMEM((MOE_TM, d), BF16)],
        compiler_params=_params("parallel", "parallel"),
        name="mem_route",
    )(x, g_mem.reshape(1, d), w_q, kv_all, kv_all, w_o, g_ffn.reshape(1, d), w_router.T.astype(F32))

    n = cnt[:, :, 0].astype(i32)
    chunk = (n + MOE_ALIGN - 1) // MOE_ALIGN * MOE_ALIGN
    seg = (jnp.sum(chunk, axis=0) + MOE_HALF + MOE_TR - 1) // MOE_TR * MOE_TR
    seg_off = jnp.cumsum(seg) - seg
    base = (seg_off[None, :] + jnp.cumsum(chunk, axis=0) - chunk).reshape(-1).astype(i32)
    n_flat = n.reshape(-1)
    bound = 2 * m + nt * N_EXPERTS * (MOE_ALIGN - 1) + N_EXPERTS * (MOE_HALF + MOE_TR - 1)
    nb = -(-bound // MOE_TR)
    blk_end = jnp.cumsum(seg // MOE_TR)
    blk_expert = jnp.minimum(jnp.sum(jnp.arange(nb, dtype=i32)[:, None] >= blk_end[None, :], axis=1),
                             N_EXPERTS - 1).astype(i32)
    n_used = blk_end[-1:].astype(i32)
    rows = nb * MOE_TR

    tile_spec = lambda shape, imap: pl.BlockSpec(shape, imap)
    xs = pl.pallas_call(
        _dispatch_kernel,
        grid_spec=pltpu.PrefetchScalarGridSpec(
            num_scalar_prefetch=2,
            grid=(nt,),
            in_specs=[
                tile_spec((MOE_TM, d), lambda i, b, c: (i, 0)),
                tile_spec((N_EXPERTS, MOE_TM), lambda i, b, c: (0, i)),
                tile_spec((N_EXPERTS, MOE_TM), lambda i, b, c: (0, i)),
                pl.BlockSpec(memory_space=pl.ANY),
            ],
            out_specs=pl.BlockSpec(memory_space=pl.ANY),
            scratch_shapes=[
                pltpu.VMEM((2, N_EXPERTS * MOE_HALF, XS_W), BF16),
                pltpu.VMEM((N_EXPERTS, MOE_HALF, XS_W), BF16),
                pltpu.SemaphoreType.DMA((2, N_EXPERTS)),
                pltpu.SemaphoreType.DMA((N_EXPERTS,)),
            ],
        ),
        out_shape=jax.ShapeDtypeStruct((rows, XS_W), BF16),
        input_output_aliases={5: 0},
        compiler_params=_params("arbitrary"),
        name="moe_dispatch",
    )(base, n_flat, hn, ct, st, jnp.zeros((rows, XS_W), BF16))

    ys = pl.pallas_call(
        _expert_kernel,
        grid_spec=pltpu.PrefetchScalarGridSpec(
            num_scalar_prefetch=2,
            grid=(nb,),
            in_specs=[
                pl.BlockSpec((MOE_TR, XS_W), lambda b, be, nu: (b, 0)),
                pl.BlockSpec((None, 1, d, ff), lambda b, be, nu: (moe_layer, be[b], 0, 0)),
                pl.BlockSpec((None, 1, d, ff), lambda b, be, nu: (moe_layer, be[b], 0, 1)),
                pl.BlockSpec((None, 1, ff, d), lambda b, be, nu: (moe_layer, be[b], 0, 0)),
            ],
            out_specs=pl.BlockSpec((MOE_TR, d), lambda b, be, nu: (b, 0)),
            scratch_shapes=[pltpu.VMEM((MOE_TR, d), F32)],
        ),
        out_shape=jax.ShapeDtypeStruct((rows, d), BF16),
        compiler_params=_params("arbitrary"),
        name="moe_experts",
    )(blk_expert, n_used, xs, w_gu, w_gu, w_down)

    out_norm = g_out is not None
    g_out = jnp.ones((d,), F32) if g_out is None else g_out
    return pl.pallas_call(
        functools.partial(_combine_kernel, out_norm=out_norm),
        grid_spec=pltpu.PrefetchScalarGridSpec(
            num_scalar_prefetch=2,
            grid=(nt,),
            in_specs=[
                tile_spec((MOE_TM, d), lambda i, b, c: (i, 0)),
                tile_spec((N_EXPERTS, MOE_TM), lambda i, b, c: (0, i)),
                pl.BlockSpec(memory_space=pl.ANY),
                tile_spec((1, d), lambda i, b, c: (0, 0)),
            ],
            out_specs=tile_spec((MOE_TM, d), lambda i, b, c: (i, 0)),
            scratch_shapes=[
                pltpu.VMEM((2, N_EXPERTS * MOE_HALF, d), BF16),
                pltpu.VMEM((N_EXPERTS, MOE_HALF, d), BF16),
                pltpu.SemaphoreType.DMA((2, N_EXPERTS)),
                pltpu.SemaphoreType.DMA((N_EXPERTS,)),
            ],
        ),
        out_shape=jax.ShapeDtypeStruct((m, d), F32),
        compiler_params=_params("arbitrary"),
        name="moe_combine",
    )(base, n_flat, x, st, ys, g_out.reshape(1, d).astype(F32))


def kernel(x, mem, norm_mix, norm_mem, norm_ffn, mem_norm, final_norm, s5_w_in, s5_lam_re, s5_lam_im, s5_log_dt, s5_b_re, s5_b_im, s5_c_re, s5_c_im, s5_d, s5_w_glu, att_w_qkv, att_w_o, att_rel_bias, lru_w_in, lru_conv_w, lru_conv_b, lru_w_a, lru_b_a, lru_w_x, lru_b_x, lru_lam, lru_w_out, mem_w_q, mem_w_kv, mem_w_o, ffn_w_gu, ffn_w_down, moe_w_router, moe_w_gu, moe_w_down):
    bsz, seq, d = x.shape
    x = x.reshape(bsz * seq, d).astype(F32)

    w_kv_all = jnp.concatenate([mem_w_kv[i] for i in range(DEPTH)], axis=1).astype(BF16)
    kv_all = _norm_linear(mem.reshape(bsz * N_MEM, d).astype(F32), mem_norm, w_kv_all, BF16,
                          tm=bsz * N_MEM, tn=2 * d)

    mem_wq, mem_wo = mem_w_q.astype(BF16), mem_w_o.astype(BF16)
    ffn_wgu, ffn_wd = ffn_w_gu.astype(BF16), ffn_w_down.astype(BF16)
    moe_wgu, moe_wd = moe_w_gu.astype(BF16), moe_w_down.astype(BF16)
    for i in range(DEPTH):
        kind, j = i % 3, i // 3
        if kind == 0:
            x = _s5_layer(x, norm_mix[i], s5_w_in[j], s5_lam_re[j], s5_lam_im[j], s5_log_dt[j],
                          s5_b_re[j], s5_b_im[j], s5_c_re[j], s5_c_im[j], s5_d[j], s5_w_glu[j],
                          bsz, seq)
        elif kind == 1:
            x = _attn_layer(x, norm_mix[i], att_w_qkv[j], att_w_o[j], att_rel_bias[j], bsz, seq)
        else:
            x = _lru_layer(x, norm_mix[i], lru_w_in[j], lru_conv_w[j], lru_conv_b[j], lru_w_a[j],
                           lru_b_a[j], lru_w_x[j], lru_b_x[j], lru_lam[j], lru_w_out[j], bsz, seq)
        if i % 2 == 0:
            x = _mem_ffn_layer(x, norm_mem[i], mem_wq, kv_all, i, mem_wo, norm_ffn[i], ffn_wgu, ffn_wd,
                               i // 2, bsz, seq)
        else:
            x = _moe_layer(x, norm_mem[i], mem_wq, kv_all, i, mem_wo, norm_ffn[i], moe_w_router[i // 2],
                           moe_wgu, moe_wd, i // 2, bsz, seq,
                           g_out=final_norm if i == DEPTH - 1 else None)
    return x.reshape(bsz, seq, d)
```

```python
import functools
import math

import jax
import jax.numpy as jnp
from jax import lax
from jax.experimental import pallas as pl
from jax.experimental.pallas import tpu as pltpu

F32 = jnp.float32
BF16 = jnp.bfloat16
HIGHEST = lax.Precision.HIGHEST

D_MODEL = 1024
DEPTH = 4
CHUNK = 64
N_MEM = 256
EPS = 1e-6

S5_GROUP = 16
S5_GROUPS = D_MODEL // S5_GROUP
S5_STATE = 64

ATT_HEADS = 16
ATT_HEAD_DIM = D_MODEL // ATT_HEADS
LEFT_CHUNKS = 8
MAX_REL = 128
ATT_TQ = 4 * CHUNK
ATT_LEFT = LEFT_CHUNKS * CHUNK
ATT_TK = ATT_LEFT + ATT_TQ
NEG_INF = -1e30
LOG2E = math.log2(math.e)

LRU_BLOCKS = 8
LRU_BLOCK_W = D_MODEL // LRU_BLOCKS
CONV_W = 4
LRU_C = 8.0
LRU_T = 512
SUBLANES = 8
LANES = 128

MEM_HEADS = 4
MEM_HEAD_DIM = D_MODEL // MEM_HEADS

N_EXPERTS = 8
FFN_TF = 256

VMEM_LIMIT = 56 * 1024 * 1024


def _params(*sem):
    return pltpu.CompilerParams(dimension_semantics=sem, vmem_limit_bytes=VMEM_LIMIT)


def _rms(x, g):
    ms = jnp.mean(x * x, axis=-1, keepdims=True)
    return x * lax.rsqrt(ms + EPS) * g


def _norm_linear_kernel(x_ref, g_ref, w_ref, o_ref, hn_ref):
    @pl.when(pl.program_id(1) == 0)
    def _():
        hn_ref[...] = _rms(x_ref[...], g_ref[...]).astype(BF16)

    o_ref[...] = jnp.dot(hn_ref[...], w_ref[...], preferred_element_type=F32).astype(o_ref.dtype)


def _norm_linear(x, g, w, out_dtype, tm=512, tn=None):
    m, d = x.shape
    n = w.shape[1]
    tn = n if tn is None else tn
    return pl.pallas_call(
        _norm_linear_kernel,
        grid=(m // tm, n // tn),
        in_specs=[
            pl.BlockSpec((tm, d), lambda i, j: (i, 0)),
            pl.BlockSpec((1, d), lambda i, j: (0, 0)),
            pl.BlockSpec((d, tn), lambda i, j: (0, j)),
        ],
        out_specs=pl.BlockSpec((tm, tn), lambda i, j: (i, j)),
        out_shape=jax.ShapeDtypeStruct((m, n), out_dtype),
        scratch_shapes=[pltpu.VMEM((tm, d), BF16)],
        compiler_params=_params("parallel", "arbitrary"),
        name="norm_linear",
    )(x, g.reshape(1, d), w)


def _s5_operators(lam_re, lam_im, log_dt, b_re, b_im, c_re, c_im, d_skip, n_chunks):
    g, n, p = S5_GROUPS, S5_STATE, S5_GROUP
    lr = lam_re.astype(F32)
    li = lam_im.astype(F32)
    dt = jnp.exp(log_dt.astype(F32))[:, None]
    mag = jnp.exp(lr * dt)
    ar = mag * jnp.cos(li * dt)
    ai = mag * jnp.sin(li * dt)
    den = lr * lr + li * li
    fr = ((ar - 1.0) * lr + ai * li) / den
    fi = (ai * lr - (ar - 1.0) * li) / den
    bbr = fr[..., None] * b_re - fi[..., None] * b_im
    bbi = fr[..., None] * b_im + fi[..., None] * b_re
    bbt = jnp.concatenate([bbr, bbi], axis=1).transpose(0, 2, 1)

    kk = jnp.arange(CHUNK + 1, dtype=F32)
    mag_k = jnp.exp((lr * dt)[..., None] * kk)
    pr = mag_k * jnp.cos((li * dt)[..., None] * kk)
    pi = mag_k * jnp.sin((li * dt)[..., None] * kk)
    pk = jnp.concatenate([pr[..., :CHUNK], pi[..., :CHUNK], pr[..., 1:], pi[..., 1:],
                          pr[..., CHUNK - 1::-1], pi[..., CHUNK - 1::-1]], axis=-1)
    cb = jnp.concatenate([c_re.transpose(0, 2, 1), c_im.transpose(0, 2, 1), bbr, bbi], axis=-1)
    dq = jnp.broadcast_to(d_skip.reshape(g, p, 1), (g, p, LANES))
    qr, qi = [pr[..., CHUNK]], [pi[..., CHUNK]]
    for _ in range(int(math.log2(n_chunks)) - 1):
        qr, qi = qr + [qr[-1] * qr[-1] - qi[-1] * qi[-1]], qi + [2.0 * qr[-1] * qi[-1]]
    levels = len(qr)
    qr, qi = jnp.stack(qr), jnp.stack(qi)
    m1 = jnp.concatenate([qr, qr], axis=-1)
    m2 = jnp.concatenate([-qi, qi], axis=-1)
    ap = jnp.stack([m1, m2], axis=-1).transpose(1, 2, 0, 3).reshape(g, 2 * n, 2 * levels)
    return bbt, pk, cb, dq, ap


def _s5_in_kernel(x3_ref, g_ref, wt_ref, u3_ref, xbuf, ubuf, sem_in, sem_out):
    s = pl.program_id(0)
    last = pl.num_programs(0) - 1
    slot = s % 2

    def load(step):
        return pltpu.make_async_copy(x3_ref.at[pl.ds(0, x3_ref.shape[0]), step], xbuf.at[step % 2],
                                     sem_in.at[step % 2])

    def store(step):
        return pltpu.make_async_copy(ubuf.at[step % 2], u3_ref.at[pl.ds(0, u3_ref.shape[0]), step],
                                     sem_out.at[step % 2])

    pl.when(s == 0)(lambda: load(s).start())
    pl.when(s < last)(lambda: load(s + 1).start())
    load(s).wait()
    pl.when(s >= 2)(lambda: store(s - 2).wait())
    hn = _rms(xbuf[slot], g_ref[...]).astype(BF16)
    ubuf[slot] = lax.dot_general(wt_ref[...], hn, (((1,), (1,)), ((), ())), preferred_element_type=F32)
    store(s).start()

    @pl.when(s == last)
    def _():
        store(s - 1).wait()
        store(s).wait()


def _tile_lanes(a, times):
    a2 = jnp.concatenate([a, a], axis=1)
    return jnp.concatenate([a2] * (times // 2), axis=1)


def _spread_lanes(a):
    r, c = a.shape
    low = lax.broadcasted_iota(jnp.int32, (r, LANES), 1) < CHUNK
    pairs = [jnp.where(low, jnp.broadcast_to(a[:, 2 * j:2 * j + 1], (r, LANES)),
                       jnp.broadcast_to(a[:, 2 * j + 1:2 * j + 2], (r, LANES))) for j in range(c // 2)]
    return jnp.concatenate(pairs, axis=1)


def _ssm_kernel(u_ref, bbt_ref, pk_ref, cb_ref, dq_ref, ap_ref, o_ref, toet_ref, *, n_chunks):
    n = S5_STATE
    rows = S5_GROUP * CHUNK
    ncol = u_ref.shape[-1]
    u = u_ref[0].reshape(rows, ncol).astype(BF16)

    pk = pk_ref[0]
    cb = cb_ref[0]
    cr, ci, br, bi = [_spread_lanes(cb[:, S5_GROUP * k:S5_GROUP * (k + 1)]) for k in range(4)]
    p0r, p0i, p1r, p1i, rr, ri = [_tile_lanes(pk[:, CHUNK * k:CHUNK * (k + 1)], S5_GROUP) for k in range(6)]
    lk = jnp.concatenate([cr * p0r - ci * p0i, -(cr * p0i + ci * p0r)], axis=0)
    cinjt = jnp.concatenate([cr * p1r - ci * p1i, -(cr * p1i + ci * p1r)], axis=0).astype(BF16)
    bm = jnp.concatenate([rr * br - ri * bi, rr * bi + ri * br], axis=0).astype(BF16)

    kern = jnp.dot(bbt_ref[0], lk, preferred_element_type=F32, precision=HIGHEST)
    lane = lax.broadcasted_iota(jnp.int32, (S5_GROUP, rows), 1)
    own_lag0 = lane == lax.broadcasted_iota(jnp.int32, (S5_GROUP, rows), 0) * CHUNK
    kern = kern + jnp.where(own_lag0, jnp.concatenate([dq_ref[0]] * (rows // LANES), axis=1), 0.0)
    causal = (lax.broadcasted_iota(jnp.int32, (CHUNK, rows), 1) % CHUNK
              >= lax.broadcasted_iota(jnp.int32, (CHUNK, rows), 0))
    for p in range(S5_GROUP):
        base = jnp.broadcast_to(kern[p:p + 1, :], (CHUNK, rows))
        rolled = pltpu.roll(base, 0, 1, stride=1, stride_axis=0)
        toet_ref[p * CHUNK:(p + 1) * CHUNK, :] = jnp.where(causal, rolled, 0.0).astype(BF16)
    tn_dims = (((0,), (0,)), ((), ()))
    y = lax.dot_general(toet_ref[...], u, tn_dims, preferred_element_type=F32)
    x = jnp.dot(bm, u, preferred_element_type=F32)
    ap = ap_ref[0]
    col = lax.broadcasted_iota(jnp.int32, x.shape, 1) % n_chunks
    for k in range(int(math.log2(n_chunks))):
        sh = 1 << k
        xs = jnp.where(col >= sh, pltpu.roll(x, sh, 1), 0.0)
        xsw = jnp.concatenate([xs[n:], xs[:n]], axis=0)
        x = x + ap[:, 2 * k:2 * k + 1] * xs + ap[:, 2 * k + 1:2 * k + 2] * xsw
    h0 = jnp.where(col >= 1, pltpu.roll(x, 1, 1), 0.0)
    h0_hi = h0.astype(BF16)
    h0_lo = (h0 - h0_hi.astype(F32)).astype(BF16)
    y = y + lax.dot_general(cinjt, h0_hi, tn_dims, preferred_element_type=F32)
    y = y + lax.dot_general(cinjt, h0_lo, tn_dims, preferred_element_type=F32)
    o_ref[0] = jax.nn.gelu(y).reshape(S5_GROUP, CHUNK, ncol)


def _s5_out_kernel(v3_ref, x3_ref, w_ref, o3_ref, vbuf, xbuf, obuf, sem_v, sem_x, sem_o):
    s = pl.program_id(0)
    last = pl.num_programs(0) - 1
    slot = s % 2

    def load_v(step):
        return pltpu.make_async_copy(v3_ref.at[pl.ds(0, v3_ref.shape[0]), step], vbuf.at[step % 2],
                                     sem_v.at[step % 2])

    def load_x(step):
        return pltpu.make_async_copy(x3_ref.at[pl.ds(0, x3_ref.shape[0]), step], xbuf.at[step % 2],
                                     sem_x.at[step % 2])

    def store(step):
        return pltpu.make_async_copy(obuf.at[step % 2], o3_ref.at[pl.ds(0, o3_ref.shape[0]), step],
                                     sem_o.at[step % 2])

    def load(step):
        load_v(step).start()
        load_x(step).start()

    pl.when(s == 0)(lambda: load(s))
    pl.when(s < last)(lambda: load(s + 1))
    load_v(s).wait()
    load_x(s).wait()
    pl.when(s >= 2)(lambda: store(s - 2).wait())
    z = lax.dot_general(vbuf[slot].astype(BF16), w_ref[...], (((0,), (0,)), ((), ())),
                        preferred_element_type=F32)
    d = xbuf.shape[-1]
    obuf[slot] = xbuf[slot] + z[:, :d] * jax.nn.sigmoid(z[:, d:])
    store(s).start()

    @pl.when(s == last)
    def _():
        store(s - 1).wait()
        store(s).wait()


def _s5_layer(x, g_norm, w_in, lam_re, lam_im, log_dt, b_re, b_im, c_re, c_im, d_skip, w_glu,
              bsz, seq):
    m, d = x.shape
    n_chunks = seq // CHUNK
    ncol = bsz * n_chunks
    rows = S5_GROUP * CHUNK
    bbt, pk, cb, dq, ap = _s5_operators(lam_re, lam_im, log_dt, b_re, b_im, c_re, c_im, d_skip,
                                        n_chunks)
    x3 = x.reshape(ncol, CHUNK, d)
    hbm = pl.BlockSpec(memory_space=pl.ANY)
    u3 = pl.pallas_call(
        _s5_in_kernel,
        grid=(CHUNK,),
        in_specs=[hbm, pl.BlockSpec((1, d), lambda s: (0, 0)), pl.BlockSpec((d, d), lambda s: (0, 0))],
        out_specs=hbm,
        out_shape=jax.ShapeDtypeStruct((d, CHUNK, ncol), F32),
        scratch_shapes=[
            pltpu.VMEM((2, ncol, d), F32),
            pltpu.VMEM((2, d, ncol), F32),
            pltpu.SemaphoreType.DMA((2,)),
            pltpu.SemaphoreType.DMA((2,)),
        ],
        compiler_params=_params("arbitrary"),
        name="s5_in",
    )(x3, g_norm.reshape(1, d), w_in.T.astype(BF16))
    blk4 = (1, S5_GROUP, CHUNK, ncol)
    per_group = lambda a: pl.BlockSpec((1,) + a.shape[1:], lambda g: (g, 0, 0))
    v4 = pl.pallas_call(
        functools.partial(_ssm_kernel, n_chunks=n_chunks),
        grid=(S5_GROUPS,),
        in_specs=[pl.BlockSpec(blk4, lambda g: (g, 0, 0, 0))]
        + [per_group(a) for a in (bbt, pk, cb, dq, ap)],
        out_specs=pl.BlockSpec(blk4, lambda g: (g, 0, 0, 0)),
        out_shape=jax.ShapeDtypeStruct((S5_GROUPS, S5_GROUP, CHUNK, ncol), F32),
        scratch_shapes=[pltpu.VMEM((rows, rows), BF16)],
        compiler_params=_params("parallel"),
        name="s5_ssm",
    )(u3.reshape(S5_GROUPS, S5_GROUP, CHUNK, ncol), bbt, pk, cb, dq, ap)
    o3 = pl.pallas_call(
        _s5_out_kernel,
        grid=(CHUNK,),
        in_specs=[hbm, hbm, pl.BlockSpec((d, 2 * d), lambda s: (0, 0))],
        out_specs=hbm,
        out_shape=jax.ShapeDtypeStruct((ncol, CHUNK, d), F32),
        scratch_shapes=[
            pltpu.VMEM((2, d, ncol), F32),
            pltpu.VMEM((2, ncol, d), F32),
            pltpu.VMEM((2, ncol, d), F32),
            pltpu.SemaphoreType.DMA((2,)),
            pltpu.SemaphoreType.DMA((2,)),
            pltpu.SemaphoreType.DMA((2,)),
        ],
        compiler_params=_params("arbitrary"),
        name="s5_out",
    )(v4.reshape(d, CHUNK, ncol), x3, w_glu.astype(BF16))
    return o3.reshape(m, d)


ATT_PERIOD = ATT_TQ + ATT_TK


def _attn_bias_seed(rel_bias):
    k = jnp.arange(ATT_PERIOD)
    rel = jnp.where(k < ATT_TK, ATT_LEFT - k, ATT_LEFT + ATT_PERIOD - k)
    return jnp.take(rel_bias.astype(F32), jnp.clip(rel, -MAX_REL, MAX_REL) + MAX_REL, axis=1)


def _attn_kernel(x_ref, q_ref, k0_ref, k1_ref, k2_ref, v0_ref, v1_ref, v2_ref, seed_ref, wo_ref,
                 o_ref, bias_ref, ocat_ref):
    i = pl.program_id(1)

    @pl.when((pl.program_id(0) == 0) & (i == 0))
    def _():
        r = lax.broadcasted_iota(jnp.int32, (ATT_TQ, ATT_TK), 0)
        j = lax.broadcasted_iota(jnp.int32, (ATT_TQ, ATT_TK), 1)
        dchunk = (r + ATT_LEFT) // CHUNK - j // CHUNK
        valid = (dchunk >= 0) & (dchunk <= LEFT_CHUNKS)
        for h in range(ATT_HEADS):
            seed = jnp.broadcast_to(seed_ref[h:h + 1, :], (ATT_TQ, ATT_PERIOD))
            rolled = pltpu.roll(seed, 0, 1, stride=1, stride_axis=0)
            bias_ref[h] = jnp.where(valid, rolled[:, :ATT_TK] * LOG2E, NEG_INF)

    lane = lax.broadcasted_iota(jnp.int32, (1, LANES), 1)
    scale = ATT_HEAD_DIM ** -0.5
    heads_per = LANES // ATT_HEAD_DIM

    def attend(mask_pad):
        if mask_pad:
            col = lax.broadcasted_iota(jnp.int32, (ATT_TQ, ATT_TK), 1)
            pad = col < ATT_LEFT - i * ATT_TQ
        for hp in range(D_MODEL // LANES):
            sl = slice(hp * LANES, (hp + 1) * LANES)
            q = q_ref[:, sl] * scale
            k = jnp.concatenate([k0_ref[:, sl], k1_ref[:, sl], k2_ref[:, sl]], axis=0)
            v = jnp.concatenate([v0_ref[:, sl], v1_ref[:, sl], v2_ref[:, sl]], axis=0)
            acc = jnp.zeros((ATT_TQ, LANES), F32)
            for h in range(heads_per):
                in_head = (lane >= h * ATT_HEAD_DIM) & (lane < (h + 1) * ATT_HEAD_DIM)
                qh = jnp.where(in_head, q, jnp.zeros_like(q))
                s = lax.dot_general(qh, k, (((1,), (1,)), ((), ())), preferred_element_type=F32)
                s = s + bias_ref[hp * heads_per + h]
                if mask_pad:
                    s = jnp.where(pad, NEG_INF, s)
                p = jnp.exp2((s - jnp.max(s, axis=-1, keepdims=True)).astype(BF16))
                vh = jnp.where(in_head, v, jnp.ones_like(v))
                pv = jnp.dot(p, vh, preferred_element_type=F32)
                acc = jnp.where(in_head, pv / pltpu.roll(pv, ATT_HEAD_DIM, 1), acc)
            ocat_ref[:, sl] = acc.astype(BF16)

    first_steps = ATT_LEFT // ATT_TQ
    pl.when(i < first_steps)(functools.partial(attend, True))
    pl.when(i >= first_steps)(functools.partial(attend, False))
    o_ref[...] = x_ref[...] + jnp.dot(ocat_ref[...], wo_ref[...], preferred_element_type=F32)


def _attn_layer(x, g_norm, w_qkv, w_o, rel_bias, bsz, seq):
    m, d = x.shape
    w_qkv = jnp.concatenate([w_qkv[:, :d] * LOG2E, w_qkv[:, d:]], axis=1)
    qkv = _norm_linear(x, g_norm, w_qkv.astype(BF16), BF16)
    nq = seq // ATT_TQ

    def rows(col, back):
        return lambda b, i: (b * nq + jnp.maximum(i - back, 0), col)

    blk = (ATT_TQ, d)
    nkb = ATT_TK // ATT_TQ
    return pl.pallas_call(
        _attn_kernel,
        grid=(bsz, nq),
        in_specs=(
            [pl.BlockSpec(blk, rows(0, 0)), pl.BlockSpec(blk, rows(0, 0))]
            + [pl.BlockSpec(blk, rows(1, nkb - 1 - kb)) for kb in range(nkb)]
            + [pl.BlockSpec(blk, rows(2, nkb - 1 - kb)) for kb in range(nkb)]
            + [pl.BlockSpec((ATT_HEADS, ATT_PERIOD), lambda b, i: (0, 0)),
               pl.BlockSpec((d, d), lambda b, i: (0, 0))]
        ),
        out_specs=pl.BlockSpec(blk, rows(0, 0)),
        out_shape=jax.ShapeDtypeStruct((m, d), F32),
        scratch_shapes=[pltpu.VMEM((ATT_HEADS, ATT_TQ, ATT_TK), F32), pltpu.VMEM((ATT_TQ, d), BF16)],
        compiler_params=_params("arbitrary", "arbitrary"),
        name="chunk_attn",
    )(x, *([qkv] * (1 + 2 * nkb)), _attn_bias_seed(rel_bias), w_o.astype(BF16))


def _lru_kernel(x_ref, g_ref, win_ref, cw_ref, cb_ref, wax_ref, ba_ref, bx_ref, lam_ref, wout_ref,
                o_ref, prev_ref, carry_ref, a_s, b_s):
    t, w = x_ref.shape

    @pl.when(pl.program_id(1) == 0)
    def _():
        prev_ref[...] = jnp.zeros_like(prev_ref)
        carry_ref[...] = jnp.zeros_like(carry_ref)

    x = x_ref[...]
    z = jnp.dot(_rms(x, g_ref[...]).astype(BF16), win_ref[...], preferred_element_type=F32)
    gate = z[:, :w]
    xr = z[:, w:]
    xext = jnp.concatenate([prev_ref[...], xr], axis=0)
    cw = cw_ref[...]
    xc = cb_ref[...] + cw[0:1] * xext[SUBLANES - 3:SUBLANES - 3 + t]
    for k in range(1, CONV_W):
        xc = xc + cw[k:k + 1] * xext[SUBLANES - 3 + k:SUBLANES - 3 + k + t]
    prev_ref[...] = xr[t - SUBLANES:]

    xcb = xc.astype(BF16)
    pre_a, pre_x = [], []
    for blk in range(LRU_BLOCKS):
        pre = jnp.dot(xcb[:, blk * LRU_BLOCK_W:(blk + 1) * LRU_BLOCK_W], wax_ref[blk],
                      preferred_element_type=F32)
        pre_a.append(pre[:, :LRU_BLOCK_W])
        pre_x.append(pre[:, LRU_BLOCK_W:])
    r = jax.nn.sigmoid(jnp.concatenate(pre_a, axis=1) + ba_ref[...])
    ig = jax.nn.sigmoid(jnp.concatenate(pre_x, axis=1) + bx_ref[...])
    z = -lam_ref[...]
    softplus = jnp.maximum(z, 0.0) + jnp.log1p(jnp.exp(-jnp.abs(z)))
    log_a = -LRU_C * r * softplus
    a = jnp.exp(log_a)
    b = jnp.sqrt(1.0 - a * a) * (ig * xc)

    a3 = a.reshape(t // SUBLANES, SUBLANES, w)
    b3 = b.reshape(t // SUBLANES, SUBLANES, w)
    sub = lax.broadcasted_iota(jnp.int32, a3.shape, 1)
    for sh in (1, 2, 4):
        a_sh = jnp.where(sub >= sh, pltpu.roll(a3, sh, 1), 1.0)
        b_sh = jnp.where(sub >= sh, pltpu.roll(b3, sh, 1), 0.0)
        b3 = a3 * b_sh + b3
        a3 = a3 * a_sh
    a_s[...] = a3
    b_s[...] = b3

    def body(j, carry):
        hb = b_s[j] + a_s[j] * carry
        b_s[j] = hb
        return hb[SUBLANES - 1:SUBLANES, :]

    carry = lax.fori_loop(0, t // SUBLANES, body, carry_ref[0:1, :])
    carry_ref[0:1, :] = carry
    h = b_s[...].reshape(t, w)
    y = (jax.nn.gelu(gate) * h).astype(BF16)
    o_ref[...] = x + jnp.dot(y, wout_ref[...], preferred_element_type=F32)


def _lru_layer(x, g_norm, w_in, conv_w, conv_b, w_a, b_a, w_x, b_x, lam, w_out, bsz, seq):
    m, d = x.shape
    wax = jnp.concatenate([w_a, w_x], axis=-1).astype(BF16)
    nt = seq // LRU_T
    row = lambda v: v.reshape(1, d).astype(F32)
    vec = pl.BlockSpec((1, d), lambda b, i: (0, 0))
    return pl.pallas_call(
        _lru_kernel,
        grid=(bsz, nt),
        in_specs=[
            pl.BlockSpec((LRU_T, d), lambda b, i: (b * nt + i, 0)),
            vec,
            pl.BlockSpec((d, 2 * d), lambda b, i: (0, 0)),
            pl.BlockSpec((CONV_W, d), lambda b, i: (0, 0)),
            vec,
            pl.BlockSpec(wax.shape, lambda b, i: (0, 0, 0)),
            vec, vec, vec,
            pl.BlockSpec((d, d), lambda b, i: (0, 0)),
        ],
        out_specs=pl.BlockSpec((LRU_T, d), lambda b, i: (b * nt + i, 0)),
        out_shape=jax.ShapeDtypeStruct((m, d), F32),
        scratch_shapes=[
            pltpu.VMEM((SUBLANES, d), F32),
            pltpu.VMEM((SUBLANES, d), F32),
            pltpu.VMEM((LRU_T // SUBLANES, SUBLANES, d), F32),
            pltpu.VMEM((LRU_T // SUBLANES, SUBLANES, d), F32),
        ],
        compiler_params=_params("parallel", "arbitrary"),
        name="rglru",
    )(x, row(g_norm), w_in.astype(BF16), conv_w.astype(F32), row(conv_b), wax, row(b_a), row(b_x),
      row(lam), w_out.astype(BF16))


def _memattn(x, g_ref, wq_ref, k_ref, v_ref, wo_ref, ocat_ref):
    hn = _rms(x, g_ref[...]).astype(BF16)
    q = jnp.dot(hn, wq_ref[...], preferred_element_type=F32) * (MEM_HEAD_DIM ** -0.5)
    q = q.astype(BF16)
    for h in range(MEM_HEADS):
        sl = slice(h * MEM_HEAD_DIM, (h + 1) * MEM_HEAD_DIM)
        s = lax.dot_general(q[:, sl], k_ref[:, sl], (((1,), (1,)), ((), ())),
                            preferred_element_type=F32)
        mx = jnp.max(s, axis=-1, keepdims=True)
        p = jnp.exp(s - mx)
        l = jnp.sum(p, axis=-1, keepdims=True)
        o = jnp.dot(p.astype(BF16), v_ref[:, sl], preferred_element_type=F32) / l
        ocat_ref[:, sl] = o.astype(BF16)
    return x + jnp.dot(ocat_ref[...], wo_ref[...], preferred_element_type=F32)


def _mem_ffn_kernel(x_ref, g_ref, wq_ref, k_ref, v_ref, wo_ref, g2_ref, wgu_ref, wd_ref, o_ref, ocat_ref):
    ff = wd_ref.shape[0]
    x1 = _memattn(x_ref[...], g_ref, wq_ref, k_ref, v_ref, wo_ref, ocat_ref)
    hn = _rms(x1, g2_ref[...]).astype(BF16)
    o_ref[...] = x1
    for f in range(ff // FFN_TF):
        lo = f * FFN_TF
        gate = jnp.dot(hn, wgu_ref[:, lo:lo + FFN_TF], preferred_element_type=F32)
        up = jnp.dot(hn, wgu_ref[:, ff + lo:ff + lo + FFN_TF], preferred_element_type=F32)
        act = gate * jax.nn.sigmoid(gate) * up
        o_ref[...] += jnp.dot(act.astype(BF16), wd_ref[lo:lo + FFN_TF, :], preferred_element_type=F32)


def _mem_route_kernel(x_ref, g_ref, wq_ref, k_ref, v_ref, wo_ref, g2_ref, wrt_ref,
                      o_ref, hn_ref, ct_ref, st_ref, cnt_ref, ocat_ref):
    x1 = _memattn(x_ref[...], g_ref, wq_ref, k_ref, v_ref, wo_ref, ocat_ref)
    o_ref[...] = x1
    hn = _rms(x1, g2_ref[...])
    hn_ref[...] = hn.astype(BF16)
    lg = lax.dot_general(wrt_ref[...], hn, (((1,), (1,)), ((), ())),
                         preferred_element_type=F32, precision=HIGHEST)
    idx = lax.broadcasted_iota(jnp.int32, lg.shape, 0)
    m1 = jnp.max(lg, axis=0, keepdims=True)
    i1 = jnp.min(jnp.where(lg == m1, idx, N_EXPERTS), axis=0, keepdims=True)
    oh1 = idx == i1
    lg2 = jnp.where(oh1, -jnp.inf, lg)
    m2 = jnp.max(lg2, axis=0, keepdims=True)
    i2 = jnp.min(jnp.where(lg2 == m2, idx, N_EXPERTS), axis=0, keepdims=True)
    oh2 = idx == i2
    e2 = jnp.exp(m2 - m1)
    g1 = 1.0 / (1.0 + e2)
    g2 = e2 / (1.0 + e2)
    ct_ref[...] = jnp.where(oh1, g1, 0.0) + jnp.where(oh2, g2, 0.0)
    sel = (oh1 | oh2).astype(F32)
    st_ref[...] = sel
    cnt_ref[0] = jnp.broadcast_to(jnp.sum(sel, axis=1, keepdims=True), (N_EXPERTS, LANES))


def _mem_specs(d, nt, tm, layer):
    wspec = pl.BlockSpec((None, d, d), lambda b, i: (layer, 0, 0))
    return [
        pl.BlockSpec((tm, d), lambda b, i: (b * nt + i, 0)),
        pl.BlockSpec((1, d), lambda b, i: (0, 0)),
        wspec,
        pl.BlockSpec((N_MEM, d), lambda b, i: (b, 2 * layer)),
        pl.BlockSpec((N_MEM, d), lambda b, i: (b, 2 * layer + 1)),
        wspec,
    ]


def _mem_ffn_layer(x, g_mem, w_q, kv_all, layer, w_o, g_ffn, w_gu, w_down, ffn_layer, bsz, seq, tm=1024):
    m, d = x.shape
    nt = seq // tm
    ff = w_down.shape[1]
    resident = lambda shape: pl.BlockSpec((None,) + shape, lambda b, i: (ffn_layer, 0, 0),
                                          pipeline_mode=pl.Buffered(1))
    return pl.pallas_call(
        _mem_ffn_kernel,
        grid=(bsz, nt),
        in_specs=_mem_specs(d, nt, tm, layer) + [
            pl.BlockSpec((1, d), lambda b, i: (0, 0)),
            resident((d, 2 * ff)),
            resident((ff, d)),
        ],
        out_specs=pl.BlockSpec((tm, d), lambda b, i: (b * nt + i, 0)),
        out_shape=jax.ShapeDtypeStruct((m, d), F32),
        scratch_shapes=[pltpu.VMEM((tm, d), BF16)],
        compiler_params=_params("parallel", "parallel"),
        name="mem_ffn",
    )(x, g_mem.reshape(1, d), w_q, kv_all, kv_all, w_o, g_ffn.reshape(1, d), w_gu, w_down)


MOE_TM = 512
MOE_TR = 1024
MOE_HALF = MOE_TM // 2
MOE_ALIGN = 16
MOE_TF = 512
XS_W = D_MODEL + LANES
MOE_ZR = MOE_TR + MOE_HALF


def _tile_rank(st):
    tm = st.shape[1]
    before = (lax.broadcasted_iota(jnp.int32, (tm, tm), 0)
              < lax.broadcasted_iota(jnp.int32, (tm, tm), 1))
    return jnp.dot(st.astype(BF16), before.astype(BF16), preferred_element_type=F32)


def _one_hot_rows(rank, st, e, half):
    tm = rank.shape[1]
    rows = lax.broadcasted_iota(jnp.int32, (MOE_HALF, tm), 0).astype(F32) + float(half * MOE_HALF)
    hit = (rank[e:e + 1] == rows) & (st[e:e + 1] > 0.0)
    return jnp.where(hit, 1.0, 0.0).astype(BF16)


def _one_hot_all(rank, st):
    return jnp.concatenate([_one_hot_rows(rank, st, e, 0) for e in range(N_EXPERTS)], axis=0)


def _dispatch_kernel(base_ref, n_ref, tail_ref, hn_ref, ct_ref, st_ref, xs_ref, stage, stage2, zbuf,
                     sem, sem2, zsem):
    i = pl.program_id(0)
    last = pl.num_programs(0) - 1
    slot = i % 2

    @pl.when(i == 0)
    def _():
        zbuf[...] = jnp.zeros_like(zbuf)

        def zero(row, nrows):
            cp = pltpu.make_async_copy(zbuf.at[pl.ds(0, nrows)], xs_ref.at[pl.ds(row, nrows)], zsem)
            cp.start()
            cp.wait()

        for e in range(N_EXPERTS):
            zero(pl.multiple_of(tail_ref[e], MOE_ALIGN), MOE_ZR)

        def body(b, carry):
            zero(pl.multiple_of(b * MOE_TR, MOE_TR), MOE_TR)
            return carry

        lax.fori_loop(tail_ref[N_EXPERTS], xs_ref.shape[0] // MOE_TR, body, 0)

    hn = hn_ref[...]
    ct = ct_ref[...]
    st = st_ref[...]
    rank = _tile_rank(st)
    g_hi = ct.astype(BF16)
    g_lo = (ct - g_hi.astype(F32)).astype(BF16)
    gm = jnp.concatenate(
        [g_hi, g_lo, jnp.zeros((LANES - 2 * N_EXPERTS, ct.shape[1]), BF16)], axis=0)
    nt_dims = (((1,), (1,)), ((), ()))

    def copy(step, e):
        row = pl.multiple_of(base_ref[step * N_EXPERTS + e], MOE_ALIGN)
        return pltpu.make_async_copy(stage.at[step % 2, pl.ds(e * MOE_HALF, MOE_HALF)],
                                     xs_ref.at[pl.ds(row, MOE_HALF)], sem.at[step % 2, e])

    def copy2(e):
        row = pl.multiple_of(base_ref[i * N_EXPERTS + e] + MOE_HALF, MOE_ALIGN)
        return pltpu.make_async_copy(stage2.at[e], xs_ref.at[pl.ds(row, MOE_HALF)], sem2.at[e])

    def drain(step):
        for e in range(N_EXPERTS):
            copy(step, e).wait()

    p = _one_hot_all(rank, st)
    stage[slot, :, :D_MODEL] = jnp.dot(p, hn, preferred_element_type=F32).astype(BF16)
    stage[slot, :, D_MODEL:] = lax.dot_general(p, gm, nt_dims, preferred_element_type=F32).astype(BF16)
    pl.when(i >= 1)(lambda: drain(i - 1))
    for e in range(N_EXPERTS):
        copy(i, e).start()

    def fill2(e):
        p2 = _one_hot_rows(rank, st, e, 1)
        stage2[e, :, :D_MODEL] = jnp.dot(p2, hn, preferred_element_type=F32).astype(BF16)
        stage2[e, :, D_MODEL:] = lax.dot_general(p2, gm, nt_dims, preferred_element_type=F32).astype(BF16)
        copy2(e).start()

    for e in range(N_EXPERTS):
        pl.when(n_ref[i * N_EXPERTS + e] > MOE_HALF)(functools.partial(fill2, e))
    for e in range(N_EXPERTS):
        pl.when(n_ref[i * N_EXPERTS + e] > MOE_HALF)(lambda e=e: copy2(e).wait())

    pl.when(i == last)(lambda: drain(i))


def _expert_kernel(be_ref, nu_ref, x_ref, wg_ref, wu_ref, wd_ref, o_ref, acc_ref):
    b = pl.program_id(0)

    @pl.when(b < nu_ref[0])
    def _():
        e = be_ref[b]
        xe = x_ref[...]
        h = xe[:, :D_MODEL]
        ext = xe[:, D_MODEL:].astype(F32)
        lane = lax.broadcasted_iota(jnp.int32, ext.shape, 1)
        g = jnp.sum(jnp.where((lane == e) | (lane == e + N_EXPERTS), ext, 0.0), axis=1, keepdims=True)
        ff = wd_ref.shape[1]
        for c in range(ff // MOE_TF):
            lo = c * MOE_TF
            gate = jnp.dot(h, wg_ref[0, :, lo:lo + MOE_TF], preferred_element_type=F32)
            up = jnp.dot(h, wu_ref[0, :, lo:lo + MOE_TF], preferred_element_type=F32)
            act = (gate * jax.nn.sigmoid(gate) * up * g).astype(BF16)
            part = jnp.dot(act, wd_ref[0, lo:lo + MOE_TF, :], preferred_element_type=F32)
            if c == 0:
                acc_ref[...] = part
            else:
                acc_ref[...] += part
        o_ref[...] = acc_ref[...].astype(o_ref.dtype)

    @pl.when(b >= nu_ref[0])
    def _():
        o_ref[...] = jnp.zeros_like(o_ref)


def _combine_kernel(base_ref, n_ref, x_ref, st_ref, ys_ref, gout_ref, o_ref, ybuf, ybuf2, sem, sem2,
                    *, out_norm):
    i = pl.program_id(0)
    slot = i % 2

    def copy(step, e):
        row = pl.multiple_of(base_ref[step * N_EXPERTS + e], MOE_ALIGN)
        return pltpu.make_async_copy(ys_ref.at[pl.ds(row, MOE_HALF)],
                                     ybuf.at[step % 2, pl.ds(e * MOE_HALF, MOE_HALF)],
                                     sem.at[step % 2, e])

    def copy2(e):
        row = pl.multiple_of(base_ref[i * N_EXPERTS + e] + MOE_HALF, MOE_ALIGN)
        return pltpu.make_async_copy(ys_ref.at[pl.ds(row, MOE_HALF)], ybuf2.at[e], sem2.at[e])

    def fetch(step):
        for e in range(N_EXPERTS):
            copy(step, e).start()

    pl.when(i == 0)(lambda: fetch(i))
    pl.when(i + 1 < pl.num_programs(0))(lambda: fetch(i + 1))
    for e in range(N_EXPERTS):
        pl.when(n_ref[i * N_EXPERTS + e] > MOE_HALF)(lambda e=e: copy2(e).start())

    st = st_ref[...]
    rank = _tile_rank(st)
    p = _one_hot_all(rank, st)
    for e in range(N_EXPERTS):
        copy(i, e).wait()
    tn_dims = (((0,), (0,)), ((), ()))
    o_ref[...] = x_ref[...] + lax.dot_general(p, ybuf[slot], tn_dims, preferred_element_type=F32)

    def gather2(e):
        copy2(e).wait()
        p2 = _one_hot_rows(rank, st, e, 1)
        o_ref[...] += lax.dot_general(p2, ybuf2[e], tn_dims, preferred_element_type=F32)

    for e in range(N_EXPERTS):
        pl.when(n_ref[i * N_EXPERTS + e] > MOE_HALF)(functools.partial(gather2, e))
    if out_norm:
        o_ref[...] = _rms(o_ref[...], gout_ref[...])


def _moe_layer(x, g_mem, w_q, kv_all, layer, w_o, g_ffn, w_router, w_gu, w_down, moe_layer, bsz, seq,
               g_out=None):
    m, d = x.shape
    ff = w_down.shape[2]
    nt = m // MOE_TM
    ntb = seq // MOE_TM
    i32 = jnp.int32
    x, hn, ct, st, cnt = pl.pallas_call(
        _mem_route_kernel,
        grid=(bsz, ntb),
        in_specs=_mem_specs(d, ntb, MOE_TM, layer) + [
            pl.BlockSpec((1, d), lambda b, i: (0, 0)),
            pl.BlockSpec((N_EXPERTS, d), lambda b, i: (0, 0)),
        ],
        out_specs=[
            pl.BlockSpec((MOE_TM, d), lambda b, i: (b * ntb + i, 0)),
            pl.BlockSpec((MOE_TM, d), lambda b, i: (b * ntb + i, 0)),
            pl.BlockSpec((N_EXPERTS, MOE_TM), lambda b, i: (0, b * ntb + i)),
            pl.BlockSpec((N_EXPERTS, MOE_TM), lambda b, i: (0, b * ntb + i)),
            pl.BlockSpec((1, N_EXPERTS, LANES), lambda b, i: (b * ntb + i, 0, 0)),
        ],
        out_shape=[
            jax.ShapeDtypeStruct((m, d), F32),
            jax.ShapeDtypeStruct((m, d), BF16),
            jax.ShapeDtypeStruct((N_EXPERTS, m), F32),
            jax.ShapeDtypeStruct((N_EXPERTS, m), F32),
            jax.ShapeDtypeStruct((nt, N_EXPERTS, LANES), F32),
        ],
        scratch_shapes=[pltpu.VMEM((MOE_TM, d), BF16)],
        compiler_params=_params("parallel", "parallel"),
        name="mem_route",
    )(x, g_mem.reshape(1, d), w_q, kv_all, kv_all, w_o, g_ffn.reshape(1, d), w_router.T.astype(F32))

    n = cnt[:, :, 0].astype(i32)
    chunk = (n + MOE_ALIGN - 1) // MOE_ALIGN * MOE_ALIGN
    seg = (jnp.sum(chunk, axis=0) + MOE_HALF + MOE_TR - 1) // MOE_TR * MOE_TR
    seg_off = jnp.cumsum(seg) - seg
    base = (seg_off[None, :] + jnp.cumsum(chunk, axis=0) - chunk).reshape(-1).astype(i32)
    n_flat = n.reshape(-1)
    bound = 2 * m + nt * N_EXPERTS * (MOE_ALIGN - 1) + N_EXPERTS * (MOE_HALF + MOE_TR - 1)
    nb = -(-bound // MOE_TR)
    blk_end = jnp.cumsum(seg // MOE_TR)
    blk_expert = jnp.minimum(jnp.sum(jnp.arange(nb, dtype=i32)[:, None] >= blk_end[None, :], axis=1),
                             N_EXPERTS - 1).astype(i32)
    n_used = blk_end[-1:].astype(i32)
    rows = nb * MOE_TR
    tail = jnp.concatenate([jnp.maximum(seg_off + seg - MOE_ZR, 0), n_used]).astype(i32)

    tile_spec = lambda shape, imap: pl.BlockSpec(shape, imap)
    xs = pl.pallas_call(
        _dispatch_kernel,
        grid_spec=pltpu.PrefetchScalarGridSpec(
            num_scalar_prefetch=3,
            grid=(nt,),
            in_specs=[
                tile_spec((MOE_TM, d), lambda i, b, c, t: (i, 0)),
                tile_spec((N_EXPERTS, MOE_TM), lambda i, b, c, t: (0, i)),
                tile_spec((N_EXPERTS, MOE_TM), lambda i, b, c, t: (0, i)),
            ],
            out_specs=pl.BlockSpec(memory_space=pl.ANY),
            scratch_shapes=[
                pltpu.VMEM((2, N_EXPERTS * MOE_HALF, XS_W), BF16),
                pltpu.VMEM((N_EXPERTS, MOE_HALF, XS_W), BF16),
                pltpu.VMEM((MOE_ZR, XS_W), BF16),
                pltpu.SemaphoreType.DMA((2, N_EXPERTS)),
                pltpu.SemaphoreType.DMA((N_EXPERTS,)),
                pltpu.SemaphoreType.DMA(()),
            ],
        ),
        out_shape=jax.ShapeDtypeStruct((rows, XS_W), BF16),
        compiler_params=_params("arbitrary"),
        name="moe_dispatch",
    )(base, n_flat, tail, hn, ct, st)

    ys = pl.pallas_call(
        _expert_kernel,
        grid_spec=pltpu.PrefetchScalarGridSpec(
            num_scalar_prefetch=2,
            grid=(nb,),
            in_specs=[
                pl.BlockSpec((MOE_TR, XS_W), lambda b, be, nu: (b, 0)),
                pl.BlockSpec((None, 1, d, ff), lambda b, be, nu: (moe_layer, be[b], 0, 0)),
                pl.BlockSpec((None, 1, d, ff), lambda b, be, nu: (moe_layer, be[b], 0, 1)),
                pl.BlockSpec((None, 1, ff, d), lambda b, be, nu: (moe_layer, be[b], 0, 0)),
            ],
            out_specs=pl.BlockSpec((MOE_TR, d), lambda b, be, nu: (b, 0)),
            scratch_shapes=[pltpu.VMEM((MOE_TR, d), F32)],
        ),
        out_shape=jax.ShapeDtypeStruct((rows, d), BF16),
        compiler_params=_params("arbitrary"),
        name="moe_experts",
    )(blk_expert, n_used, xs, w_gu, w_gu, w_down)

    out_norm = g_out is not None
    g_out = jnp.ones((d,), F32) if g_out is None else g_out
    return pl.pallas_call(
        functools.partial(_combine_kernel, out_norm=out_norm),
        grid_spec=pltpu.PrefetchScalarGridSpec(
            num_scalar_prefetch=2,
            grid=(nt,),
            in_specs=[
                tile_spec((MOE_TM, d), lambda i, b, c: (i, 0)),
                tile_spec((N_EXPERTS, MOE_TM), lambda i, b, c: (0, i)),
                pl.BlockSpec(memory_space=pl.ANY),
                tile_spec((1, d), lambda i, b, c: (0, 0)),
            ],
            out_specs=tile_spec((MOE_TM, d), lambda i, b, c: (i, 0)),
            scratch_shapes=[
                pltpu.VMEM((2, N_EXPERTS * MOE_HALF, d), BF16),
                pltpu.VMEM((N_EXPERTS, MOE_HALF, d), BF16),
                pltpu.SemaphoreType.DMA((2, N_EXPERTS)),
                pltpu.SemaphoreType.DMA((N_EXPERTS,)),
            ],
        ),
        out_shape=jax.ShapeDtypeStruct((m, d), F32),
        compiler_params=_params("arbitrary"),
        name="moe_combine",
    )(base, n_flat, x, st, ys, g_out.reshape(1, d).astype(F32))


def kernel(x, mem, norm_mix, norm_mem, norm_ffn, mem_norm, final_norm, s5_w_in, s5_lam_re, s5_lam_im, s5_log_dt, s5_b_re, s5_b_im, s5_c_re, s5_c_im, s5_d, s5_w_glu, att_w_qkv, att_w_o, att_rel_bias, lru_w_in, lru_conv_w, lru_conv_b, lru_w_a, lru_b_a, lru_w_x, lru_b_x, lru_lam, lru_w_out, mem_w_q, mem_w_kv, mem_w_o, ffn_w_gu, ffn_w_down, moe_w_router, moe_w_gu, moe_w_down):
    bsz, seq, d = x.shape
    x = x.reshape(bsz * seq, d).astype(F32)

    w_kv_all = jnp.concatenate([mem_w_kv[i] for i in range(DEPTH)], axis=1).astype(BF16)
    kv_all = _norm_linear(mem.reshape(bsz * N_MEM, d).astype(F32), mem_norm, w_kv_all, BF16,
                          tm=bsz * N_MEM, tn=2 * d)

    mem_wq, mem_wo = mem_w_q.astype(BF16), mem_w_o.astype(BF16)
    ffn_wgu, ffn_wd = ffn_w_gu.astype(BF16), ffn_w_down.astype(BF16)
    moe_wgu, moe_wd = moe_w_gu.astype(BF16), moe_w_down.astype(BF16)
    for i in range(DEPTH):
        kind, j = i % 3, i // 3
        if kind == 0:
            x = _s5_layer(x, norm_mix[i], s5_w_in[j], s5_lam_re[j], s5_lam_im[j], s5_log_dt[j],
                          s5_b_re[j], s5_b_im[j], s5_c_re[j], s5_c_im[j], s5_d[j], s5_w_glu[j],
                          bsz, seq)
        elif kind == 1:
            x = _attn_layer(x, norm_mix[i], att_w_qkv[j], att_w_o[j], att_rel_bias[j], bsz, seq)
        else:
            x = _lru_layer(x, norm_mix[i], lru_w_in[j], lru_conv_w[j], lru_conv_b[j], lru_w_a[j],
                           lru_b_a[j], lru_w_x[j], lru_b_x[j], lru_lam[j], lru_w_out[j], bsz, seq)
        if i % 2 == 0:
            x = _mem_ffn_layer(x, norm_mem[i], mem_wq, kv_all, i, mem_wo, norm_ffn[i], ffn_wgu, ffn_wd,
                               i // 2, bsz, seq)
        else:
            x = _moe_layer(x, norm_mem[i], mem_wq, kv_all, i, mem_wo, norm_ffn[i], moe_w_router[i // 2],
                           moe_wgu, moe_wd, i // 2, bsz, seq,
                           g_out=final_norm if i == DEPTH - 1 else None)
    return x.reshape(bsz, seq, d)
```

```python
import functools
import math

import jax
import jax.numpy as jnp
from jax import lax
from jax.experimental import pallas as pl
from jax.experimental.pallas import tpu as pltpu

F32 = jnp.float32
BF16 = jnp.bfloat16
HIGHEST = lax.Precision.HIGHEST

D_MODEL = 1024
DEPTH = 4
CHUNK = 64
N_MEM = 256
EPS = 1e-6

S5_GROUP = 16
S5_GROUPS = D_MODEL // S5_GROUP
S5_STATE = 64

ATT_HEADS = 16
ATT_HEAD_DIM = D_MODEL // ATT_HEADS
LEFT_CHUNKS = 8
MAX_REL = 128
ATT_TQ = 4 * CHUNK
ATT_LEFT = LEFT_CHUNKS * CHUNK
ATT_TK = ATT_LEFT + ATT_TQ
NEG_INF = -1e30
LOG2E = math.log2(math.e)

LRU_BLOCKS = 8
LRU_BLOCK_W = D_MODEL // LRU_BLOCKS
CONV_W = 4
LRU_C = 8.0
LRU_T = 512
SUBLANES = 8
LANES = 128

MEM_HEADS = 4
MEM_HEAD_DIM = D_MODEL // MEM_HEADS

N_EXPERTS = 8
FFN_TF = 256

VMEM_LIMIT = 56 * 1024 * 1024


def _params(*sem):
    return pltpu.CompilerParams(dimension_semantics=sem, vmem_limit_bytes=VMEM_LIMIT)


def _rms(x, g):
    ms = jnp.mean(x * x, axis=-1, keepdims=True)
    return x * lax.rsqrt(ms + EPS) * g


def _norm_linear_kernel(x_ref, g_ref, w_ref, o_ref, hn_ref):
    @pl.when(pl.program_id(1) == 0)
    def _():
        hn_ref[...] = _rms(x_ref[...], g_ref[...]).astype(BF16)

    o_ref[...] = jnp.dot(hn_ref[...], w_ref[...], preferred_element_type=F32).astype(o_ref.dtype)


def _norm_linear(x, g, w, out_dtype, tm=512, tn=None):
    m, d = x.shape
    n = w.shape[1]
    tn = n if tn is None else tn
    return pl.pallas_call(
        _norm_linear_kernel,
        grid=(m // tm, n // tn),
        in_specs=[
            pl.BlockSpec((tm, d), lambda i, j: (i, 0)),
            pl.BlockSpec((1, d), lambda i, j: (0, 0)),
            pl.BlockSpec((d, tn), lambda i, j: (0, j)),
        ],
        out_specs=pl.BlockSpec((tm, tn), lambda i, j: (i, j)),
        out_shape=jax.ShapeDtypeStruct((m, n), out_dtype),
        scratch_shapes=[pltpu.VMEM((tm, d), BF16)],
        compiler_params=_params("parallel", "arbitrary"),
        name="norm_linear",
    )(x, g.reshape(1, d), w)


def _s5_operators(lam_re, lam_im, log_dt, b_re, b_im, c_re, c_im, d_skip, n_chunks):
    g, n, p = S5_GROUPS, S5_STATE, S5_GROUP
    lr = lam_re.astype(F32)
    li = lam_im.astype(F32)
    dt = jnp.exp(log_dt.astype(F32))[:, None]
    mag = jnp.exp(lr * dt)
    ar = mag * jnp.cos(li * dt)
    ai = mag * jnp.sin(li * dt)
    den = lr * lr + li * li
    fr = ((ar - 1.0) * lr + ai * li) / den
    fi = (ai * lr - (ar - 1.0) * li) / den
    bbr = fr[..., None] * b_re - fi[..., None] * b_im
    bbi = fr[..., None] * b_im + fi[..., None] * b_re
    bbt = jnp.concatenate([bbr, bbi], axis=1).transpose(0, 2, 1)

    kk = jnp.arange(CHUNK + 1, dtype=F32)
    mag_k = jnp.exp((lr * dt)[..., None] * kk)
    pr = mag_k * jnp.cos((li * dt)[..., None] * kk)
    pi = mag_k * jnp.sin((li * dt)[..., None] * kk)
    pk = jnp.concatenate([pr[..., :CHUNK], pi[..., :CHUNK], pr[..., 1:], pi[..., 1:],
                          pr[..., CHUNK - 1::-1], pi[..., CHUNK - 1::-1]], axis=-1)
    cb = jnp.concatenate([c_re.transpose(0, 2, 1), c_im.transpose(0, 2, 1), bbr, bbi], axis=-1)
    dq = jnp.broadcast_to(d_skip.reshape(g, p, 1), (g, p, LANES))
    qr, qi = [pr[..., CHUNK]], [pi[..., CHUNK]]
    for _ in range(int(math.log2(n_chunks)) - 1):
        qr, qi = qr + [qr[-1] * qr[-1] - qi[-1] * qi[-1]], qi + [2.0 * qr[-1] * qi[-1]]
    levels = len(qr)
    qr, qi = jnp.stack(qr), jnp.stack(qi)
    m1 = jnp.concatenate([qr, qr], axis=-1)
    m2 = jnp.concatenate([-qi, qi], axis=-1)
    ap = jnp.stack([m1, m2], axis=-1).transpose(1, 2, 0, 3).reshape(g, 2 * n, 2 * levels)
    return bbt, pk, cb, dq, ap


def _s5_in_kernel(x3_ref, g_ref, wt_ref, u3_ref, xbuf, ubuf, sem_in, sem_out):
    s = pl.program_id(0)
    last = pl.num_programs(0) - 1
    slot = s % 2

    def load(step):
        return pltpu.make_async_copy(x3_ref.at[pl.ds(0, x3_ref.shape[0]), step], xbuf.at[step % 2],
                                     sem_in.at[step % 2])

    def store(step):
        return pltpu.make_async_copy(ubuf.at[step % 2], u3_ref.at[pl.ds(0, u3_ref.shape[0]), step],
                                     sem_out.at[step % 2])

    pl.when(s == 0)(lambda: load(s).start())
    pl.when(s < last)(lambda: load(s + 1).start())
    load(s).wait()
    pl.when(s >= 2)(lambda: store(s - 2).wait())
    hn = _rms(xbuf[slot], g_ref[...]).astype(BF16)
    ubuf[slot] = lax.dot_general(wt_ref[...], hn, (((1,), (1,)), ((), ())), preferred_element_type=F32)
    store(s).start()

    @pl.when(s == last)
    def _():
        store(s - 1).wait()
        store(s).wait()


def _tile_lanes(a, times):
    a2 = jnp.concatenate([a, a], axis=1)
    return jnp.concatenate([a2] * (times // 2), axis=1)


def _spread_lanes(a):
    r, c = a.shape
    low = lax.broadcasted_iota(jnp.int32, (r, LANES), 1) < CHUNK
    pairs = [jnp.where(low, jnp.broadcast_to(a[:, 2 * j:2 * j + 1], (r, LANES)),
                       jnp.broadcast_to(a[:, 2 * j + 1:2 * j + 2], (r, LANES))) for j in range(c // 2)]
    return jnp.concatenate(pairs, axis=1)


def _ssm_kernel(u_ref, bbt_ref, pk_ref, cb_ref, dq_ref, ap_ref, o_ref, toet_ref, *, n_chunks):
    n = S5_STATE
    rows = S5_GROUP * CHUNK
    ncol = u_ref.shape[-1]
    u = u_ref[0].reshape(rows, ncol).astype(BF16)

    pk = pk_ref[0]
    cb = cb_ref[0]
    cr, ci, br, bi = [_spread_lanes(cb[:, S5_GROUP * k:S5_GROUP * (k + 1)]) for k in range(4)]
    p0r, p0i, p1r, p1i, rr, ri = [_tile_lanes(pk[:, CHUNK * k:CHUNK * (k + 1)], S5_GROUP) for k in range(6)]
    lk = jnp.concatenate([cr * p0r - ci * p0i, -(cr * p0i + ci * p0r)], axis=0)
    cinjt = jnp.concatenate([cr * p1r - ci * p1i, -(cr * p1i + ci * p1r)], axis=0).astype(BF16)
    bm = jnp.concatenate([rr * br - ri * bi, rr * bi + ri * br], axis=0).astype(BF16)

    kern = jnp.dot(bbt_ref[0], lk, preferred_element_type=F32, precision=HIGHEST)
    lane = lax.broadcasted_iota(jnp.int32, (S5_GROUP, rows), 1)
    own_lag0 = lane == lax.broadcasted_iota(jnp.int32, (S5_GROUP, rows), 0) * CHUNK
    kern = kern + jnp.where(own_lag0, jnp.concatenate([dq_ref[0]] * (rows // LANES), axis=1), 0.0)
    causal = (lax.broadcasted_iota(jnp.int32, (CHUNK, rows), 1) % CHUNK
              >= lax.broadcasted_iota(jnp.int32, (CHUNK, rows), 0))
    for p in range(S5_GROUP):
        base = jnp.broadcast_to(kern[p:p + 1, :], (CHUNK, rows))
        rolled = pltpu.roll(base, 0, 1, stride=1, stride_axis=0)
        toet_ref[p * CHUNK:(p + 1) * CHUNK, :] = jnp.where(causal, rolled, 0.0).astype(BF16)
    tn_dims = (((0,), (0,)), ((), ()))
    y = lax.dot_general(toet_ref[...], u, tn_dims, preferred_element_type=F32)
    x = jnp.dot(bm, u, preferred_element_type=F32)
    ap = ap_ref[0]
    col = lax.broadcasted_iota(jnp.int32, x.shape, 1) % n_chunks
    for k in range(int(math.log2(n_chunks))):
        sh = 1 << k
        xs = jnp.where(col >= sh, pltpu.roll(x, sh, 1), 0.0)
        xsw = jnp.concatenate([xs[n:], xs[:n]], axis=0)
        x = x + ap[:, 2 * k:2 * k + 1] * xs + ap[:, 2 * k + 1:2 * k + 2] * xsw
    h0 = jnp.where(col >= 1, pltpu.roll(x, 1, 1), 0.0)
    h0_hi = h0.astype(BF16)
    h0_lo = (h0 - h0_hi.astype(F32)).astype(BF16)
    y = y + lax.dot_general(cinjt, h0_hi, tn_dims, preferred_element_type=F32)
    y = y + lax.dot_general(cinjt, h0_lo, tn_dims, preferred_element_type=F32)
    o_ref[0] = jax.nn.gelu(y).reshape(S5_GROUP, CHUNK, ncol)


def _s5_out_kernel(v3_ref, x3_ref, w_ref, o3_ref, vbuf, xbuf, obuf, sem_v, sem_x, sem_o):
    s = pl.program_id(0)
    last = pl.num_programs(0) - 1
    slot = s % 2

    def load_v(step):
        return pltpu.make_async_copy(v3_ref.at[pl.ds(0, v3_ref.shape[0]), step], vbuf.at[step % 2],
                                     sem_v.at[step % 2])

    def load_x(step):
        return pltpu.make_async_copy(x3_ref.at[pl.ds(0, x3_ref.shape[0]), step], xbuf.at[step % 2],
                                     sem_x.at[step % 2])

    def store(step):
        return pltpu.make_async_copy(obuf.at[step % 2], o3_ref.at[pl.ds(0, o3_ref.shape[0]), step],
                                     sem_o.at[step % 2])

    def load(step):
        load_v(step).start()
        load_x(step).start()

    pl.when(s == 0)(lambda: load(s))
    pl.when(s < last)(lambda: load(s + 1))
    load_v(s).wait()
    load_x(s).wait()
    pl.when(s >= 2)(lambda: store(s - 2).wait())
    z = lax.dot_general(vbuf[slot].astype(BF16), w_ref[...], (((0,), (0,)), ((), ())),
                        preferred_element_type=F32)
    d = xbuf.shape[-1]
    obuf[slot] = xbuf[slot] + z[:, :d] * jax.nn.sigmoid(z[:, d:])
    store(s).start()

    @pl.when(s == last)
    def _():
        store(s - 1).wait()
        store(s).wait()


def _s5_layer(x, g_norm, w_in, lam_re, lam_im, log_dt, b_re, b_im, c_re, c_im, d_skip, w_glu,
              bsz, seq):
    m, d = x.shape
    n_chunks = seq // CHUNK
    ncol = bsz * n_chunks
    rows = S5_GROUP * CHUNK
    bbt, pk, cb, dq, ap = _s5_operators(lam_re, lam_im, log_dt, b_re, b_im, c_re, c_im, d_skip,
                                        n_chunks)
    x3 = x.reshape(ncol, CHUNK, d)
    hbm = pl.BlockSpec(memory_space=pl.ANY)
    u3 = pl.pallas_call(
        _s5_in_kernel,
        grid=(CHUNK,),
        in_specs=[hbm, pl.BlockSpec((1, d), lambda s: (0, 0)), pl.BlockSpec((d, d), lambda s: (0, 0))],
        out_specs=hbm,
        out_shape=jax.ShapeDtypeStruct((d, CHUNK, ncol), F32),
        scratch_shapes=[
            pltpu.VMEM((2, ncol, d), F32),
            pltpu.VMEM((2, d, ncol), F32),
            pltpu.SemaphoreType.DMA((2,)),
            pltpu.SemaphoreType.DMA((2,)),
        ],
        compiler_params=_params("arbitrary"),
        name="s5_in",
    )(x3, g_norm.reshape(1, d), w_in.T.astype(BF16))
    blk4 = (1, S5_GROUP, CHUNK, ncol)
    per_group = lambda a: pl.BlockSpec((1,) + a.shape[1:], lambda g: (g, 0, 0))
    v4 = pl.pallas_call(
        functools.partial(_ssm_kernel, n_chunks=n_chunks),
        grid=(S5_GROUPS,),
        in_specs=[pl.BlockSpec(blk4, lambda g: (g, 0, 0, 0))]
        + [per_group(a) for a in (bbt, pk, cb, dq, ap)],
        out_specs=pl.BlockSpec(blk4, lambda g: (g, 0, 0, 0)),
        out_shape=jax.ShapeDtypeStruct((S5_GROUPS, S5_GROUP, CHUNK, ncol), F32),
        scratch_shapes=[pltpu.VMEM((rows, rows), BF16)],
        compiler_params=_params("parallel"),
        name="s5_ssm",
    )(u3.reshape(S5_GROUPS, S5_GROUP, CHUNK, ncol), bbt, pk, cb, dq, ap)
    o3 = pl.pallas_call(
        _s5_out_kernel,
        grid=(CHUNK,),
        in_specs=[hbm, hbm, pl.BlockSpec((d, 2 * d), lambda s: (0, 0))],
        out_specs=hbm,
        out_shape=jax.ShapeDtypeStruct((ncol, CHUNK, d), F32),
        scratch_shapes=[
            pltpu.VMEM((2, d, ncol), F32),
            pltpu.VMEM((2, ncol, d), F32),
            pltpu.VMEM((2, ncol, d), F32),
            pltpu.SemaphoreType.DMA((2,)),
            pltpu.SemaphoreType.DMA((2,)),
            pltpu.SemaphoreType.DMA((2,)),
        ],
        compiler_params=_params("arbitrary"),
        name="s5_out",
    )(v4.reshape(d, CHUNK, ncol), x3, w_glu.astype(BF16))
    return o3.reshape(m, d)


ATT_PERIOD = ATT_TQ + ATT_TK


def _attn_bias_seed(rel_bias):
    k = jnp.arange(ATT_PERIOD)
    rel = jnp.where(k < ATT_TK, ATT_LEFT - k, ATT_LEFT + ATT_PERIOD - k)
    return jnp.take(rel_bias.astype(F32), jnp.clip(rel, -MAX_REL, MAX_REL) + MAX_REL, axis=1)


def _attn_kernel(x_ref, q_ref, k0_ref, k1_ref, k2_ref, v0_ref, v1_ref, v2_ref, seed_ref, wo_ref,
                 o_ref, bias_ref, ocat_ref):
    i = pl.program_id(1)

    @pl.when((pl.program_id(0) == 0) & (i == 0))
    def _():
        r = lax.broadcasted_iota(jnp.int32, (ATT_TQ, ATT_TK), 0)
        j = lax.broadcasted_iota(jnp.int32, (ATT_TQ, ATT_TK), 1)
        dchunk = (r + ATT_LEFT) // CHUNK - j // CHUNK
        valid = (dchunk >= 0) & (dchunk <= LEFT_CHUNKS)
        for h in range(ATT_HEADS):
            seed = jnp.broadcast_to(seed_ref[h:h + 1, :], (ATT_TQ, ATT_PERIOD))
            rolled = pltpu.roll(seed, 0, 1, stride=1, stride_axis=0)
            bias_ref[h] = jnp.where(valid, rolled[:, :ATT_TK] * LOG2E, NEG_INF)

    lane = lax.broadcasted_iota(jnp.int32, (1, LANES), 1)
    col = lax.broadcasted_iota(jnp.int32, (ATT_TQ, ATT_TK), 1)
    pad = col < ATT_LEFT - i * ATT_TQ
    scale = ATT_HEAD_DIM ** -0.5
    heads_per = LANES // ATT_HEAD_DIM
    for hp in range(D_MODEL // LANES):
        sl = slice(hp * LANES, (hp + 1) * LANES)
        q = q_ref[:, sl] * scale
        k = jnp.concatenate([k0_ref[:, sl], k1_ref[:, sl], k2_ref[:, sl]], axis=0)
        v = jnp.concatenate([v0_ref[:, sl], v1_ref[:, sl], v2_ref[:, sl]], axis=0)
        acc = jnp.zeros((ATT_TQ, LANES), F32)
        for h in range(heads_per):
            in_head = (lane >= h * ATT_HEAD_DIM) & (lane < (h + 1) * ATT_HEAD_DIM)
            qh = jnp.where(in_head, q, jnp.zeros_like(q))
            s = lax.dot_general(qh, k, (((1,), (1,)), ((), ())), preferred_element_type=F32)
            s = jnp.where(pad, NEG_INF, s + bias_ref[hp * heads_per + h])
            p = jnp.exp2(s - jnp.max(s, axis=-1, keepdims=True))
            vh = jnp.where(in_head, v, jnp.ones_like(v))
            pv = jnp.dot(p.astype(BF16), vh, preferred_element_type=F32)
            acc = jnp.where(in_head, pv / pltpu.roll(pv, ATT_HEAD_DIM, 1), acc)
        ocat_ref[:, sl] = acc.astype(BF16)
    o_ref[...] = x_ref[...] + jnp.dot(ocat_ref[...], wo_ref[...], preferred_element_type=F32)


def _attn_layer(x, g_norm, w_qkv, w_o, rel_bias, bsz, seq):
    m, d = x.shape
    w_qkv = jnp.concatenate([w_qkv[:, :d] * LOG2E, w_qkv[:, d:]], axis=1)
    qkv = _norm_linear(x, g_norm, w_qkv.astype(BF16), BF16, tm=1024)
    nq = seq // ATT_TQ

    def rows(col, back):
        return lambda b, i: (b * nq + jnp.maximum(i - back, 0), col)

    blk = (ATT_TQ, d)
    nkb = ATT_TK // ATT_TQ
    return pl.pallas_call(
        _attn_kernel,
        grid=(bsz, nq),
        in_specs=(
            [pl.BlockSpec(blk, rows(0, 0)), pl.BlockSpec(blk, rows(0, 0))]
            + [pl.BlockSpec(blk, rows(1, nkb - 1 - kb)) for kb in range(nkb)]
            + [pl.BlockSpec(blk, rows(2, nkb - 1 - kb)) for kb in range(nkb)]
            + [pl.BlockSpec((ATT_HEADS, ATT_PERIOD), lambda b, i: (0, 0)),
               pl.BlockSpec((d, d), lambda b, i: (0, 0))]
        ),
        out_specs=pl.BlockSpec(blk, rows(0, 0)),
        out_shape=jax.ShapeDtypeStruct((m, d), F32),
        scratch_shapes=[pltpu.VMEM((ATT_HEADS, ATT_TQ, ATT_TK), F32), pltpu.VMEM((ATT_TQ, d), BF16)],
        compiler_params=_params("arbitrary", "arbitrary"),
        name="chunk_attn",
    )(x, *([qkv] * (1 + 2 * nkb)), _attn_bias_seed(rel_bias), w_o.astype(BF16))


def _lru_kernel(x_ref, g_ref, win_ref, cw_ref, cb_ref, wax_ref, ba_ref, bx_ref, lam_ref, wout_ref,
                o_ref, prev_ref, carry_ref, a_s, b_s):
    t, w = x_ref.shape

    @pl.when(pl.program_id(1) == 0)
    def _():
        prev_ref[...] = jnp.zeros_like(prev_ref)
        carry_ref[...] = jnp.zeros_like(carry_ref)

    x = x_ref[...]
    z = jnp.dot(_rms(x, g_ref[...]).astype(BF16), win_ref[...], preferred_element_type=F32)
    gate = z[:, :w]
    xr = z[:, w:]
    xext = jnp.concatenate([prev_ref[...], xr], axis=0)
    cw = cw_ref[...]
    xc = cb_ref[...] + cw[0:1] * xext[SUBLANES - 3:SUBLANES - 3 + t]
    for k in range(1, CONV_W):
        xc = xc + cw[k:k + 1] * xext[SUBLANES - 3 + k:SUBLANES - 3 + k + t]
    prev_ref[...] = xr[t - SUBLANES:]

    xcb = xc.astype(BF16)
    pre_a, pre_x = [], []
    for blk in range(LRU_BLOCKS):
        pre = jnp.dot(xcb[:, blk * LRU_BLOCK_W:(blk + 1) * LRU_BLOCK_W], wax_ref[blk],
                      preferred_element_type=F32)
        pre_a.append(pre[:, :LRU_BLOCK_W])
        pre_x.append(pre[:, LRU_BLOCK_W:])
    r = jax.nn.sigmoid(jnp.concatenate(pre_a, axis=1) + ba_ref[...])
    ig = jax.nn.sigmoid(jnp.concatenate(pre_x, axis=1) + bx_ref[...])
    z = -lam_ref[...]
    softplus = jnp.maximum(z, 0.0) + jnp.log1p(jnp.exp(-jnp.abs(z)))
    log_a = -LRU_C * r * softplus
    a = jnp.exp(log_a)
    b = jnp.sqrt(1.0 - a * a) * (ig * xc)

    a3 = a.reshape(t // SUBLANES, SUBLANES, w)
    b3 = b.reshape(t // SUBLANES, SUBLANES, w)
    sub = lax.broadcasted_iota(jnp.int32, a3.shape, 1)
    for sh in (1, 2, 4):
        a_sh = jnp.where(sub >= sh, pltpu.roll(a3, sh, 1), 1.0)
        b_sh = jnp.where(sub >= sh, pltpu.roll(b3, sh, 1), 0.0)
        b3 = a3 * b_sh + b3
        a3 = a3 * a_sh
    a_s[...] = a3
    b_s[...] = b3
    ga = a_s[:, SUBLANES - 1, :]
    gb = b_s[:, SUBLANES - 1, :]
    grp = lax.broadcasted_iota(jnp.int32, ga.shape, 0)
    sh = 1
    while sh < t // SUBLANES:
        ga_sh = jnp.where(grp >= sh, pltpu.roll(ga, sh, 0), 1.0)
        gb_sh = jnp.where(grp >= sh, pltpu.roll(gb, sh, 0), 0.0)
        gb = ga * gb_sh + gb
        ga = ga * ga_sh
        sh *= 2
    leaving = gb + ga * carry_ref[0:1, :]
    entering = jnp.where(grp >= 1, pltpu.roll(leaving, 1, 0), carry_ref[0:1, :])
    carry_ref[0:1, :] = leaving[t // SUBLANES - 1:t // SUBLANES, :]
    h = (b3 + a3 * entering[:, None, :]).reshape(t, w)
    y = (jax.nn.gelu(gate) * h).astype(BF16)
    o_ref[...] = x + jnp.dot(y, wout_ref[...], preferred_element_type=F32)


def _lru_layer(x, g_norm, w_in, conv_w, conv_b, w_a, b_a, w_x, b_x, lam, w_out, bsz, seq):
    m, d = x.shape
    wax = jnp.concatenate([w_a, w_x], axis=-1).astype(BF16)
    nt = seq // LRU_T
    row = lambda v: v.reshape(1, d).astype(F32)
    vec = pl.BlockSpec((1, d), lambda b, i: (0, 0))
    return pl.pallas_call(
        _lru_kernel,
        grid=(bsz, nt),
        in_specs=[
            pl.BlockSpec((LRU_T, d), lambda b, i: (b * nt + i, 0)),
            vec,
            pl.BlockSpec((d, 2 * d), lambda b, i: (0, 0)),
            pl.BlockSpec((CONV_W, d), lambda b, i: (0, 0)),
            vec,
            pl.BlockSpec(wax.shape, lambda b, i: (0, 0, 0)),
            vec, vec, vec,
            pl.BlockSpec((d, d), lambda b, i: (0, 0)),
        ],
        out_specs=pl.BlockSpec((LRU_T, d), lambda b, i: (b * nt + i, 0)),
        out_shape=jax.ShapeDtypeStruct((m, d), F32),
        scratch_shapes=[
            pltpu.VMEM((SUBLANES, d), F32),
            pltpu.VMEM((SUBLANES, d), F32),
            pltpu.VMEM((LRU_T // SUBLANES, SUBLANES, d), F32),
            pltpu.VMEM((LRU_T // SUBLANES, SUBLANES, d), F32),
        ],
        compiler_params=_params("parallel", "arbitrary"),
        name="rglru",
    )(x, row(g_norm), w_in.astype(BF16), conv_w.astype(F32), row(conv_b), wax, row(b_a), row(b_x),
      row(lam), w_out.astype(BF16))


def _memattn(x, g_ref, wq_ref, k_ref, v_ref, wo_ref, ocat_ref):
    hn = _rms(x, g_ref[...]).astype(BF16)
    q = jnp.dot(hn, wq_ref[...], preferred_element_type=F32) * (MEM_HEAD_DIM ** -0.5)
    q = q.astype(BF16)
    for h in range(MEM_HEADS):
        sl = slice(h * MEM_HEAD_DIM, (h + 1) * MEM_HEAD_DIM)
        s = lax.dot_general(q[:, sl], k_ref[:, sl], (((1,), (1,)), ((), ())),
                            preferred_element_type=F32)
        mx = jnp.max(s, axis=-1, keepdims=True)
        p = jnp.exp(s - mx)
        l = jnp.sum(p, axis=-1, keepdims=True)
        o = jnp.dot(p.astype(BF16), v_ref[:, sl], preferred_element_type=F32) / l
        ocat_ref[:, sl] = o.astype(BF16)
    return x + jnp.dot(ocat_ref[...], wo_ref[...], preferred_element_type=F32)


def _mem_ffn_kernel(x_ref, g_ref, wq_ref, k_ref, v_ref, wo_ref, g2_ref, wgu_ref, wd_ref, o_ref, ocat_ref):
    ff = wd_ref.shape[0]
    x1 = _memattn(x_ref[...], g_ref, wq_ref, k_ref, v_ref, wo_ref, ocat_ref)
    hn = _rms(x1, g2_ref[...]).astype(BF16)
    o_ref[...] = x1
    for f in range(ff // FFN_TF):
        lo = f * FFN_TF
        gate = jnp.dot(hn, wgu_ref[:, lo:lo + FFN_TF], preferred_element_type=F32)
        up = jnp.dot(hn, wgu_ref[:, ff + lo:ff + lo + FFN_TF], preferred_element_type=F32)
        act = gate * jax.nn.sigmoid(gate) * up
        o_ref[...] += jnp.dot(act.astype(BF16), wd_ref[lo:lo + FFN_TF, :], preferred_element_type=F32)


def _mem_route_kernel(x_ref, g_ref, wq_ref, k_ref, v_ref, wo_ref, g2_ref, wrt_ref,
                      o_ref, hn_ref, ct_ref, st_ref, cnt_ref, ocat_ref):
    x1 = _memattn(x_ref[...], g_ref, wq_ref, k_ref, v_ref, wo_ref, ocat_ref)
    o_ref[...] = x1
    hn = _rms(x1, g2_ref[...])
    hn_ref[...] = hn.astype(BF16)
    lg = lax.dot_general(wrt_ref[...], hn, (((1,), (1,)), ((), ())),
                         preferred_element_type=F32, precision=HIGHEST)
    idx = lax.broadcasted_iota(jnp.int32, lg.shape, 0)
    m1 = jnp.max(lg, axis=0, keepdims=True)
    i1 = jnp.min(jnp.where(lg == m1, idx, N_EXPERTS), axis=0, keepdims=True)
    oh1 = idx == i1
    lg2 = jnp.where(oh1, -jnp.inf, lg)
    m2 = jnp.max(lg2, axis=0, keepdims=True)
    i2 = jnp.min(jnp.where(lg2 == m2, idx, N_EXPERTS), axis=0, keepdims=True)
    oh2 = idx == i2
    e2 = jnp.exp(m2 - m1)
    g1 = 1.0 / (1.0 + e2)
    g2 = e2 / (1.0 + e2)
    ct_ref[...] = jnp.where(oh1, g1, 0.0) + jnp.where(oh2, g2, 0.0)
    sel = (oh1 | oh2).astype(F32)
    st_ref[...] = sel
    cnt_ref[0] = jnp.broadcast_to(jnp.sum(sel, axis=1, keepdims=True), (N_EXPERTS, LANES))


def _mem_specs(d, nt, tm, layer):
    wspec = pl.BlockSpec((None, d, d), lambda b, i: (layer, 0, 0))
    return [
        pl.BlockSpec((tm, d), lambda b, i: (b * nt + i, 0)),
        pl.BlockSpec((1, d), lambda b, i: (0, 0)),
        wspec,
        pl.BlockSpec((N_MEM, d), lambda b, i: (b, 2 * layer)),
        pl.BlockSpec((N_MEM, d), lambda b, i: (b, 2 * layer + 1)),
        wspec,
    ]


def _mem_ffn_layer(x, g_mem, w_q, kv_all, layer, w_o, g_ffn, w_gu, w_down, ffn_layer, bsz, seq, tm=1024):
    m, d = x.shape
    nt = seq // tm
    ff = w_down.shape[1]
    resident = lambda shape: pl.BlockSpec((None,) + shape, lambda b, i: (ffn_layer, 0, 0),
                                          pipeline_mode=pl.Buffered(1))
    return pl.pallas_call(
        _mem_ffn_kernel,
        grid=(bsz, nt),
        in_specs=_mem_specs(d, nt, tm, layer) + [
            pl.BlockSpec((1, d), lambda b, i: (0, 0)),
            resident((d, 2 * ff)),
            resident((ff, d)),
        ],
        out_specs=pl.BlockSpec((tm, d), lambda b, i: (b * nt + i, 0)),
        out_shape=jax.ShapeDtypeStruct((m, d), F32),
        scratch_shapes=[pltpu.VMEM((tm, d), BF16)],
        compiler_params=_params("parallel", "parallel"),
        name="mem_ffn",
    )(x, g_mem.reshape(1, d), w_q, kv_all, kv_all, w_o, g_ffn.reshape(1, d), w_gu, w_down)


MOE_TM = 512
MOE_TR = 512
MOE_HALF = MOE_TM // 2
MOE_ALIGN = 16
MOE_TF = 512
XS_W = D_MODEL + LANES
MOE_ZR = MOE_TR + MOE_HALF


def _tile_rank(st):
    tm = st.shape[1]
    before = (lax.broadcasted_iota(jnp.int32, (tm, tm), 0)
              < lax.broadcasted_iota(jnp.int32, (tm, tm), 1))
    return jnp.dot(st.astype(BF16), before.astype(BF16), preferred_element_type=F32)


def _one_hot_rows(rank, st, e, half):
    tm = rank.shape[1]
    rows = lax.broadcasted_iota(jnp.int32, (MOE_HALF, tm), 0).astype(F32) + float(half * MOE_HALF)
    hit = (rank[e:e + 1] == rows) & (st[e:e + 1] > 0.0)
    return jnp.where(hit, 1.0, 0.0).astype(BF16)


def _one_hot_all(rank, st):
    return jnp.concatenate([_one_hot_rows(rank, st, e, 0) for e in range(N_EXPERTS)], axis=0)


def _dispatch_kernel(base_ref, n_ref, tail_ref, hn_ref, ct_ref, st_ref, xs_ref, stage, stage2, zbuf,
                     sem, sem2, zsem):
    i = pl.program_id(0)
    last = pl.num_programs(0) - 1
    slot = i % 2

    @pl.when(i == 0)
    def _():
        zbuf[...] = jnp.zeros_like(zbuf)

        def zero(row, nrows):
            cp = pltpu.make_async_copy(zbuf.at[pl.ds(0, nrows)], xs_ref.at[pl.ds(row, nrows)], zsem)
            cp.start()
            cp.wait()

        for e in range(N_EXPERTS):
            zero(pl.multiple_of(tail_ref[e], MOE_ALIGN), MOE_ZR)

        def body(b, carry):
            zero(pl.multiple_of(b * MOE_TR, MOE_TR), MOE_TR)
            return carry

        lax.fori_loop(tail_ref[N_EXPERTS], xs_ref.shape[0] // MOE_TR, body, 0)

    hn = hn_ref[...]
    ct = ct_ref[...]
    st = st_ref[...]
    rank = _tile_rank(st)
    g_hi = ct.astype(BF16)
    g_lo = (ct - g_hi.astype(F32)).astype(BF16)
    gm = jnp.concatenate(
        [g_hi, g_lo, jnp.zeros((LANES - 2 * N_EXPERTS, ct.shape[1]), BF16)], axis=0)
    nt_dims = (((1,), (1,)), ((), ()))

    def copy(step, e):
        row = pl.multiple_of(base_ref[step * N_EXPERTS + e], MOE_ALIGN)
        return pltpu.make_async_copy(stage.at[step % 2, pl.ds(e * MOE_HALF, MOE_HALF)],
                                     xs_ref.at[pl.ds(row, MOE_HALF)], sem.at[step % 2, e])

    def copy2(e):
        row = pl.multiple_of(base_ref[i * N_EXPERTS + e] + MOE_HALF, MOE_ALIGN)
        return pltpu.make_async_copy(stage2.at[e], xs_ref.at[pl.ds(row, MOE_HALF)], sem2.at[e])

    def drain(step):
        for e in range(N_EXPERTS):
            copy(step, e).wait()

    p = _one_hot_all(rank, st)
    stage[slot, :, :D_MODEL] = jnp.dot(p, hn, preferred_element_type=F32).astype(BF16)
    stage[slot, :, D_MODEL:] = lax.dot_general(p, gm, nt_dims, preferred_element_type=F32).astype(BF16)
    pl.when(i >= 1)(lambda: drain(i - 1))
    for e in range(N_EXPERTS):
        copy(i, e).start()

    def fill2(e):
        p2 = _one_hot_rows(rank, st, e, 1)
        stage2[e, :, :D_MODEL] = jnp.dot(p2, hn, preferred_element_type=F32).astype(BF16)
        stage2[e, :, D_MODEL:] = lax.dot_general(p2, gm, nt_dims, preferred_element_type=F32).astype(BF16)
        copy2(e).start()

    for e in range(N_EXPERTS):
        pl.when(n_ref[i * N_EXPERTS + e] > MOE_HALF)(functools.partial(fill2, e))
    for e in range(N_EXPERTS):
        pl.when(n_ref[i * N_EXPERTS + e] > MOE_HALF)(lambda e=e: copy2(e).wait())

    pl.when(i == last)(lambda: drain(i))


def _expert_kernel(be_ref, nu_ref, x_ref, wg_ref, wu_ref, wd_ref, o_ref, acc_ref):
    b = pl.program_id(0)

    @pl.when(b < nu_ref[0])
    def _():
        e = be_ref[b]
        xe = x_ref[...]
        h = xe[:, :D_MODEL]
        ext = xe[:, D_MODEL:].astype(F32)
        lane = lax.broadcasted_iota(jnp.int32, ext.shape, 1)
        g = jnp.sum(jnp.where((lane == e) | (lane == e + N_EXPERTS), ext, 0.0), axis=1, keepdims=True)
        ff = wd_ref.shape[1]
        for c in range(ff // MOE_TF):
            lo = c * MOE_TF
            gate = jnp.dot(h, wg_ref[0, :, lo:lo + MOE_TF], preferred_element_type=F32)
            up = jnp.dot(h, wu_ref[0, :, lo:lo + MOE_TF], preferred_element_type=F32)
            act = (gate * jax.nn.sigmoid(gate) * up * g).astype(BF16)
            part = jnp.dot(act, wd_ref[0, lo:lo + MOE_TF, :], preferred_element_type=F32)
            if c == 0:
                acc_ref[...] = part
            else:
                acc_ref[...] += part
        o_ref[...] = acc_ref[...].astype(o_ref.dtype)

    @pl.when(b >= nu_ref[0])
    def _():
        o_ref[...] = jnp.zeros_like(o_ref)


def _combine_kernel(base_ref, n_ref, x_ref, st_ref, ys_ref, gout_ref, o_ref, ybuf, ybuf2, sem, sem2,
                    *, out_norm):
    i = pl.program_id(0)
    slot = i % 2

    def copy(step, e):
        row = pl.multiple_of(base_ref[step * N_EXPERTS + e], MOE_ALIGN)
        return pltpu.make_async_copy(ys_ref.at[pl.ds(row, MOE_HALF)],
                                     ybuf.at[step % 2, pl.ds(e * MOE_HALF, MOE_HALF)],
                                     sem.at[step % 2, e])

    def copy2(e):
        row = pl.multiple_of(base_ref[i * N_EXPERTS + e] + MOE_HALF, MOE_ALIGN)
        return pltpu.make_async_copy(ys_ref.at[pl.ds(row, MOE_HALF)], ybuf2.at[e], sem2.at[e])

    def fetch(step):
        for e in range(N_EXPERTS):
            copy(step, e).start()

    pl.when(i == 0)(lambda: fetch(i))
    pl.when(i + 1 < pl.num_programs(0))(lambda: fetch(i + 1))
    for e in range(N_EXPERTS):
        pl.when(n_ref[i * N_EXPERTS + e] > MOE_HALF)(lambda e=e: copy2(e).start())

    st = st_ref[...]
    rank = _tile_rank(st)
    p = _one_hot_all(rank, st)
    for e in range(N_EXPERTS):
        copy(i, e).wait()
    tn_dims = (((0,), (0,)), ((), ()))
    o_ref[...] = x_ref[...] + lax.dot_general(p, ybuf[slot], tn_dims, preferred_element_type=F32)

    def gather2(e):
        copy2(e).wait()
        p2 = _one_hot_rows(rank, st, e, 1)
        o_ref[...] += lax.dot_general(p2, ybuf2[e], tn_dims, preferred_element_type=F32)

    for e in range(N_EXPERTS):
        pl.when(n_ref[i * N_EXPERTS + e] > MOE_HALF)(functools.partial(gather2, e))
    if out_norm:
        o_ref[...] = _rms(o_ref[...], gout_ref[...])


def _moe_layer(x, g_mem, w_q, kv_all, layer, w_o, g_ffn, w_router, w_gu, w_down, moe_layer, bsz, seq,
               g_out=None):
    m, d = x.shape
    ff = w_down.shape[2]
    nt = m // MOE_TM
    ntb = seq // MOE_TM
    i32 = jnp.int32
    x, hn, ct, st, cnt = pl.pallas_call(
        _mem_route_kernel,
        grid=(bsz, ntb),
        in_specs=_mem_specs(d, ntb, MOE_TM, layer) + [
            pl.BlockSpec((1, d), lambda b, i: (0, 0)),
            pl.BlockSpec((N_EXPERTS, d), lambda b, i: (0, 0)),
        ],
        out_specs=[
            pl.BlockSpec((MOE_TM, d), lambda b, i: (b * ntb + i, 0)),
            pl.BlockSpec((MOE_TM, d), lambda b, i: (b * ntb + i, 0)),
            pl.BlockSpec((N_EXPERTS, MOE_TM), lambda b, i: (0, b * ntb + i)),
            pl.BlockSpec((N_EXPERTS, MOE_TM), lambda b, i: (0, b * ntb + i)),
            pl.BlockSpec((1, N_EXPERTS, LANES), lambda b, i: (b * ntb + i, 0, 0)),
        ],
        out_shape=[
            jax.ShapeDtypeStruct((m, d), F32),
            jax.ShapeDtypeStruct((m, d), BF16),
            jax.ShapeDtypeStruct((N_EXPERTS, m), F32),
            jax.ShapeDtypeStruct((N_EXPERTS, m), F32),
            jax.ShapeDtypeStruct((nt, N_EXPERTS, LANES), F32),
        ],
        scratch_shapes=[pltpu.VMEM((MOE_TM, d), BF16)],
        compiler_params=_params("parallel", "parallel"),
        name="mem_route",
    )(x, g_mem.reshape(1, d), w_q, kv_all, kv_all, w_o, g_ffn.reshape(1, d), w_router.T.astype(F32))

    n = cnt[:, :, 0].astype(i32)
    chunk = (n + MOE_ALIGN - 1) // MOE_ALIGN * MOE_ALIGN
    seg = (jnp.sum(chunk, axis=0) + MOE_HALF + MOE_TR - 1) // MOE_TR * MOE_TR
    seg_off = jnp.cumsum(seg) - seg
    base = (seg_off[None, :] + jnp.cumsum(chunk, axis=0) - chunk).reshape(-1).astype(i32)
    n_flat = n.reshape(-1)
    bound = 2 * m + nt * N_EXPERTS * (MOE_ALIGN - 1) + N_EXPERTS * (MOE_HALF + MOE_TR - 1)
    nb = -(-bound // MOE_TR)
    blk_end = jnp.cumsum(seg // MOE_TR)
    blk_expert = jnp.minimum(jnp.sum(jnp.arange(nb, dtype=i32)[:, None] >= blk_end[None, :], axis=1),
                             N_EXPERTS - 1).astype(i32)
    n_used = blk_end[-1:].astype(i32)
    rows = nb * MOE_TR
    tail = jnp.concatenate([jnp.maximum(seg_off + seg - MOE_ZR, 0), n_used]).astype(i32)

    tile_spec = lambda shape, imap: pl.BlockSpec(shape, imap)
    xs = pl.pallas_call(
        _dispatch_kernel,
        grid_spec=pltpu.PrefetchScalarGridSpec(
            num_scalar_prefetch=3,
            grid=(nt,),
            in_specs=[
                tile_spec((MOE_TM, d), lambda i, b, c, t: (i, 0)),
                tile_spec((N_EXPERTS, MOE_TM), lambda i, b, c, t: (0, i)),
                tile_spec((N_EXPERTS, MOE_TM), lambda i, b, c, t: (0, i)),
            ],
            out_specs=pl.BlockSpec(memory_space=pl.ANY),
            scratch_shapes=[
                pltpu.VMEM((2, N_EXPERTS * MOE_HALF, XS_W), BF16),
                pltpu.VMEM((N_EXPERTS, MOE_HALF, XS_W), BF16),
                pltpu.VMEM((MOE_ZR, XS_W), BF16),
                pltpu.SemaphoreType.DMA((2, N_EXPERTS)),
                pltpu.SemaphoreType.DMA((N_EXPERTS,)),
                pltpu.SemaphoreType.DMA(()),
            ],
        ),
        out_shape=jax.ShapeDtypeStruct((rows, XS_W), BF16),
        compiler_params=_params("arbitrary"),
        name="moe_dispatch",
    )(base, n_flat, tail, hn, ct, st)

    ys = pl.pallas_call(
        _expert_kernel,
        grid_spec=pltpu.PrefetchScalarGridSpec(
            num_scalar_prefetch=2,
            grid=(nb,),
            in_specs=[
                pl.BlockSpec((MOE_TR, XS_W), lambda b, be, nu: (b, 0)),
                pl.BlockSpec((None, 1, d, ff), lambda b, be, nu: (moe_layer, be[b], 0, 0)),
                pl.BlockSpec((None, 1, d, ff), lambda b, be, nu: (moe_layer, be[b], 0, 1)),
                pl.BlockSpec((None, 1, ff, d), lambda b, be, nu: (moe_layer, be[b], 0, 0)),
            ],
            out_specs=pl.BlockSpec((MOE_TR, d), lambda b, be, nu: (b, 0)),
            scratch_shapes=[pltpu.VMEM((MOE_TR, d), F32)],
        ),
        out_shape=jax.ShapeDtypeStruct((rows, d), BF16),
        compiler_params=_params("arbitrary"),
        name="moe_experts",
    )(blk_expert, n_used, xs, w_gu, w_gu, w_down)

    out_norm = g_out is not None
    g_out = jnp.ones((d,), F32) if g_out is None else g_out
    return pl.pallas_call(
        functools.partial(_combine_kernel, out_norm=out_norm),
        grid_spec=pltpu.PrefetchScalarGridSpec(
            num_scalar_prefetch=2,
            grid=(nt,),
            in_specs=[
                tile_spec((MOE_TM, d), lambda i, b, c: (i, 0)),
                tile_spec((N_EXPERTS, MOE_TM), lambda i, b, c: (0, i)),
                pl.BlockSpec(memory_space=pl.ANY),
                tile_spec((1, d), lambda i, b, c: (0, 0)),
            ],
            out_specs=tile_spec((MOE_TM, d), lambda i, b, c: (i, 0)),
            scratch_shapes=[
                pltpu.VMEM((2, N_EXPERTS * MOE_HALF, d), BF16),
                pltpu.VMEM((N_EXPERTS, MOE_HALF, d), BF16),
                pltpu.SemaphoreType.DMA((2, N_EXPERTS)),
                pltpu.SemaphoreType.DMA((N_EXPERTS,)),
            ],
        ),
        out_shape=jax.ShapeDtypeStruct((m, d), F32),
        compiler_params=_params("arbitrary"),
        name="moe_combine",
    )(base, n_flat, x, st, ys, g_out.reshape(1, d).astype(F32))


def kernel(x, mem, norm_mix, norm_mem, norm_ffn, mem_norm, final_norm, s5_w_in, s5_lam_re, s5_lam_im, s5_log_dt, s5_b_re, s5_b_im, s5_c_re, s5_c_im, s5_d, s5_w_glu, att_w_qkv, att_w_o, att_rel_bias, lru_w_in, lru_conv_w, lru_conv_b, lru_w_a, lru_b_a, lru_w_x, lru_b_x, lru_lam, lru_w_out, mem_w_q, mem_w_kv, mem_w_o, ffn_w_gu, ffn_w_down, moe_w_router, moe_w_gu, moe_w_down):
    bsz, seq, d = x.shape
    x = x.reshape(bsz * seq, d).astype(F32)

    w_kv_all = jnp.concatenate([mem_w_kv[i] for i in range(DEPTH)], axis=1).astype(BF16)
    kv_all = _norm_linear(mem.reshape(bsz * N_MEM, d).astype(F32), mem_norm, w_kv_all, BF16,
                          tm=bsz * N_MEM, tn=2 * d)

    mem_wq, mem_wo = mem_w_q.astype(BF16), mem_w_o.astype(BF16)
    ffn_wgu, ffn_wd = ffn_w_gu.astype(BF16), ffn_w_down.astype(BF16)
    moe_wgu, moe_wd = moe_w_gu.astype(BF16), moe_w_down.astype(BF16)
    for i in range(DEPTH):
        kind, j = i % 3, i // 3
        if kind == 0:
            x = _s5_layer(x, norm_mix[i], s5_w_in[j], s5_lam_re[j], s5_lam_im[j], s5_log_dt[j],
                          s5_b_re[j], s5_b_im[j], s5_c_re[j], s5_c_im[j], s5_d[j], s5_w_glu[j],
                          bsz, seq)
        elif kind == 1:
            x = _attn_layer(x, norm_mix[i], att_w_qkv[j], att_w_o[j], att_rel_bias[j], bsz, seq)
        else:
            x = _lru_layer(x, norm_mix[i], lru_w_in[j], lru_conv_w[j], lru_conv_b[j], lru_w_a[j],
                           lru_b_a[j], lru_w_x[j], lru_b_x[j], lru_lam[j], lru_w_out[j], bsz, seq)
        if i % 2 == 0:
            x = _mem_ffn_layer(x, norm_mem[i], mem_wq, kv_all, i, mem_wo, norm_ffn[i], ffn_wgu, ffn_wd,
                               i // 2, bsz, seq)
        else:
            x = _moe_layer(x, norm_mem[i], mem_wq, kv_all, i, mem_wo, norm_ffn[i], moe_w_router[i // 2],
                           moe_wgu, moe_wd, i // 2, bsz, seq,
                           g_out=final_norm if i == DEPTH - 1 else None)
    return x.reshape(bsz, seq, d)
```

```python
import functools
import math

import jax
import jax.numpy as jnp
from jax import lax
from jax.experimental import pallas as pl
from jax.experimental.pallas import tpu as pltpu

F32 = jnp.float32
BF16 = jnp.bfloat16
HIGHEST = lax.Precision.HIGHEST

D_MODEL = 1024
DEPTH = 4
CHUNK = 64
N_MEM = 256
EPS = 1e-6

S5_GROUP = 16
S5_GROUPS = D_MODEL // S5_GROUP
S5_STATE = 64

ATT_HEADS = 16
ATT_HEAD_DIM = D_MODEL // ATT_HEADS
LEFT_CHUNKS = 8
MAX_REL = 128
ATT_TQ = 4 * CHUNK
ATT_LEFT = LEFT_CHUNKS * CHUNK
ATT_TK = ATT_LEFT + ATT_TQ
NEG_INF = -1e30
LOG2E = math.log2(math.e)

LRU_BLOCKS = 8
LRU_BLOCK_W = D_MODEL // LRU_BLOCKS
CONV_W = 4
LRU_C = 8.0
LRU_T = 512
SUBLANES = 8
LANES = 128

MEM_HEADS = 4
MEM_HEAD_DIM = D_MODEL // MEM_HEADS

N_EXPERTS = 8
FFN_TF = 256

VMEM_LIMIT = 56 * 1024 * 1024


def _params(*sem):
    return pltpu.CompilerParams(dimension_semantics=sem, vmem_limit_bytes=VMEM_LIMIT)


def _rms(x, g):
    ms = jnp.mean(x * x, axis=-1, keepdims=True)
    return x * lax.rsqrt(ms + EPS) * g


def _norm_linear_kernel(x_ref, g_ref, w_ref, o_ref, hn_ref):
    @pl.when(pl.program_id(1) == 0)
    def _():
        hn_ref[...] = _rms(x_ref[...], g_ref[...]).astype(BF16)

    o_ref[...] = jnp.dot(hn_ref[...], w_ref[...], preferred_element_type=F32).astype(o_ref.dtype)


def _norm_linear(x, g, w, out_dtype, tm=512, tn=None):
    m, d = x.shape
    n = w.shape[1]
    tn = n if tn is None else tn
    return pl.pallas_call(
        _norm_linear_kernel,
        grid=(m // tm, n // tn),
        in_specs=[
            pl.BlockSpec((tm, d), lambda i, j: (i, 0)),
            pl.BlockSpec((1, d), lambda i, j: (0, 0)),
            pl.BlockSpec((d, tn), lambda i, j: (0, j)),
        ],
        out_specs=pl.BlockSpec((tm, tn), lambda i, j: (i, j)),
        out_shape=jax.ShapeDtypeStruct((m, n), out_dtype),
        scratch_shapes=[pltpu.VMEM((tm, d), BF16)],
        compiler_params=_params("parallel", "arbitrary"),
        name="norm_linear",
    )(x, g.reshape(1, d), w)


def _s5_operators(lam_re, lam_im, log_dt, b_re, b_im, c_re, c_im, d_skip, n_chunks):
    g, n, p = S5_GROUPS, S5_STATE, S5_GROUP
    lr = lam_re.astype(F32)
    li = lam_im.astype(F32)
    dt = jnp.exp(log_dt.astype(F32))[:, None]
    mag = jnp.exp(lr * dt)
    ar = mag * jnp.cos(li * dt)
    ai = mag * jnp.sin(li * dt)
    den = lr * lr + li * li
    fr = ((ar - 1.0) * lr + ai * li) / den
    fi = (ai * lr - (ar - 1.0) * li) / den
    bbr = fr[..., None] * b_re - fi[..., None] * b_im
    bbi = fr[..., None] * b_im + fi[..., None] * b_re
    bbt = jnp.concatenate([bbr, bbi], axis=1).transpose(0, 2, 1)

    kk = jnp.arange(CHUNK + 1, dtype=F32)
    mag_k = jnp.exp((lr * dt)[..., None] * kk)
    pr = mag_k * jnp.cos((li * dt)[..., None] * kk)
    pi = mag_k * jnp.sin((li * dt)[..., None] * kk)
    pk = jnp.concatenate([pr[..., :CHUNK], pi[..., :CHUNK], pr[..., 1:], pi[..., 1:],
                          pr[..., CHUNK - 1::-1], pi[..., CHUNK - 1::-1]], axis=-1)
    cb = jnp.concatenate([c_re.transpose(0, 2, 1), c_im.transpose(0, 2, 1), bbr, bbi], axis=-1)
    dq = jnp.broadcast_to(d_skip.reshape(g, p, 1), (g, p, LANES))
    qr, qi = [pr[..., CHUNK]], [pi[..., CHUNK]]
    for _ in range(int(math.log2(n_chunks)) - 1):
        qr, qi = qr + [qr[-1] * qr[-1] - qi[-1] * qi[-1]], qi + [2.0 * qr[-1] * qi[-1]]
    levels = len(qr)
    qr, qi = jnp.stack(qr), jnp.stack(qi)
    m1 = jnp.concatenate([qr, qr], axis=-1)
    m2 = jnp.concatenate([-qi, qi], axis=-1)
    ap = jnp.stack([m1, m2], axis=-1).transpose(1, 2, 0, 3).reshape(g, 2 * n, 2 * levels)
    return bbt, pk, cb, dq, ap


def _s5_in_kernel(x3_ref, g_ref, wt_ref, u3_ref, xbuf, ubuf, sem_in, sem_out):
    s = pl.program_id(0)
    last = pl.num_programs(0) - 1
    slot = s % 2

    def load(step):
        return pltpu.make_async_copy(x3_ref.at[pl.ds(0, x3_ref.shape[0]), step], xbuf.at[step % 2],
                                     sem_in.at[step % 2])

    def store(step):
        return pltpu.make_async_copy(ubuf.at[step % 2], u3_ref.at[pl.ds(0, u3_ref.shape[0]), step],
                                     sem_out.at[step % 2])

    pl.when(s == 0)(lambda: load(s).start())
    pl.when(s < last)(lambda: load(s + 1).start())
    load(s).wait()
    pl.when(s >= 2)(lambda: store(s - 2).wait())
    hn = _rms(xbuf[slot], g_ref[...]).astype(BF16)
    ubuf[slot] = lax.dot_general(wt_ref[...], hn, (((1,), (1,)), ((), ())), preferred_element_type=F32)
    store(s).start()

    @pl.when(s == last)
    def _():
        store(s - 1).wait()
        store(s).wait()


def _tile_lanes(a, times):
    a2 = jnp.concatenate([a, a], axis=1)
    return jnp.concatenate([a2] * (times // 2), axis=1)


def _spread_lanes(a):
    r, c = a.shape
    low = lax.broadcasted_iota(jnp.int32, (r, LANES), 1) < CHUNK
    pairs = [jnp.where(low, jnp.broadcast_to(a[:, 2 * j:2 * j + 1], (r, LANES)),
                       jnp.broadcast_to(a[:, 2 * j + 1:2 * j + 2], (r, LANES))) for j in range(c // 2)]
    return jnp.concatenate(pairs, axis=1)


def _ssm_kernel(u_ref, bbt_ref, pk_ref, cb_ref, dq_ref, ap_ref, o_ref, toet_ref, *, n_chunks):
    n = S5_STATE
    rows = S5_GROUP * CHUNK
    ncol = u_ref.shape[-1]
    u = u_ref[0].reshape(rows, ncol).astype(BF16)

    pk = pk_ref[0]
    cb = cb_ref[0]
    cr, ci, br, bi = [_spread_lanes(cb[:, S5_GROUP * k:S5_GROUP * (k + 1)]) for k in range(4)]
    p0r, p0i, p1r, p1i, rr, ri = [_tile_lanes(pk[:, CHUNK * k:CHUNK * (k + 1)], S5_GROUP) for k in range(6)]
    lk = jnp.concatenate([cr * p0r - ci * p0i, -(cr * p0i + ci * p0r)], axis=0)
    cinjt = jnp.concatenate([cr * p1r - ci * p1i, -(cr * p1i + ci * p1r)], axis=0).astype(BF16)
    bm = jnp.concatenate([rr * br - ri * bi, rr * bi + ri * br], axis=0).astype(BF16)

    kern = jnp.dot(bbt_ref[0], lk, preferred_element_type=F32, precision=HIGHEST)
    lane = lax.broadcasted_iota(jnp.int32, (S5_GROUP, rows), 1)
    own_lag0 = lane == lax.broadcasted_iota(jnp.int32, (S5_GROUP, rows), 0) * CHUNK
    kern = kern + jnp.where(own_lag0, jnp.concatenate([dq_ref[0]] * (rows // LANES), axis=1), 0.0)
    causal = (lax.broadcasted_iota(jnp.int32, (CHUNK, rows), 1) % CHUNK
              >= lax.broadcasted_iota(jnp.int32, (CHUNK, rows), 0))
    for p in range(S5_GROUP):
        base = jnp.broadcast_to(kern[p:p + 1, :], (CHUNK, rows))
        rolled = pltpu.roll(base, 0, 1, stride=1, stride_axis=0)
        toet_ref[p * CHUNK:(p + 1) * CHUNK, :] = jnp.where(causal, rolled, 0.0).astype(BF16)
    tn_dims = (((0,), (0,)), ((), ()))
    y = lax.dot_general(toet_ref[...], u, tn_dims, preferred_element_type=F32)
    x = jnp.dot(bm, u, preferred_element_type=F32)
    ap = ap_ref[0]
    col = lax.broadcasted_iota(jnp.int32, x.shape, 1) % n_chunks
    for k in range(int(math.log2(n_chunks))):
        sh = 1 << k
        xs = jnp.where(col >= sh, pltpu.roll(x, sh, 1), 0.0)
        xsw = jnp.concatenate([xs[n:], xs[:n]], axis=0)
        x = x + ap[:, 2 * k:2 * k + 1] * xs + ap[:, 2 * k + 1:2 * k + 2] * xsw
    h0 = jnp.where(col >= 1, pltpu.roll(x, 1, 1), 0.0)
    h0_hi = h0.astype(BF16)
    h0_lo = (h0 - h0_hi.astype(F32)).astype(BF16)
    y = y + lax.dot_general(cinjt, h0_hi, tn_dims, preferred_element_type=F32)
    y = y + lax.dot_general(cinjt, h0_lo, tn_dims, preferred_element_type=F32)
    o_ref[0] = jax.nn.gelu(y).reshape(S5_GROUP, CHUNK, ncol)


def _s5_out_kernel(v3_ref, x3_ref, w_ref, o3_ref, vbuf, xbuf, obuf, sem_v, sem_x, sem_o):
    s = pl.program_id(0)
    last = pl.num_programs(0) - 1
    slot = s % 2

    def load_v(step):
        return pltpu.make_async_copy(v3_ref.at[pl.ds(0, v3_ref.shape[0]), step], vbuf.at[step % 2],
                                     sem_v.at[step % 2])

    def load_x(step):
        return pltpu.make_async_copy(x3_ref.at[pl.ds(0, x3_ref.shape[0]), step], xbuf.at[step % 2],
                                     sem_x.at[step % 2])

    def store(step):
        return pltpu.make_async_copy(obuf.at[step % 2], o3_ref.at[pl.ds(0, o3_ref.shape[0]), step],
                                     sem_o.at[step % 2])

    def load(step):
        load_v(step).start()
        load_x(step).start()

    pl.when(s == 0)(lambda: load(s))
    pl.when(s < last)(lambda: load(s + 1))
    load_v(s).wait()
    load_x(s).wait()
    pl.when(s >= 2)(lambda: store(s - 2).wait())
    z = lax.dot_general(vbuf[slot].astype(BF16), w_ref[...], (((0,), (0,)), ((), ())),
                        preferred_element_type=F32)
    d = xbuf.shape[-1]
    obuf[slot] = xbuf[slot] + z[:, :d] * jax.nn.sigmoid(z[:, d:])
    store(s).start()

    @pl.when(s == last)
    def _():
        store(s - 1).wait()
        store(s).wait()


def _s5_layer(x, g_norm, w_in, lam_re, lam_im, log_dt, b_re, b_im, c_re, c_im, d_skip, w_glu,
              bsz, seq):
    m, d = x.shape
    n_chunks = seq // CHUNK
    ncol = bsz * n_chunks
    rows = S5_GROUP * CHUNK
    bbt, pk, cb, dq, ap = _s5_operators(lam_re, lam_im, log_dt, b_re, b_im, c_re, c_im, d_skip,
                                        n_chunks)
    x3 = x.reshape(ncol, CHUNK, d)
    hbm = pl.BlockSpec(memory_space=pl.ANY)
    u3 = pl.pallas_call(
        _s5_in_kernel,
        grid=(CHUNK,),
        in_specs=[hbm, pl.BlockSpec((1, d), lambda s: (0, 0)), pl.BlockSpec((d, d), lambda s: (0, 0))],
        out_specs=hbm,
        out_shape=jax.ShapeDtypeStruct((d, CHUNK, ncol), F32),
        scratch_shapes=[
            pltpu.VMEM((2, ncol, d), F32),
            pltpu.VMEM((2, d, ncol), F32),
            pltpu.SemaphoreType.DMA((2,)),
            pltpu.SemaphoreType.DMA((2,)),
        ],
        compiler_params=_params("arbitrary"),
        name="s5_in",
    )(x3, g_norm.reshape(1, d), w_in.T.astype(BF16))
    blk4 = (1, S5_GROUP, CHUNK, ncol)
    per_group = lambda a: pl.BlockSpec((1,) + a.shape[1:], lambda g: (g, 0, 0))
    v4 = pl.pallas_call(
        functools.partial(_ssm_kernel, n_chunks=n_chunks),
        grid=(S5_GROUPS,),
        in_specs=[pl.BlockSpec(blk4, lambda g: (g, 0, 0, 0))]
        + [per_group(a) for a in (bbt, pk, cb, dq, ap)],
        out_specs=pl.BlockSpec(blk4, lambda g: (g, 0, 0, 0)),
        out_shape=jax.ShapeDtypeStruct((S5_GROUPS, S5_GROUP, CHUNK, ncol), F32),
        scratch_shapes=[pltpu.VMEM((rows, rows), BF16)],
        compiler_params=_params("parallel"),
        name="s5_ssm",
    )(u3.reshape(S5_GROUPS, S5_GROUP, CHUNK, ncol), bbt, pk, cb, dq, ap)
    o3 = pl.pallas_call(
        _s5_out_kernel,
        grid=(CHUNK,),
        in_specs=[hbm, hbm, pl.BlockSpec((d, 2 * d), lambda s: (0, 0))],
        out_specs=hbm,
        out_shape=jax.ShapeDtypeStruct((ncol, CHUNK, d), F32),
        scratch_shapes=[
            pltpu.VMEM((2, d, ncol), F32),
            pltpu.VMEM((2, ncol, d), F32),
            pltpu.VMEM((2, ncol, d), F32),
            pltpu.SemaphoreType.DMA((2,)),
            pltpu.SemaphoreType.DMA((2,)),
            pltpu.SemaphoreType.DMA((2,)),
        ],
        compiler_params=_params("arbitrary"),
        name="s5_out",
    )(v4.reshape(d, CHUNK, ncol), x3, w_glu.astype(BF16))
    return o3.reshape(m, d)


ATT_PERIOD = ATT_TQ + ATT_TK


def _attn_bias_seed(rel_bias):
    k = jnp.arange(ATT_PERIOD)
    rel = jnp.where(k < ATT_TK, ATT_LEFT - k, ATT_LEFT + ATT_PERIOD - k)
    return jnp.take(rel_bias.astype(F32), jnp.clip(rel, -MAX_REL, MAX_REL) + MAX_REL, axis=1)


def _attn_kernel(x_ref, q_ref, k0_ref, k1_ref, k2_ref, v0_ref, v1_ref, v2_ref, seed_ref, wo_ref,
                 o_ref, bias_ref, ocat_ref):
    i = pl.program_id(1)

    @pl.when((pl.program_id(0) == 0) & (i == 0))
    def _():
        r = lax.broadcasted_iota(jnp.int32, (ATT_TQ, ATT_TK), 0)
        j = lax.broadcasted_iota(jnp.int32, (ATT_TQ, ATT_TK), 1)
        dchunk = (r + ATT_LEFT) // CHUNK - j // CHUNK
        valid = (dchunk >= 0) & (dchunk <= LEFT_CHUNKS)
        for h in range(ATT_HEADS):
            seed = jnp.broadcast_to(seed_ref[h:h + 1, :], (ATT_TQ, ATT_PERIOD))
            rolled = pltpu.roll(seed, 0, 1, stride=1, stride_axis=0)
            bias_ref[h] = jnp.where(valid, rolled[:, :ATT_TK] * LOG2E, NEG_INF)

    lane = lax.broadcasted_iota(jnp.int32, (1, LANES), 1)
    col = lax.broadcasted_iota(jnp.int32, (ATT_TQ, ATT_TK), 1)
    pad = col < ATT_LEFT - i * ATT_TQ
    scale = ATT_HEAD_DIM ** -0.5
    heads_per = LANES // ATT_HEAD_DIM
    for hp in range(D_MODEL // LANES):
        sl = slice(hp * LANES, (hp + 1) * LANES)
        q = q_ref[:, sl] * scale
        k = jnp.concatenate([k0_ref[:, sl], k1_ref[:, sl], k2_ref[:, sl]], axis=0)
        v = jnp.concatenate([v0_ref[:, sl], v1_ref[:, sl], v2_ref[:, sl]], axis=0)
        acc = jnp.zeros((ATT_TQ, LANES), F32)
        for h in range(heads_per):
            in_head = (lane >= h * ATT_HEAD_DIM) & (lane < (h + 1) * ATT_HEAD_DIM)
            qh = jnp.where(in_head, q, jnp.zeros_like(q))
            s = lax.dot_general(qh, k, (((1,), (1,)), ((), ())), preferred_element_type=F32)
            s = jnp.where(pad, NEG_INF, s + bias_ref[hp * heads_per + h])
            p = jnp.exp2(s - jnp.max(s, axis=-1, keepdims=True))
            vh = jnp.where(in_head, v, jnp.ones_like(v))
            pv = jnp.dot(p.astype(BF16), vh, preferred_element_type=F32)
            acc = jnp.where(in_head, pv / pltpu.roll(pv, ATT_HEAD_DIM, 1), acc)
        ocat_ref[:, sl] = acc.astype(BF16)
    o_ref[...] = x_ref[...] + jnp.dot(ocat_ref[...], wo_ref[...], preferred_element_type=F32)


def _attn_layer(x, g_norm, w_qkv, w_o, rel_bias, bsz, seq):
    m, d = x.shape
    w_qkv = jnp.concatenate([w_qkv[:, :d] * LOG2E, w_qkv[:, d:]], axis=1)
    qkv = _norm_linear(x, g_norm, w_qkv.astype(BF16), BF16, tm=1024)
    nq = seq // ATT_TQ

    def rows(col, back):
        return lambda b, i: (b * nq + jnp.maximum(i - back, 0), col)

    blk = (ATT_TQ, d)
    nkb = ATT_TK // ATT_TQ
    return pl.pallas_call(
        _attn_kernel,
        grid=(bsz, nq),
        in_specs=(
            [pl.BlockSpec(blk, rows(0, 0)), pl.BlockSpec(blk, rows(0, 0))]
            + [pl.BlockSpec(blk, rows(1, nkb - 1 - kb)) for kb in range(nkb)]
            + [pl.BlockSpec(blk, rows(2, nkb - 1 - kb)) for kb in range(nkb)]
            + [pl.BlockSpec((ATT_HEADS, ATT_PERIOD), lambda b, i: (0, 0)),
               pl.BlockSpec((d, d), lambda b, i: (0, 0))]
        ),
        out_specs=pl.BlockSpec(blk, rows(0, 0)),
        out_shape=jax.ShapeDtypeStruct((m, d), F32),
        scratch_shapes=[pltpu.VMEM((ATT_HEADS, ATT_TQ, ATT_TK), F32), pltpu.VMEM((ATT_TQ, d), BF16)],
        compiler_params=_params("arbitrary", "arbitrary"),
        name="chunk_attn",
    )(x, *([qkv] * (1 + 2 * nkb)), _attn_bias_seed(rel_bias), w_o.astype(BF16))


def _lru_kernel(x_ref, g_ref, win_ref, cw_ref, cb_ref, wax_ref, ba_ref, bx_ref, lam_ref, wout_ref,
                o_ref, prev_ref, carry_ref, a_s, b_s):
    t, w = x_ref.shape

    @pl.when(pl.program_id(1) == 0)
    def _():
        prev_ref[...] = jnp.zeros_like(prev_ref)
        carry_ref[...] = jnp.zeros_like(carry_ref)

    x = x_ref[...]
    z = jnp.dot(_rms(x, g_ref[...]).astype(BF16), win_ref[...], preferred_element_type=F32)
    gate = z[:, :w]
    xr = z[:, w:]
    xext = jnp.concatenate([prev_ref[...], xr], axis=0)
    cw = cw_ref[...]
    xc = cb_ref[...] + cw[0:1] * xext[SUBLANES - 3:SUBLANES - 3 + t]
    for k in range(1, CONV_W):
        xc = xc + cw[k:k + 1] * xext[SUBLANES - 3 + k:SUBLANES - 3 + k + t]
    prev_ref[...] = xr[t - SUBLANES:]

    xcb = xc.astype(BF16)
    pre_a, pre_x = [], []
    for blk in range(LRU_BLOCKS):
        pre = jnp.dot(xcb[:, blk * LRU_BLOCK_W:(blk + 1) * LRU_BLOCK_W], wax_ref[blk],
                      preferred_element_type=F32)
        pre_a.append(pre[:, :LRU_BLOCK_W])
        pre_x.append(pre[:, LRU_BLOCK_W:])
    r = jax.nn.sigmoid(jnp.concatenate(pre_a, axis=1) + ba_ref[...])
    ig = jax.nn.sigmoid(jnp.concatenate(pre_x, axis=1) + bx_ref[...])
    z = -lam_ref[...]
    softplus = jnp.maximum(z, 0.0) + jnp.log1p(jnp.exp(-jnp.abs(z)))
    log_a = -LRU_C * r * softplus
    a = jnp.exp(log_a)
    b = jnp.sqrt(1.0 - a * a) * (ig * xc)

    a3 = a.reshape(t // SUBLANES, SUBLANES, w)
    b3 = b.reshape(t // SUBLANES, SUBLANES, w)
    sub = lax.broadcasted_iota(jnp.int32, a3.shape, 1)
    for sh in (1, 2, 4):
        a_sh = jnp.where(sub >= sh, pltpu.roll(a3, sh, 1), 1.0)
        b_sh = jnp.where(sub >= sh, pltpu.roll(b3, sh, 1), 0.0)
        b3 = a3 * b_sh + b3
        a3 = a3 * a_sh
    a_s[...] = a3
    b_s[...] = b3

    def body(j, carry):
        hb = b_s[j] + a_s[j] * carry
        b_s[j] = hb
        return hb[SUBLANES - 1:SUBLANES, :]

    carry = lax.fori_loop(0, t // SUBLANES, body, carry_ref[0:1, :])
    carry_ref[0:1, :] = carry
    h = b_s[...].reshape(t, w)
    y = (jax.nn.gelu(gate) * h).astype(BF16)
    o_ref[...] = x + jnp.dot(y, wout_ref[...], preferred_element_type=F32)


def _lru_layer(x, g_norm, w_in, conv_w, conv_b, w_a, b_a, w_x, b_x, lam, w_out, bsz, seq):
    m, d = x.shape
    wax = jnp.concatenate([w_a, w_x], axis=-1).astype(BF16)
    nt = seq // LRU_T
    row = lambda v: v.reshape(1, d).astype(F32)
    vec = pl.BlockSpec((1, d), lambda b, i: (0, 0))
    return pl.pallas_call(
        _lru_kernel,
        grid=(bsz, nt),
        in_specs=[
            pl.BlockSpec((LRU_T, d), lambda b, i: (b * nt + i, 0)),
            vec,
            pl.BlockSpec((d, 2 * d), lambda b, i: (0, 0)),
            pl.BlockSpec((CONV_W, d), lambda b, i: (0, 0)),
            vec,
            pl.BlockSpec(wax.shape, lambda b, i: (0, 0, 0)),
            vec, vec, vec,
            pl.BlockSpec((d, d), lambda b, i: (0, 0)),
        ],
        out_specs=pl.BlockSpec((LRU_T, d), lambda b, i: (b * nt + i, 0)),
        out_shape=jax.ShapeDtypeStruct((m, d), F32),
        scratch_shapes=[
            pltpu.VMEM((SUBLANES, d), F32),
            pltpu.VMEM((SUBLANES, d), F32),
            pltpu.VMEM((LRU_T // SUBLANES, SUBLANES, d), F32),
            pltpu.VMEM((LRU_T // SUBLANES, SUBLANES, d), F32),
        ],
        compiler_params=_params("parallel", "arbitrary"),
        name="rglru",
    )(x, row(g_norm), w_in.astype(BF16), conv_w.astype(F32), row(conv_b), wax, row(b_a), row(b_x),
      row(lam), w_out.astype(BF16))


def _memattn(x, g_ref, wq_ref, k_ref, v_ref, wo_ref, ocat_ref):
    hn = _rms(x, g_ref[...]).astype(BF16)
    q = jnp.dot(hn, wq_ref[...], preferred_element_type=F32) * (MEM_HEAD_DIM ** -0.5)
    q = q.astype(BF16)
    for h in range(MEM_HEADS):
        sl = slice(h * MEM_HEAD_DIM, (h + 1) * MEM_HEAD_DIM)
        s = lax.dot_general(q[:, sl], k_ref[:, sl], (((1,), (1,)), ((), ())),
                            preferred_element_type=F32)
        mx = jnp.max(s, axis=-1, keepdims=True)
        p = jnp.exp(s - mx)
        l = jnp.sum(p, axis=-1, keepdims=True)
        o = jnp.dot(p.astype(BF16), v_ref[:, sl], preferred_element_type=F32) / l
        ocat_ref[:, sl] = o.astype(BF16)
    return x + jnp.dot(ocat_ref[...], wo_ref[...], preferred_element_type=F32)


def _mem_ffn_kernel(x_ref, g_ref, wq_ref, k_ref, v_ref, wo_ref, g2_ref, wgu_ref, wd_ref, o_ref, ocat_ref):
    ff = wd_ref.shape[0]
    x1 = _memattn(x_ref[...], g_ref, wq_ref, k_ref, v_ref, wo_ref, ocat_ref)
    hn = _rms(x1, g2_ref[...]).astype(BF16)
    o_ref[...] = x1
    for f in range(ff // FFN_TF):
        lo = f * FFN_TF
        gate = jnp.dot(hn, wgu_ref[:, lo:lo + FFN_TF], preferred_element_type=F32)
        up = jnp.dot(hn, wgu_ref[:, ff + lo:ff + lo + FFN_TF], preferred_element_type=F32)
        act = gate * jax.nn.sigmoid(gate) * up
        o_ref[...] += jnp.dot(act.astype(BF16), wd_ref[lo:lo + FFN_TF, :], preferred_element_type=F32)


def _mem_route_kernel(x_ref, g_ref, wq_ref, k_ref, v_ref, wo_ref, g2_ref, wr_ref,
                      o_ref, hn_ref, ct_ref, st_ref, cnt_ref, ocat_ref):
    x1 = _memattn(x_ref[...], g_ref, wq_ref, k_ref, v_ref, wo_ref, ocat_ref)
    o_ref[...] = x1
    hn = _rms(x1, g2_ref[...])
    hn_ref[...] = hn.astype(BF16)
    lgt = jnp.dot(hn, wr_ref[...], preferred_element_type=F32, precision=HIGHEST)
    lg = lgt.T[:N_EXPERTS]
    idx = lax.broadcasted_iota(jnp.int32, lg.shape, 0)
    m1 = jnp.max(lg, axis=0, keepdims=True)
    i1 = jnp.min(jnp.where(lg == m1, idx, N_EXPERTS), axis=0, keepdims=True)
    oh1 = idx == i1
    lg2 = jnp.where(oh1, -jnp.inf, lg)
    m2 = jnp.max(lg2, axis=0, keepdims=True)
    i2 = jnp.min(jnp.where(lg2 == m2, idx, N_EXPERTS), axis=0, keepdims=True)
    oh2 = idx == i2
    e2 = jnp.exp(m2 - m1)
    g1 = 1.0 / (1.0 + e2)
    g2 = e2 / (1.0 + e2)
    ct_ref[...] = jnp.where(oh1, g1, 0.0) + jnp.where(oh2, g2, 0.0)
    sel = (oh1 | oh2).astype(F32)
    st_ref[...] = sel
    cnt_ref[0] = jnp.broadcast_to(jnp.sum(sel, axis=1, keepdims=True), (N_EXPERTS, LANES))


def _mem_specs(d, nt, tm, layer):
    wspec = pl.BlockSpec((None, d, d), lambda b, i: (layer, 0, 0))
    return [
        pl.BlockSpec((tm, d), lambda b, i: (b * nt + i, 0)),
        pl.BlockSpec((1, d), lambda b, i: (0, 0)),
        wspec,
        pl.BlockSpec((N_MEM, d), lambda b, i: (b, 2 * layer)),
        pl.BlockSpec((N_MEM, d), lambda b, i: (b, 2 * layer + 1)),
        wspec,
    ]


def _mem_ffn_layer(x, g_mem, w_q, kv_all, layer, w_o, g_ffn, w_gu, w_down, ffn_layer, bsz, seq, tm=1024):
    m, d = x.shape
    nt = seq // tm
    ff = w_down.shape[1]
    resident = lambda shape: pl.BlockSpec((None,) + shape, lambda b, i: (ffn_layer, 0, 0),
                                          pipeline_mode=pl.Buffered(1))
    return pl.pallas_call(
        _mem_ffn_kernel,
        grid=(bsz, nt),
        in_specs=_mem_specs(d, nt, tm, layer) + [
            pl.BlockSpec((1, d), lambda b, i: (0, 0)),
            resident((d, 2 * ff)),
            resident((ff, d)),
        ],
        out_specs=pl.BlockSpec((tm, d), lambda b, i: (b * nt + i, 0)),
        out_shape=jax.ShapeDtypeStruct((m, d), F32),
        scratch_shapes=[pltpu.VMEM((tm, d), BF16)],
        compiler_params=_params("parallel", "parallel"),
        name="mem_ffn",
    )(x, g_mem.reshape(1, d), w_q, kv_all, kv_all, w_o, g_ffn.reshape(1, d), w_gu, w_down)


MOE_TM = 512
MOE_TR = 512
MOE_HALF = MOE_TM // 2
MOE_ALIGN = 16
MOE_TF = 512
XS_W = D_MODEL + LANES
MOE_ZR = MOE_TR + MOE_HALF


def _tile_rank(st):
    tm = st.shape[1]
    before = (lax.broadcasted_iota(jnp.int32, (tm, tm), 0)
              < lax.broadcasted_iota(jnp.int32, (tm, tm), 1))
    return jnp.dot(st.astype(BF16), before.astype(BF16), preferred_element_type=F32)


def _one_hot_rows(rank, st, e, half):
    tm = rank.shape[1]
    rows = lax.broadcasted_iota(jnp.int32, (MOE_HALF, tm), 0).astype(F32) + float(half * MOE_HALF)
    hit = (rank[e:e + 1] == rows) & (st[e:e + 1] > 0.0)
    return jnp.where(hit, 1.0, 0.0).astype(BF16)


def _one_hot_all(rank, st):
    return jnp.concatenate([_one_hot_rows(rank, st, e, 0) for e in range(N_EXPERTS)], axis=0)


def _dispatch_kernel(base_ref, n_ref, tail_ref, hn_ref, ct_ref, st_ref, xs_ref, stage, stage2, zbuf,
                     sem, sem2, zsem):
    i = pl.program_id(0)
    last = pl.num_programs(0) - 1
    slot = i % 2

    @pl.when(i == 0)
    def _():
        zbuf[...] = jnp.zeros_like(zbuf)

        def zero(row, nrows):
            cp = pltpu.make_async_copy(zbuf.at[pl.ds(0, nrows)], xs_ref.at[pl.ds(row, nrows)], zsem)
            cp.start()
            cp.wait()

        for e in range(N_EXPERTS):
            zero(pl.multiple_of(tail_ref[e], MOE_ALIGN), MOE_ZR)

        def body(b, carry):
            zero(pl.multiple_of(b * MOE_TR, MOE_TR), MOE_TR)
            return carry

        lax.fori_loop(tail_ref[N_EXPERTS], xs_ref.shape[0] // MOE_TR, body, 0)

    hn = hn_ref[...]
    ct = ct_ref[...]
    st = st_ref[...]
    rank = _tile_rank(st)
    g_hi = ct.astype(BF16)
    g_lo = (ct - g_hi.astype(F32)).astype(BF16)
    gm = jnp.concatenate(
        [g_hi, g_lo, jnp.zeros((LANES - 2 * N_EXPERTS, ct.shape[1]), BF16)], axis=0)
    nt_dims = (((1,), (1,)), ((), ()))

    def copy(step, e):
        row = pl.multiple_of(base_ref[step * N_EXPERTS + e], MOE_ALIGN)
        return pltpu.make_async_copy(stage.at[step % 2, pl.ds(e * MOE_HALF, MOE_HALF)],
                                     xs_ref.at[pl.ds(row, MOE_HALF)], sem.at[step % 2, e])

    def copy2(e):
        row = pl.multiple_of(base_ref[i * N_EXPERTS + e] + MOE_HALF, MOE_ALIGN)
        return pltpu.make_async_copy(stage2.at[e], xs_ref.at[pl.ds(row, MOE_HALF)], sem2.at[e])

    def drain(step):
        for e in range(N_EXPERTS):
            copy(step, e).wait()

    p = _one_hot_all(rank, st)
    stage[slot, :, :D_MODEL] = jnp.dot(p, hn, preferred_element_type=F32).astype(BF16)
    stage[slot, :, D_MODEL:] = lax.dot_general(p, gm, nt_dims, preferred_element_type=F32).astype(BF16)
    pl.when(i >= 1)(lambda: drain(i - 1))
    for e in range(N_EXPERTS):
        copy(i, e).start()

    def fill2(e):
        p2 = _one_hot_rows(rank, st, e, 1)
        stage2[e, :, :D_MODEL] = jnp.dot(p2, hn, preferred_element_type=F32).astype(BF16)
        stage2[e, :, D_MODEL:] = lax.dot_general(p2, gm, nt_dims, preferred_element_type=F32).astype(BF16)
        copy2(e).start()

    for e in range(N_EXPERTS):
        pl.when(n_ref[i * N_EXPERTS + e] > MOE_HALF)(functools.partial(fill2, e))
    for e in range(N_EXPERTS):
        pl.when(n_ref[i * N_EXPERTS + e] > MOE_HALF)(lambda e=e: copy2(e).wait())

    pl.when(i == last)(lambda: drain(i))


def _expert_kernel(be_ref, nu_ref, x_ref, wg_ref, wu_ref, wd_ref, o_ref, acc_ref):
    b = pl.program_id(0)

    @pl.when(b < nu_ref[0])
    def _():
        e = be_ref[b]
        xe = x_ref[...]
        h = xe[:, :D_MODEL]
        ext = xe[:, D_MODEL:].astype(F32)
        lane = lax.broadcasted_iota(jnp.int32, ext.shape, 1)
        g = jnp.sum(jnp.where((lane == e) | (lane == e + N_EXPERTS), ext, 0.0), axis=1, keepdims=True)
        ff = wd_ref.shape[1]
        for c in range(ff // MOE_TF):
            lo = c * MOE_TF
            gate = jnp.dot(h, wg_ref[0, :, lo:lo + MOE_TF], preferred_element_type=F32)
            up = jnp.dot(h, wu_ref[0, :, lo:lo + MOE_TF], preferred_element_type=F32)
            act = (gate * jax.nn.sigmoid(gate) * up * g).astype(BF16)
            part = jnp.dot(act, wd_ref[0, lo:lo + MOE_TF, :], preferred_element_type=F32)
            if c == 0:
                acc_ref[...] = part
            else:
                acc_ref[...] += part
        o_ref[...] = acc_ref[...].astype(o_ref.dtype)

    @pl.when(b >= nu_ref[0])
    def _():
        o_ref[...] = jnp.zeros_like(o_ref)


def _combine_kernel(base_ref, n_ref, x_ref, st_ref, ys_ref, gout_ref, o_ref, ybuf, ybuf2, sem, sem2,
                    *, out_norm):
    i = pl.program_id(0)
    slot = i % 2

    def copy(step, e):
        row = pl.multiple_of(base_ref[step * N_EXPERTS + e], MOE_ALIGN)
        return pltpu.make_async_copy(ys_ref.at[pl.ds(row, MOE_HALF)],
                                     ybuf.at[step % 2, pl.ds(e * MOE_HALF, MOE_HALF)],
                                     sem.at[step % 2, e])

    def copy2(e):
        row = pl.multiple_of(base_ref[i * N_EXPERTS + e] + MOE_HALF, MOE_ALIGN)
        return pltpu.make_async_copy(ys_ref.at[pl.ds(row, MOE_HALF)], ybuf2.at[e], sem2.at[e])

    def fetch(step):
        for e in range(N_EXPERTS):
            copy(step, e).start()

    pl.when(i == 0)(lambda: fetch(i))
    pl.when(i + 1 < pl.num_programs(0))(lambda: fetch(i + 1))
    for e in range(N_EXPERTS):
        pl.when(n_ref[i * N_EXPERTS + e] > MOE_HALF)(lambda e=e: copy2(e).start())

    st = st_ref[...]
    rank = _tile_rank(st)
    p = _one_hot_all(rank, st)
    for e in range(N_EXPERTS):
        copy(i, e).wait()
    tn_dims = (((0,), (0,)), ((), ()))
    o_ref[...] = x_ref[...] + lax.dot_general(p, ybuf[slot], tn_dims, preferred_element_type=F32)

    def gather2(e):
        copy2(e).wait()
        p2 = _one_hot_rows(rank, st, e, 1)
        o_ref[...] += lax.dot_general(p2, ybuf2[e], tn_dims, preferred_element_type=F32)

    for e in range(N_EXPERTS):
        pl.when(n_ref[i * N_EXPERTS + e] > MOE_HALF)(functools.partial(gather2, e))
    if out_norm:
        o_ref[...] = _rms(o_ref[...], gout_ref[...])


def _moe_layer(x, g_mem, w_q, kv_all, layer, w_o, g_ffn, w_router, w_gu, w_down, moe_layer, bsz, seq,
               g_out=None):
    m, d = x.shape
    ff = w_down.shape[2]
    nt = m // MOE_TM
    ntb = seq // MOE_TM
    i32 = jnp.int32
    x, hn, ct, st, cnt = pl.pallas_call(
        _mem_route_kernel,
        grid=(bsz, ntb),
        in_specs=_mem_specs(d, ntb, MOE_TM, layer) + [
            pl.BlockSpec((1, d), lambda b, i: (0, 0)),
            pl.BlockSpec((d, LANES), lambda b, i: (0, 0)),
        ],
        out_specs=[
            pl.BlockSpec((MOE_TM, d), lambda b, i: (b * ntb + i, 0)),
            pl.BlockSpec((MOE_TM, d), lambda b, i: (b * ntb + i, 0)),
            pl.BlockSpec((N_EXPERTS, MOE_TM), lambda b, i: (0, b * ntb + i)),
            pl.BlockSpec((N_EXPERTS, MOE_TM), lambda b, i: (0, b * ntb + i)),
            pl.BlockSpec((1, N_EXPERTS, LANES), lambda b, i: (b * ntb + i, 0, 0)),
        ],
        out_shape=[
            jax.ShapeDtypeStruct((m, d), F32),
            jax.ShapeDtypeStruct((m, d), BF16),
            jax.ShapeDtypeStruct((N_EXPERTS, m), F32),
            jax.ShapeDtypeStruct((N_EXPERTS, m), F32),
            jax.ShapeDtypeStruct((nt, N_EXPERTS, LANES), F32),
        ],
        scratch_shapes=[pltpu.VMEM((MOE_TM, d), BF16)],
        compiler_params=_params("parallel", "parallel"),
        name="mem_route",
    )(x, g_mem.reshape(1, d), w_q, kv_all, kv_all, w_o, g_ffn.reshape(1, d),
      jnp.pad(w_router.astype(F32), ((0, 0), (0, LANES - N_EXPERTS))))

    n = cnt[:, :, 0].astype(i32)
    chunk = (n + MOE_ALIGN - 1) // MOE_ALIGN * MOE_ALIGN
    seg = (jnp.sum(chunk, axis=0) + MOE_HALF + MOE_TR - 1) // MOE_TR * MOE_TR
    seg_off = jnp.cumsum(seg) - seg
    base = (seg_off[None, :] + jnp.cumsum(chunk, axis=0) - chunk).reshape(-1).astype(i32)
    n_flat = n.reshape(-1)
    bound = 2 * m + nt * N_EXPERTS * (MOE_ALIGN - 1) + N_EXPERTS * (MOE_HALF + MOE_TR - 1)
    nb = -(-bound // MOE_TR)
    blk_end = jnp.cumsum(seg // MOE_TR)
    blk_expert = jnp.minimum(jnp.sum(jnp.arange(nb, dtype=i32)[:, None] >= blk_end[None, :], axis=1),
                             N_EXPERTS - 1).astype(i32)
    n_used = blk_end[-1:].astype(i32)
    rows = nb * MOE_TR
    tail = jnp.concatenate([jnp.maximum(seg_off + seg - MOE_ZR, 0), n_used]).astype(i32)

    tile_spec = lambda shape, imap: pl.BlockSpec(shape, imap)
    xs = pl.pallas_call(
        _dispatch_kernel,
        grid_spec=pltpu.PrefetchScalarGridSpec(
            num_scalar_prefetch=3,
            grid=(nt,),
            in_specs=[
                tile_spec((MOE_TM, d), lambda i, b, c, t: (i, 0)),
                tile_spec((N_EXPERTS, MOE_TM), lambda i, b, c, t: (0, i)),
                tile_spec((N_EXPERTS, MOE_TM), lambda i, b, c, t: (0, i)),
            ],
            out_specs=pl.BlockSpec(memory_space=pl.ANY),
            scratch_shapes=[
                pltpu.VMEM((2, N_EXPERTS * MOE_HALF, XS_W), BF16),
                pltpu.VMEM((N_EXPERTS, MOE_HALF, XS_W), BF16),
                pltpu.VMEM((MOE_ZR, XS_W), BF16),
                pltpu.SemaphoreType.DMA((2, N_EXPERTS)),
                pltpu.SemaphoreType.DMA((N_EXPERTS,)),
                pltpu.SemaphoreType.DMA(()),
            ],
        ),
        out_shape=jax.ShapeDtypeStruct((rows, XS_W), BF16),
        compiler_params=_params("arbitrary"),
        name="moe_dispatch",
    )(base, n_flat, tail, hn, ct, st)

    ys = pl.pallas_call(
        _expert_kernel,
        grid_spec=pltpu.PrefetchScalarGridSpec(
            num_scalar_prefetch=2,
            grid=(nb,),
            in_specs=[
                pl.BlockSpec((MOE_TR, XS_W), lambda b, be, nu: (b, 0)),
                pl.BlockSpec((None, 1, d, ff), lambda b, be, nu: (moe_layer, be[b], 0, 0)),
                pl.BlockSpec((None, 1, d, ff), lambda b, be, nu: (moe_layer, be[b], 0, 1)),
                pl.BlockSpec((None, 1, ff, d), lambda b, be, nu: (moe_layer, be[b], 0, 0)),
            ],
            out_specs=pl.BlockSpec((MOE_TR, d), lambda b, be, nu: (b, 0)),
            scratch_shapes=[pltpu.VMEM((MOE_TR, d), F32)],
        ),
        out_shape=jax.ShapeDtypeStruct((rows, d), BF16),
        compiler_params=_params("arbitrary"),
        name="moe_experts",
    )(blk_expert, n_used, xs, w_gu, w_gu, w_down)

    out_norm = g_out is not None
    g_out = jnp.ones((d,), F32) if g_out is None else g_out
    return pl.pallas_call(
        functools.partial(_combine_kernel, out_norm=out_norm),
        grid_spec=pltpu.PrefetchScalarGridSpec(
            num_scalar_prefetch=2,
            grid=(nt,),
            in_specs=[
                tile_spec((MOE_TM, d), lambda i, b, c: (i, 0)),
                tile_spec((N_EXPERTS, MOE_TM), lambda i, b, c: (0, i)),
                pl.BlockSpec(memory_space=pl.ANY),
                tile_spec((1, d), lambda i, b, c: (0, 0)),
            ],
            out_specs=tile_spec((MOE_TM, d), lambda i, b, c: (i, 0)),
            scratch_shapes=[
                pltpu.VMEM((2, N_EXPERTS * MOE_HALF, d), BF16),
                pltpu.VMEM((N_EXPERTS, MOE_HALF, d), BF16),
                pltpu.SemaphoreType.DMA((2, N_EXPERTS)),
                pltpu.SemaphoreType.DMA((N_EXPERTS,)),
            ],
        ),
        out_shape=jax.ShapeDtypeStruct((m, d), F32),
        compiler_params=_params("arbitrary"),
        name="moe_combine",
    )(base, n_flat, x, st, ys, g_out.reshape(1, d).astype(F32))


def kernel(x, mem, norm_mix, norm_mem, norm_ffn, mem_norm, final_norm, s5_w_in, s5_lam_re, s5_lam_im, s5_log_dt, s5_b_re, s5_b_im, s5_c_re, s5_c_im, s5_d, s5_w_glu, att_w_qkv, att_w_o, att_rel_bias, lru_w_in, lru_conv_w, lru_conv_b, lru_w_a, lru_b_a, lru_w_x, lru_b_x, lru_lam, lru_w_out, mem_w_q, mem_w_kv, mem_w_o, ffn_w_gu, ffn_w_down, moe_w_router, moe_w_gu, moe_w_down):
    bsz, seq, d = x.shape
    x = x.reshape(bsz * seq, d).astype(F32)

    w_kv_all = jnp.concatenate([mem_w_kv[i] for i in range(DEPTH)], axis=1).astype(BF16)
    kv_all = _norm_linear(mem.reshape(bsz * N_MEM, d).astype(F32), mem_norm, w_kv_all, BF16,
                          tm=bsz * N_MEM, tn=2 * d)

    mem_wq, mem_wo = mem_w_q.astype(BF16), mem_w_o.astype(BF16)
    ffn_wgu, ffn_wd = ffn_w_gu.astype(BF16), ffn_w_down.astype(BF16)
    moe_wgu, moe_wd = moe_w_gu.astype(BF16), moe_w_down.astype(BF16)
    for i in range(DEPTH):
        kind, j = i % 3, i // 3
        if kind == 0:
            x = _s5_layer(x, norm_mix[i], s5_w_in[j], s5_lam_re[j], s5_lam_im[j], s5_log_dt[j],
                          s5_b_re[j], s5_b_im[j], s5_c_re[j], s5_c_im[j], s5_d[j], s5_w_glu[j],
                          bsz, seq)
        elif kind == 1:
            x = _attn_layer(x, norm_mix[i], att_w_qkv[j], att_w_o[j], att_rel_bias[j], bsz, seq)
        else:
            x = _lru_layer(x, norm_mix[i], lru_w_in[j], lru_conv_w[j], lru_conv_b[j], lru_w_a[j],
                           lru_b_a[j], lru_w_x[j], lru_b_x[j], lru_lam[j], lru_w_out[j], bsz, seq)
        if i % 2 == 0:
            x = _mem_ffn_layer(x, norm_mem[i], mem_wq, kv_all, i, mem_wo, norm_ffn[i], ffn_wgu, ffn_wd,
                               i // 2, bsz, seq)
        else:
            x = _moe_layer(x, norm_mem[i], mem_wq, kv_all, i, mem_wo, norm_ffn[i], moe_w_router[i // 2],
                           moe_wgu, moe_wd, i // 2, bsz, seq,
                           g_out=final_norm if i == DEPTH - 1 else None)
    return x.reshape(bsz, seq, d)
```

```python
import functools
import math

import jax
import jax.numpy as jnp
from jax import lax
from jax.experimental import pallas as pl
from jax.experimental.pallas import tpu as pltpu

F32 = jnp.float32
BF16 = jnp.bfloat16
HIGHEST = lax.Precision.HIGHEST

D_MODEL = 1024
DEPTH = 4
CHUNK = 64
N_MEM = 256
EPS = 1e-6

S5_GROUP = 16
S5_GROUPS = D_MODEL // S5_GROUP
S5_STATE = 64

ATT_HEADS = 16
ATT_HEAD_DIM = D_MODEL // ATT_HEADS
LEFT_CHUNKS = 8
MAX_REL = 128
ATT_TQ = 4 * CHUNK
ATT_LEFT = LEFT_CHUNKS * CHUNK
ATT_TK = ATT_LEFT + ATT_TQ
NEG_INF = -1e30
LOG2E = math.log2(math.e)

LRU_BLOCKS = 8
LRU_BLOCK_W = D_MODEL // LRU_BLOCKS
CONV_W = 4
LRU_C = 8.0
LRU_T = 512
SUBLANES = 8
LANES = 128

MEM_HEADS = 4
MEM_HEAD_DIM = D_MODEL // MEM_HEADS

N_EXPERTS = 8
FFN_TF = 256

VMEM_LIMIT = 56 * 1024 * 1024


def _params(*sem):
    return pltpu.CompilerParams(dimension_semantics=sem, vmem_limit_bytes=VMEM_LIMIT)


def _rms(x, g):
    ms = jnp.mean(x * x, axis=-1, keepdims=True)
    return x * lax.rsqrt(ms + EPS) * g


def _norm_linear_kernel(x_ref, g_ref, w_ref, o_ref, hn_ref):
    @pl.when(pl.program_id(1) == 0)
    def _():
        hn_ref[...] = _rms(x_ref[...], g_ref[...]).astype(BF16)

    o_ref[...] = jnp.dot(hn_ref[...], w_ref[...], preferred_element_type=F32).astype(o_ref.dtype)


def _norm_linear(x, g, w, out_dtype, tm=512, tn=None):
    m, d = x.shape
    n = w.shape[1]
    tn = n if tn is None else tn
    return pl.pallas_call(
        _norm_linear_kernel,
        grid=(m // tm, n // tn),
        in_specs=[
            pl.BlockSpec((tm, d), lambda i, j: (i, 0)),
            pl.BlockSpec((1, d), lambda i, j: (0, 0)),
            pl.BlockSpec((d, tn), lambda i, j: (0, j)),
        ],
        out_specs=pl.BlockSpec((tm, tn), lambda i, j: (i, j)),
        out_shape=jax.ShapeDtypeStruct((m, n), out_dtype),
        scratch_shapes=[pltpu.VMEM((tm, d), BF16)],
        compiler_params=_params("parallel", "arbitrary"),
        name="norm_linear",
    )(x, g.reshape(1, d), w)


def _s5_operators(lam_re, lam_im, log_dt, b_re, b_im, c_re, c_im, d_skip, n_chunks):
    g, n, p = S5_GROUPS, S5_STATE, S5_GROUP
    lr = lam_re.astype(F32)
    li = lam_im.astype(F32)
    dt = jnp.exp(log_dt.astype(F32))[:, None]
    mag = jnp.exp(lr * dt)
    ar = mag * jnp.cos(li * dt)
    ai = mag * jnp.sin(li * dt)
    den = lr * lr + li * li
    fr = ((ar - 1.0) * lr + ai * li) / den
    fi = (ai * lr - (ar - 1.0) * li) / den
    bbr = fr[..., None] * b_re - fi[..., None] * b_im
    bbi = fr[..., None] * b_im + fi[..., None] * b_re
    bbt = jnp.concatenate([bbr, bbi], axis=1).transpose(0, 2, 1)

    kk = jnp.arange(CHUNK + 1, dtype=F32)
    mag_k = jnp.exp((lr * dt)[..., None] * kk)
    pr = mag_k * jnp.cos((li * dt)[..., None] * kk)
    pi = mag_k * jnp.sin((li * dt)[..., None] * kk)
    pk = jnp.concatenate([pr[..., :CHUNK], pi[..., :CHUNK], pr[..., 1:], pi[..., 1:],
                          pr[..., CHUNK - 1::-1], pi[..., CHUNK - 1::-1]], axis=-1)
    cb = jnp.concatenate([c_re.transpose(0, 2, 1), c_im.transpose(0, 2, 1), bbr, bbi], axis=-1)
    dq = jnp.broadcast_to(d_skip.reshape(g, p, 1), (g, p, LANES))
    qr, qi = [pr[..., CHUNK]], [pi[..., CHUNK]]
    for _ in range(int(math.log2(n_chunks)) - 1):
        qr, qi = qr + [qr[-1] * qr[-1] - qi[-1] * qi[-1]], qi + [2.0 * qr[-1] * qi[-1]]
    levels = len(qr)
    qr, qi = jnp.stack(qr), jnp.stack(qi)
    m1 = jnp.concatenate([qr, qr], axis=-1)
    m2 = jnp.concatenate([-qi, qi], axis=-1)
    ap = jnp.stack([m1, m2], axis=-1).transpose(1, 2, 0, 3).reshape(g, 2 * n, 2 * levels)
    return bbt, pk, cb, dq, ap


def _s5_in_kernel(x3_ref, g_ref, wt_ref, u3_ref, xbuf, ubuf, sem_in, sem_out):
    s = pl.program_id(0)
    last = pl.num_programs(0) - 1
    slot = s % 2

    def load(step):
        return pltpu.make_async_copy(x3_ref.at[pl.ds(0, x3_ref.shape[0]), step], xbuf.at[step % 2],
                                     sem_in.at[step % 2])

    def store(step):
        return pltpu.make_async_copy(ubuf.at[step % 2], u3_ref.at[pl.ds(0, u3_ref.shape[0]), step],
                                     sem_out.at[step % 2])

    pl.when(s == 0)(lambda: load(s).start())
    pl.when(s < last)(lambda: load(s + 1).start())
    load(s).wait()
    pl.when(s >= 2)(lambda: store(s - 2).wait())
    hn = _rms(xbuf[slot], g_ref[...]).astype(BF16)
    ubuf[slot] = lax.dot_general(wt_ref[...], hn, (((1,), (1,)), ((), ())), preferred_element_type=F32)
    store(s).start()

    @pl.when(s == last)
    def _():
        store(s - 1).wait()
        store(s).wait()


def _tile_lanes(a, times):
    a2 = jnp.concatenate([a, a], axis=1)
    return jnp.concatenate([a2] * (times // 2), axis=1)


def _spread_lanes(a):
    r, c = a.shape
    low = lax.broadcasted_iota(jnp.int32, (r, LANES), 1) < CHUNK
    pairs = [jnp.where(low, jnp.broadcast_to(a[:, 2 * j:2 * j + 1], (r, LANES)),
                       jnp.broadcast_to(a[:, 2 * j + 1:2 * j + 2], (r, LANES))) for j in range(c // 2)]
    return jnp.concatenate(pairs, axis=1)


def _ssm_kernel(u_ref, bbt_ref, pk_ref, cb_ref, dq_ref, ap_ref, o_ref, toet_ref, *, n_chunks):
    n = S5_STATE
    rows = S5_GROUP * CHUNK
    ncol = u_ref.shape[-1]
    u = u_ref[0].reshape(rows, ncol).astype(BF16)

    pk = pk_ref[0]
    cb = cb_ref[0]
    cr, ci, br, bi = [_spread_lanes(cb[:, S5_GROUP * k:S5_GROUP * (k + 1)]) for k in range(4)]
    p0r, p0i, p1r, p1i, rr, ri = [_tile_lanes(pk[:, CHUNK * k:CHUNK * (k + 1)], S5_GROUP) for k in range(6)]
    lk = jnp.concatenate([cr * p0r - ci * p0i, -(cr * p0i + ci * p0r)], axis=0)
    cinjt = jnp.concatenate([cr * p1r - ci * p1i, -(cr * p1i + ci * p1r)], axis=0).astype(BF16)
    bm = jnp.concatenate([rr * br - ri * bi, rr * bi + ri * br], axis=0).astype(BF16)

    kern = jnp.dot(bbt_ref[0], lk, preferred_element_type=F32, precision=HIGHEST)
    lane = lax.broadcasted_iota(jnp.int32, (S5_GROUP, rows), 1)
    own_lag0 = lane == lax.broadcasted_iota(jnp.int32, (S5_GROUP, rows), 0) * CHUNK
    kern = kern + jnp.where(own_lag0, jnp.concatenate([dq_ref[0]] * (rows // LANES), axis=1), 0.0)
    causal = (lax.broadcasted_iota(jnp.int32, (CHUNK, rows), 1) % CHUNK
              >= lax.broadcasted_iota(jnp.int32, (CHUNK, rows), 0))
    for p in range(S5_GROUP):
        base = jnp.broadcast_to(kern[p:p + 1, :], (CHUNK, rows))
        rolled = pltpu.roll(base, 0, 1, stride=1, stride_axis=0)
        toet_ref[p * CHUNK:(p + 1) * CHUNK, :] = jnp.where(causal, rolled, 0.0).astype(BF16)
    tn_dims = (((0,), (0,)), ((), ()))
    y = lax.dot_general(toet_ref[...], u, tn_dims, preferred_element_type=F32)
    x = jnp.dot(bm, u, preferred_element_type=F32)
    ap = ap_ref[0]
    col = lax.broadcasted_iota(jnp.int32, x.shape, 1) % n_chunks
    for k in range(int(math.log2(n_chunks))):
        sh = 1 << k
        xs = jnp.where(col >= sh, pltpu.roll(x, sh, 1), 0.0)
        xsw = jnp.concatenate([xs[n:], xs[:n]], axis=0)
        x = x + ap[:, 2 * k:2 * k + 1] * xs + ap[:, 2 * k + 1:2 * k + 2] * xsw
    h0 = jnp.where(col >= 1, pltpu.roll(x, 1, 1), 0.0)
    h0_hi = h0.astype(BF16)
    h0_lo = (h0 - h0_hi.astype(F32)).astype(BF16)
    y = y + lax.dot_general(cinjt, h0_hi, tn_dims, preferred_element_type=F32)
    y = y + lax.dot_general(cinjt, h0_lo, tn_dims, preferred_element_type=F32)
    o_ref[0] = jax.nn.gelu(y).reshape(S5_GROUP, CHUNK, ncol)


def _s5_out_kernel(v3_ref, x3_ref, w_ref, o3_ref, vbuf, xbuf, obuf, sem_v, sem_x, sem_o):
    s = pl.program_id(0)
    last = pl.num_programs(0) - 1
    slot = s % 2

    def load_v(step):
        return pltpu.make_async_copy(v3_ref.at[pl.ds(0, v3_ref.shape[0]), step], vbuf.at[step % 2],
                                     sem_v.at[step % 2])

    def load_x(step):
        return pltpu.make_async_copy(x3_ref.at[pl.ds(0, x3_ref.shape[0]), step], xbuf.at[step % 2],
                                     sem_x.at[step % 2])

    def store(step):
        return pltpu.make_async_copy(obuf.at[step % 2], o3_ref.at[pl.ds(0, o3_ref.shape[0]), step],
                                     sem_o.at[step % 2])

    def load(step):
        load_v(step).start()
        load_x(step).start()

    pl.when(s == 0)(lambda: load(s))
    pl.when(s < last)(lambda: load(s + 1))
    load_v(s).wait()
    load_x(s).wait()
    pl.when(s >= 2)(lambda: store(s - 2).wait())
    z = lax.dot_general(vbuf[slot].astype(BF16), w_ref[...], (((0,), (0,)), ((), ())),
                        preferred_element_type=F32)
    d = xbuf.shape[-1]
    obuf[slot] = xbuf[slot] + z[:, :d] * jax.nn.sigmoid(z[:, d:])
    store(s).start()

    @pl.when(s == last)
    def _():
        store(s - 1).wait()
        store(s).wait()


def _s5_layer(x, g_norm, w_in, lam_re, lam_im, log_dt, b_re, b_im, c_re, c_im, d_skip, w_glu,
              bsz, seq):
    m, d = x.shape
    n_chunks = seq // CHUNK
    ncol = bsz * n_chunks
    rows = S5_GROUP * CHUNK
    bbt, pk, cb, dq, ap = _s5_operators(lam_re, lam_im, log_dt, b_re, b_im, c_re, c_im, d_skip,
                                        n_chunks)
    x3 = x.reshape(ncol, CHUNK, d)
    hbm = pl.BlockSpec(memory_space=pl.ANY)
    u3 = pl.pallas_call(
        _s5_in_kernel,
        grid=(CHUNK,),
        in_specs=[hbm, pl.BlockSpec((1, d), lambda s: (0, 0)), pl.BlockSpec((d, d), lambda s: (0, 0))],
        out_specs=hbm,
        out_shape=jax.ShapeDtypeStruct((d, CHUNK, ncol), F32),
        scratch_shapes=[
            pltpu.VMEM((2, ncol, d), F32),
            pltpu.VMEM((2, d, ncol), F32),
            pltpu.SemaphoreType.DMA((2,)),
            pltpu.SemaphoreType.DMA((2,)),
        ],
        compiler_params=_params("arbitrary"),
        name="s5_in",
    )(x3, g_norm.reshape(1, d), w_in.T.astype(BF16))
    blk4 = (1, S5_GROUP, CHUNK, ncol)
    per_group = lambda a: pl.BlockSpec((1,) + a.shape[1:], lambda g: (g, 0, 0))
    v4 = pl.pallas_call(
        functools.partial(_ssm_kernel, n_chunks=n_chunks),
        grid=(S5_GROUPS,),
        in_specs=[pl.BlockSpec(blk4, lambda g: (g, 0, 0, 0))]
        + [per_group(a) for a in (bbt, pk, cb, dq, ap)],
        out_specs=pl.BlockSpec(blk4, lambda g: (g, 0, 0, 0)),
        out_shape=jax.ShapeDtypeStruct((S5_GROUPS, S5_GROUP, CHUNK, ncol), F32),
        scratch_shapes=[pltpu.VMEM((rows, rows), BF16)],
        compiler_params=_params("parallel"),
        name="s5_ssm",
    )(u3.reshape(S5_GROUPS, S5_GROUP, CHUNK, ncol), bbt, pk, cb, dq, ap)
    o3 = pl.pallas_call(
        _s5_out_kernel,
        grid=(CHUNK,),
        in_specs=[hbm, hbm, pl.BlockSpec((d, 2 * d), lambda s: (0, 0))],
        out_specs=hbm,
        out_shape=jax.ShapeDtypeStruct((ncol, CHUNK, d), F32),
        scratch_shapes=[
            pltpu.VMEM((2, d, ncol), F32),
            pltpu.VMEM((2, ncol, d), F32),
            pltpu.VMEM((2, ncol, d), F32),
            pltpu.SemaphoreType.DMA((2,)),
            pltpu.SemaphoreType.DMA((2,)),
            pltpu.SemaphoreType.DMA((2,)),
        ],
        compiler_params=_params("arbitrary"),
        name="s5_out",
    )(v4.reshape(d, CHUNK, ncol), x3, w_glu.astype(BF16))
    return o3.reshape(m, d)


ATT_PERIOD = ATT_TQ + ATT_TK


def _attn_bias_seed(rel_bias):
    k = jnp.arange(ATT_PERIOD)
    rel = jnp.where(k < ATT_TK, ATT_LEFT - k, ATT_LEFT + ATT_PERIOD - k)
    return jnp.take(rel_bias.astype(F32), jnp.clip(rel, -MAX_REL, MAX_REL) + MAX_REL, axis=1)


def _attn_kernel(x_ref, q_ref, k0_ref, k1_ref, k2_ref, v0_ref, v1_ref, v2_ref, seed_ref, wo_ref,
                 o_ref, bias_ref, ocat_ref):
    i = pl.program_id(1)

    @pl.when((pl.program_id(0) == 0) & (i == 0))
    def _():
        r = lax.broadcasted_iota(jnp.int32, (ATT_TQ, ATT_TK), 0)
        j = lax.broadcasted_iota(jnp.int32, (ATT_TQ, ATT_TK), 1)
        dchunk = (r + ATT_LEFT) // CHUNK - j // CHUNK
        valid = (dchunk >= 0) & (dchunk <= LEFT_CHUNKS)
        for h in range(ATT_HEADS):
            seed = jnp.broadcast_to(seed_ref[h:h + 1, :], (ATT_TQ, ATT_PERIOD))
            rolled = pltpu.roll(seed, 0, 1, stride=1, stride_axis=0)
            bias_ref[h] = jnp.where(valid, rolled[:, :ATT_TK] * LOG2E, NEG_INF)

    lane = lax.broadcasted_iota(jnp.int32, (1, LANES), 1)
    col = lax.broadcasted_iota(jnp.int32, (ATT_TQ, ATT_TK), 1)
    pad = col < ATT_LEFT - i * ATT_TQ
    scale = ATT_HEAD_DIM ** -0.5
    heads_per = LANES // ATT_HEAD_DIM
    for hp in range(D_MODEL // LANES):
        sl = slice(hp * LANES, (hp + 1) * LANES)
        q = q_ref[:, sl] * scale
        k = jnp.concatenate([k0_ref[:, sl], k1_ref[:, sl], k2_ref[:, sl]], axis=0)
        v = jnp.concatenate([v0_ref[:, sl], v1_ref[:, sl], v2_ref[:, sl]], axis=0)
        acc = jnp.zeros((ATT_TQ, LANES), F32)
        for h in range(heads_per):
            in_head = (lane >= h * ATT_HEAD_DIM) & (lane < (h + 1) * ATT_HEAD_DIM)
            qh = jnp.where(in_head, q, jnp.zeros_like(q))
            s = lax.dot_general(qh, k, (((1,), (1,)), ((), ())), preferred_element_type=F32)
            s = jnp.where(pad, NEG_INF, s + bias_ref[hp * heads_per + h])
            p = jnp.exp2(s - jnp.max(s, axis=-1, keepdims=True))
            vh = jnp.where(in_head, v, jnp.ones_like(v))
            pv = jnp.dot(p.astype(BF16), vh, preferred_element_type=F32)
            acc = jnp.where(in_head, pv / pltpu.roll(pv, ATT_HEAD_DIM, 1), acc)
        ocat_ref[:, sl] = acc.astype(BF16)
    o_ref[...] = x_ref[...] + jnp.dot(ocat_ref[...], wo_ref[...], preferred_element_type=F32)


def _attn_layer(x, g_norm, w_qkv, w_o, rel_bias, bsz, seq):
    m, d = x.shape
    w_qkv = jnp.concatenate([w_qkv[:, :d] * LOG2E, w_qkv[:, d:]], axis=1)
    qkv = _norm_linear(x, g_norm, w_qkv.astype(BF16), BF16, tm=1024)
    nq = seq // ATT_TQ

    def rows(col, back):
        return lambda b, i: (b * nq + jnp.maximum(i - back, 0), col)

    blk = (ATT_TQ, d)
    nkb = ATT_TK // ATT_TQ
    return pl.pallas_call(
        _attn_kernel,
        grid=(bsz, nq),
        in_specs=(
            [pl.BlockSpec(blk, rows(0, 0)), pl.BlockSpec(blk, rows(0, 0))]
            + [pl.BlockSpec(blk, rows(1, nkb - 1 - kb)) for kb in range(nkb)]
            + [pl.BlockSpec(blk, rows(2, nkb - 1 - kb)) for kb in range(nkb)]
            + [pl.BlockSpec((ATT_HEADS, ATT_PERIOD), lambda b, i: (0, 0)),
               pl.BlockSpec((d, d), lambda b, i: (0, 0))]
        ),
        out_specs=pl.BlockSpec(blk, rows(0, 0)),
        out_shape=jax.ShapeDtypeStruct((m, d), F32),
        scratch_shapes=[pltpu.VMEM((ATT_HEADS, ATT_TQ, ATT_TK), F32), pltpu.VMEM((ATT_TQ, d), BF16)],
        compiler_params=_params("arbitrary", "arbitrary"),
        name="chunk_attn",
    )(x, *([qkv] * (1 + 2 * nkb)), _attn_bias_seed(rel_bias), w_o.astype(BF16))


def _lru_kernel(x_ref, g_ref, win_ref, cw_ref, cb_ref, wax_ref, ba_ref, bx_ref, lam_ref, wout_ref,
                o_ref, prev_ref, carry_ref, a_s, b_s):
    t, w = x_ref.shape

    @pl.when(pl.program_id(1) == 0)
    def _():
        prev_ref[...] = jnp.zeros_like(prev_ref)
        carry_ref[...] = jnp.zeros_like(carry_ref)

    x = x_ref[...]
    z = jnp.dot(_rms(x, g_ref[...]).astype(BF16), win_ref[...], preferred_element_type=F32)
    gate = z[:, :w]
    xr = z[:, w:]
    xext = jnp.concatenate([prev_ref[...], xr], axis=0)
    cw = cw_ref[...]
    xc = cb_ref[...] + cw[0:1] * xext[SUBLANES - 3:SUBLANES - 3 + t]
    for k in range(1, CONV_W):
        xc = xc + cw[k:k + 1] * xext[SUBLANES - 3 + k:SUBLANES - 3 + k + t]
    prev_ref[...] = xr[t - SUBLANES:]

    xcb = xc.astype(BF16)
    pre_a, pre_x = [], []
    for blk in range(LRU_BLOCKS):
        pre = jnp.dot(xcb[:, blk * LRU_BLOCK_W:(blk + 1) * LRU_BLOCK_W], wax_ref[blk],
                      preferred_element_type=F32)
        pre_a.append(pre[:, :LRU_BLOCK_W])
        pre_x.append(pre[:, LRU_BLOCK_W:])
    r = jax.nn.sigmoid(jnp.concatenate(pre_a, axis=1) + ba_ref[...])
    ig = jax.nn.sigmoid(jnp.concatenate(pre_x, axis=1) + bx_ref[...])
    z = -lam_ref[...]
    softplus = jnp.maximum(z, 0.0) + jnp.log1p(jnp.exp(-jnp.abs(z)))
    log_a = -LRU_C * r * softplus
    a = jnp.exp(log_a)
    b = jnp.sqrt(1.0 - a * a) * (ig * xc)

    a3 = a.reshape(t // SUBLANES, SUBLANES, w)
    b3 = b.reshape(t // SUBLANES, SUBLANES, w)
    sub = lax.broadcasted_iota(jnp.int32, a3.shape, 1)
    for sh in (1, 2, 4):
        a_sh = jnp.where(sub >= sh, pltpu.roll(a3, sh, 1), 1.0)
        b_sh = jnp.where(sub >= sh, pltpu.roll(b3, sh, 1), 0.0)
        b3 = a3 * b_sh + b3
        a3 = a3 * a_sh
    a_s[...] = a3
    b_s[...] = b3

    def body(j, carry):
        hb = b_s[j] + a_s[j] * carry
        b_s[j] = hb
        return hb[SUBLANES - 1:SUBLANES, :]

    carry = lax.fori_loop(0, t // SUBLANES, body, carry_ref[0:1, :])
    carry_ref[0:1, :] = carry
    h = b_s[...].reshape(t, w)
    y = (jax.nn.gelu(gate) * h).astype(BF16)
    o_ref[...] = x + jnp.dot(y, wout_ref[...], preferred_element_type=F32)


def _lru_layer(x, g_norm, w_in, conv_w, conv_b, w_a, b_a, w_x, b_x, lam, w_out, bsz, seq):
    m, d = x.shape
    wax = jnp.concatenate([w_a, w_x], axis=-1).astype(BF16)
    nt = seq // LRU_T
    row = lambda v: v.reshape(1, d).astype(F32)
    vec = pl.BlockSpec((1, d), lambda b, i: (0, 0))
    return pl.pallas_call(
        _lru_kernel,
        grid=(bsz, nt),
        in_specs=[
            pl.BlockSpec((LRU_T, d), lambda b, i: (b * nt + i, 0)),
            vec,
            pl.BlockSpec((d, 2 * d), lambda b, i: (0, 0)),
            pl.BlockSpec((CONV_W, d), lambda b, i: (0, 0)),
            vec,
            pl.BlockSpec(wax.shape, lambda b, i: (0, 0, 0)),
            vec, vec, vec,
            pl.BlockSpec((d, d), lambda b, i: (0, 0)),
        ],
        out_specs=pl.BlockSpec((LRU_T, d), lambda b, i: (b * nt + i, 0)),
        out_shape=jax.ShapeDtypeStruct((m, d), F32),
        scratch_shapes=[
            pltpu.VMEM((SUBLANES, d), F32),
            pltpu.VMEM((SUBLANES, d), F32),
            pltpu.VMEM((LRU_T // SUBLANES, SUBLANES, d), F32),
            pltpu.VMEM((LRU_T // SUBLANES, SUBLANES, d), F32),
        ],
        compiler_params=_params("parallel", "arbitrary"),
        name="rglru",
    )(x, row(g_norm), w_in.astype(BF16), conv_w.astype(F32), row(conv_b), wax, row(b_a), row(b_x),
      row(lam), w_out.astype(BF16))


def _memattn(x, g_ref, wq_ref, k_ref, v_ref, wo_ref, ocat_ref):
    hn = _rms(x, g_ref[...]).astype(BF16)
    q = jnp.dot(hn, wq_ref[...], preferred_element_type=F32) * (MEM_HEAD_DIM ** -0.5)
    q = q.astype(BF16)
    for h in range(MEM_HEADS):
        sl = slice(h * MEM_HEAD_DIM, (h + 1) * MEM_HEAD_DIM)
        s = lax.dot_general(q[:, sl], k_ref[:, sl], (((1,), (1,)), ((), ())),
                            preferred_element_type=F32)
        mx = jnp.max(s, axis=-1, keepdims=True)
        p = jnp.exp(s - mx)
        l = jnp.sum(p, axis=-1, keepdims=True)
        o = jnp.dot(p.astype(BF16), v_ref[:, sl], preferred_element_type=F32) / l
        ocat_ref[:, sl] = o.astype(BF16)
    return x + jnp.dot(ocat_ref[...], wo_ref[...], preferred_element_type=F32)


def _mem_ffn_kernel(x_ref, g_ref, wq_ref, k_ref, v_ref, wo_ref, g2_ref, wgu_ref, wd_ref, o_ref, ocat_ref):
    ff = wd_ref.shape[0]
    x1 = _memattn(x_ref[...], g_ref, wq_ref, k_ref, v_ref, wo_ref, ocat_ref)
    hn = _rms(x1, g2_ref[...]).astype(BF16)
    o_ref[...] = x1
    for f in range(ff // FFN_TF):
        lo = f * FFN_TF
        gate = jnp.dot(hn, wgu_ref[:, lo:lo + FFN_TF], preferred_element_type=F32)
        up = jnp.dot(hn, wgu_ref[:, ff + lo:ff + lo + FFN_TF], preferred_element_type=F32)
        act = gate * jax.nn.sigmoid(gate) * up
        o_ref[...] += jnp.dot(act.astype(BF16), wd_ref[lo:lo + FFN_TF, :], preferred_element_type=F32)


def _mem_route_kernel(x_ref, g_ref, wq_ref, k_ref, v_ref, wo_ref, g2_ref, wrt_ref,
                      o_ref, hn_ref, ct_ref, st_ref, cnt_ref, ocat_ref):
    x1 = _memattn(x_ref[...], g_ref, wq_ref, k_ref, v_ref, wo_ref, ocat_ref)
    o_ref[...] = x1
    hn = _rms(x1, g2_ref[...])
    hn_ref[...] = hn.astype(BF16)
    lg = lax.dot_general(wrt_ref[...], hn, (((1,), (1,)), ((), ())),
                         preferred_element_type=F32, precision=HIGHEST)
    idx = lax.broadcasted_iota(jnp.int32, lg.shape, 0)
    m1 = jnp.max(lg, axis=0, keepdims=True)
    i1 = jnp.min(jnp.where(lg == m1, idx, N_EXPERTS), axis=0, keepdims=True)
    oh1 = idx == i1
    lg2 = jnp.where(oh1, -jnp.inf, lg)
    m2 = jnp.max(lg2, axis=0, keepdims=True)
    i2 = jnp.min(jnp.where(lg2 == m2, idx, N_EXPERTS), axis=0, keepdims=True)
    oh2 = idx == i2
    e2 = jnp.exp(m2 - m1)
    g1 = 1.0 / (1.0 + e2)
    g2 = e2 / (1.0 + e2)
    ct_ref[...] = jnp.where(oh1, g1, 0.0) + jnp.where(oh2, g2, 0.0)
    sel = (oh1 | oh2).astype(F32)
    st_ref[...] = sel
    cnt_ref[0] = jnp.broadcast_to(jnp.sum(sel, axis=1, keepdims=True), (N_EXPERTS, LANES))


def _mem_specs(d, nt, tm, layer):
    wspec = pl.BlockSpec((None, d, d), lambda b, i: (layer, 0, 0))
    return [
        pl.BlockSpec((tm, d), lambda b, i: (b * nt + i, 0)),
        pl.BlockSpec((1, d), lambda b, i: (0, 0)),
        wspec,
        pl.BlockSpec((N_MEM, d), lambda b, i: (b, 2 * layer)),
        pl.BlockSpec((N_MEM, d), lambda b, i: (b, 2 * layer + 1)),
        wspec,
    ]


def _mem_ffn_layer(x, g_mem, w_q, kv_all, layer, w_o, g_ffn, w_gu, w_down, ffn_layer, bsz, seq, tm=1024):
    m, d = x.shape
    nt = seq // tm
    ff = w_down.shape[1]
    resident = lambda shape: pl.BlockSpec((None,) + shape, lambda b, i: (ffn_layer, 0, 0),
                                          pipeline_mode=pl.Buffered(1))
    return pl.pallas_call(
        _mem_ffn_kernel,
        grid=(bsz, nt),
        in_specs=_mem_specs(d, nt, tm, layer) + [
            pl.BlockSpec((1, d), lambda b, i: (0, 0)),
            resident((d, 2 * ff)),
            resident((ff, d)),
        ],
        out_specs=pl.BlockSpec((tm, d), lambda b, i: (b * nt + i, 0)),
        out_shape=jax.ShapeDtypeStruct((m, d), F32),
        scratch_shapes=[pltpu.VMEM((tm, d), BF16)],
        compiler_params=_params("parallel", "parallel"),
        name="mem_ffn",
    )(x, g_mem.reshape(1, d), w_q, kv_all, kv_all, w_o, g_ffn.reshape(1, d), w_gu, w_down)


MOE_TM = 512
MOE_TR = 512
MOE_HALF = MOE_TM // 2
MOE_ALIGN = 16
MOE_TF = 512
XS_W = D_MODEL + LANES
MOE_ZR = MOE_TR + MOE_HALF


def _tile_rank(st):
    tm = st.shape[1]
    before = (lax.broadcasted_iota(jnp.int32, (tm, tm), 0)
              < lax.broadcasted_iota(jnp.int32, (tm, tm), 1))
    return jnp.dot(st.astype(BF16), before.astype(BF16), preferred_element_type=F32)


def _one_hot_rows(rank, st, e, half):
    tm = rank.shape[1]
    rows = lax.broadcasted_iota(jnp.int32, (MOE_HALF, tm), 0).astype(F32) + float(half * MOE_HALF)
    hit = (rank[e:e + 1] == rows) & (st[e:e + 1] > 0.0)
    return jnp.where(hit, 1.0, 0.0).astype(BF16)


def _one_hot_all(rank, st):
    return jnp.concatenate([_one_hot_rows(rank, st, e, 0) for e in range(N_EXPERTS)], axis=0)


def _dispatch_kernel(base_ref, n_ref, tail_ref, hn_ref, ct_ref, st_ref, xs_ref, stage, stage2, zbuf,
                     sem, sem2, zsem):
    i = pl.program_id(0)
    last = pl.num_programs(0) - 1
    slot = i % 2

    @pl.when(i == 0)
    def _():
        zbuf[...] = jnp.zeros_like(zbuf)

        def zero(row, nrows):
            cp = pltpu.make_async_copy(zbuf.at[pl.ds(0, nrows)], xs_ref.at[pl.ds(row, nrows)], zsem)
            cp.start()
            cp.wait()

        for e in range(N_EXPERTS):
            zero(pl.multiple_of(tail_ref[e], MOE_ALIGN), MOE_ZR)

        def body(b, carry):
            zero(pl.multiple_of(b * MOE_TR, MOE_TR), MOE_TR)
            return carry

        lax.fori_loop(tail_ref[N_EXPERTS], xs_ref.shape[0] // MOE_TR, body, 0)

    hn = hn_ref[...]
    ct = ct_ref[...]
    st = st_ref[...]
    rank = _tile_rank(st)
    g_hi = ct.astype(BF16)
    g_lo = (ct - g_hi.astype(F32)).astype(BF16)
    gm = jnp.concatenate(
        [g_hi, g_lo, jnp.zeros((LANES - 2 * N_EXPERTS, ct.shape[1]), BF16)], axis=0)
    nt_dims = (((1,), (1,)), ((), ()))

    def copy(step, e):
        row = pl.multiple_of(base_ref[step * N_EXPERTS + e], MOE_ALIGN)
        return pltpu.make_async_copy(stage.at[step % 2, pl.ds(e * MOE_HALF, MOE_HALF)],
                                     xs_ref.at[pl.ds(row, MOE_HALF)], sem.at[step % 2, e])

    def copy2(e):
        row = pl.multiple_of(base_ref[i * N_EXPERTS + e] + MOE_HALF, MOE_ALIGN)
        return pltpu.make_async_copy(stage2.at[e], xs_ref.at[pl.ds(row, MOE_HALF)], sem2.at[e])

    def drain(step):
        for e in range(N_EXPERTS):
            copy(step, e).wait()

    p = _one_hot_all(rank, st)
    stage[slot, :, :D_MODEL] = jnp.dot(p, hn, preferred_element_type=F32).astype(BF16)
    stage[slot, :, D_MODEL:] = lax.dot_general(p, gm, nt_dims, preferred_element_type=F32).astype(BF16)
    pl.when(i >= 1)(lambda: drain(i - 1))
    for e in range(N_EXPERTS):
        copy(i, e).start()

    def fill2(e):
        p2 = _one_hot_rows(rank, st, e, 1)
        stage2[e, :, :D_MODEL] = jnp.dot(p2, hn, preferred_element_type=F32).astype(BF16)
        stage2[e, :, D_MODEL:] = lax.dot_general(p2, gm, nt_dims, preferred_element_type=F32).astype(BF16)
        copy2(e).start()

    for e in range(N_EXPERTS):
        pl.when(n_ref[i * N_EXPERTS + e] > MOE_HALF)(functools.partial(fill2, e))
    for e in range(N_EXPERTS):
        pl.when(n_ref[i * N_EXPERTS + e] > MOE_HALF)(lambda e=e: copy2(e).wait())

    pl.when(i == last)(lambda: drain(i))


def _expert_kernel(be_ref, nu_ref, x_ref, wg_ref, wu_ref, wd_ref, o_ref, acc_ref):
    b = pl.program_id(0)

    @pl.when(b < nu_ref[0])
    def _():
        e = be_ref[b]
        xe = x_ref[...]
        h = xe[:, :D_MODEL]
        ext = xe[:, D_MODEL:].astype(F32)
        lane = lax.broadcasted_iota(jnp.int32, ext.shape, 1)
        g = jnp.sum(jnp.where((lane == e) | (lane == e + N_EXPERTS), ext, 0.0), axis=1, keepdims=True)
        ff = wd_ref.shape[1]
        for c in range(ff // MOE_TF):
            lo = c * MOE_TF
            gate = jnp.dot(h, wg_ref[0, :, lo:lo + MOE_TF], preferred_element_type=F32)
            up = jnp.dot(h, wu_ref[0, :, lo:lo + MOE_TF], preferred_element_type=F32)
            act = (gate * jax.nn.sigmoid(gate) * up * g).astype(BF16)
            part = jnp.dot(act, wd_ref[0, lo:lo + MOE_TF, :], preferred_element_type=F32)
            if c == 0:
                acc_ref[...] = part
            else:
                acc_ref[...] += part
        o_ref[...] = acc_ref[...].astype(o_ref.dtype)

    @pl.when(b >= nu_ref[0])
    def _():
        o_ref[...] = jnp.zeros_like(o_ref)


def _combine_kernel(base_ref, n_ref, x_ref, st_ref, ys_ref, gout_ref, o_ref, ybuf, ybuf2, sem, sem2,
                    *, out_norm):
    i = pl.program_id(0)
    slot = i % 2

    def copy(step, e):
        row = pl.multiple_of(base_ref[step * N_EXPERTS + e], MOE_ALIGN)
        return pltpu.make_async_copy(ys_ref.at[pl.ds(row, MOE_HALF)],
                                     ybuf.at[step % 2, pl.ds(e * MOE_HALF, MOE_HALF)],
                                     sem.at[step % 2, e])

    def copy2(e):
        row = pl.multiple_of(base_ref[i * N_EXPERTS + e] + MOE_HALF, MOE_ALIGN)
        return pltpu.make_async_copy(ys_ref.at[pl.ds(row, MOE_HALF)], ybuf2.at[e], sem2.at[e])

    def fetch(step):
        for e in range(N_EXPERTS):
            copy(step, e).start()

    pl.when(i == 0)(lambda: fetch(i))
    pl.when(i + 1 < pl.num_programs(0))(lambda: fetch(i + 1))
    for e in range(N_EXPERTS):
        pl.when(n_ref[i * N_EXPERTS + e] > MOE_HALF)(lambda e=e: copy2(e).start())

    st = st_ref[...]
    rank = _tile_rank(st)
    p = _one_hot_all(rank, st)
    for e in range(N_EXPERTS):
        copy(i, e).wait()
    tn_dims = (((0,), (0,)), ((), ()))
    o_ref[...] = x_ref[...] + lax.dot_general(p, ybuf[slot], tn_dims, preferred_element_type=F32)

    def gather2(e):
        copy2(e).wait()
        p2 = _one_hot_rows(rank, st, e, 1)
        o_ref[...] += lax.dot_general(p2, ybuf2[e], tn_dims, preferred_element_type=F32)

    for e in range(N_EXPERTS):
        pl.when(n_ref[i * N_EXPERTS + e] > MOE_HALF)(functools.partial(gather2, e))
    if out_norm:
        o_ref[...] = _rms(o_ref[...], gout_ref[...])


def _moe_layer(x, g_mem, w_q, kv_all, layer, w_o, g_ffn, w_router, w_gu, w_down, moe_layer, bsz, seq,
               g_out=None):
    m, d = x.shape
    ff = w_down.shape[2]
    nt = m // MOE_TM
    ntb = seq // MOE_TM
    i32 = jnp.int32
    x, hn, ct, st, cnt = pl.pallas_call(
        _mem_route_kernel,
        grid=(bsz, ntb),
        in_specs=_mem_specs(d, ntb, MOE_TM, layer) + [
            pl.BlockSpec((1, d), lambda b, i: (0, 0)),
            pl.BlockSpec((N_EXPERTS, d), lambda b, i: (0, 0)),
        ],
        out_specs=[
            pl.BlockSpec((MOE_TM, d), lambda b, i: (b * ntb + i, 0)),
            pl.BlockSpec((MOE_TM, d), lambda b, i: (b * ntb + i, 0)),
            pl.BlockSpec((N_EXPERTS, MOE_TM), lambda b, i: (0, b * ntb + i)),
            pl.BlockSpec((N_EXPERTS, MOE_TM), lambda b, i: (0, b * ntb + i)),
            pl.BlockSpec((1, N_EXPERTS, LANES), lambda b, i: (b * ntb + i, 0, 0)),
        ],
        out_shape=[
            jax.ShapeDtypeStruct((m, d), F32),
            jax.ShapeDtypeStruct((m, d), BF16),
            jax.ShapeDtypeStruct((N_EXPERTS, m), F32),
            jax.ShapeDtypeStruct((N_EXPERTS, m), F32),
            jax.ShapeDtypeStruct((nt, N_EXPERTS, LANES), F32),
        ],
        scratch_shapes=[pltpu.VMEM((MOE_TM, d), BF16)],
        compiler_params=_params("parallel", "parallel"),
        name="mem_route",
    )(x, g_mem.reshape(1, d), w_q, kv_all, kv_all, w_o, g_ffn.reshape(1, d), w_router.T.astype(F32))

    n = cnt[:, :, 0].astype(i32)
    chunk = (n + MOE_ALIGN - 1) // MOE_ALIGN * MOE_ALIGN
    seg = (jnp.sum(chunk, axis=0) + MOE_HALF + MOE_TR - 1) // MOE_TR * MOE_TR
    seg_off = jnp.cumsum(seg) - seg
    base = (seg_off[None, :] + jnp.cumsum(chunk, axis=0) - chunk).reshape(-1).astype(i32)
    n_flat = n.reshape(-1)
    bound = 2 * m + nt * N_EXPERTS * (MOE_ALIGN - 1) + N_EXPERTS * (MOE_HALF + MOE_TR - 1)
    nb = -(-bound // MOE_TR)
    blk_end = jnp.cumsum(seg // MOE_TR)
    blk_expert = jnp.minimum(jnp.sum(jnp.arange(nb, dtype=i32)[:, None] >= blk_end[None, :], axis=1),
                             N_EXPERTS - 1).astype(i32)
    n_used = blk_end[-1:].astype(i32)
    rows = nb * MOE_TR
    tail = jnp.concatenate([jnp.maximum(seg_off + seg - MOE_ZR, 0), n_used]).astype(i32)

    tile_spec = lambda shape, imap: pl.BlockSpec(shape, imap)
    xs = pl.pallas_call(
        _dispatch_kernel,
        grid_spec=pltpu.PrefetchScalarGridSpec(
            num_scalar_prefetch=3,
            grid=(nt,),
            in_specs=[
                tile_spec((MOE_TM, d), lambda i, b, c, t: (i, 0)),
                tile_spec((N_EXPERTS, MOE_TM), lambda i, b, c, t: (0, i)),
                tile_spec((N_EXPERTS, MOE_TM), lambda i, b, c, t: (0, i)),
            ],
            out_specs=pl.BlockSpec(memory_space=pl.ANY),
            scratch_shapes=[
                pltpu.VMEM((2, N_EXPERTS * MOE_HALF, XS_W), BF16),
                pltpu.VMEM((N_EXPERTS, MOE_HALF, XS_W), BF16),
                pltpu.VMEM((MOE_ZR, XS_W), BF16),
                pltpu.SemaphoreType.DMA((2, N_EXPERTS)),
                pltpu.SemaphoreType.DMA((N_EXPERTS,)),
                pltpu.SemaphoreType.DMA(()),
            ],
        ),
        out_shape=jax.ShapeDtypeStruct((rows, XS_W), BF16),
        compiler_params=_params("arbitrary"),
        name="moe_dispatch",
    )(base, n_flat, tail, hn, ct, st)

    ys = pl.pallas_call(
        _expert_kernel,
        grid_spec=pltpu.PrefetchScalarGridSpec(
            num_scalar_prefetch=2,
            grid=(nb,),
            in_specs=[
                pl.BlockSpec((MOE_TR, XS_W), lambda b, be, nu: (b, 0)),
                pl.BlockSpec((None, 1, d, ff), lambda b, be, nu: (moe_layer, be[b], 0, 0)),
                pl.BlockSpec((None, 1, d, ff), lambda b, be, nu: (moe_layer, be[b], 0, 1)),
                pl.BlockSpec((None, 1, ff, d), lambda b, be, nu: (moe_layer, be[b], 0, 0)),
            ],
            out_specs=pl.BlockSpec((MOE_TR, d), lambda b, be, nu: (b, 0)),
            scratch_shapes=[pltpu.VMEM((MOE_TR, d), F32)],
        ),
        out_shape=jax.ShapeDtypeStruct((rows, d), BF16),
        compiler_params=_params("arbitrary"),
        name="moe_experts",
    )(blk_expert, n_used, xs, w_gu, w_gu, w_down)

    out_norm = g_out is not None
    g_out = jnp.ones((d,), F32) if g_out is None else g_out
    return pl.pallas_call(
        functools.partial(_combine_kernel, out_norm=out_norm),
        grid_spec=pltpu.PrefetchScalarGridSpec(
            num_scalar_prefetch=2,
            grid=(nt,),
            in_specs=[
                tile_spec((MOE_TM, d), lambda i, b, c: (i, 0)),
                tile_spec((N_EXPERTS, MOE_TM), lambda i, b, c: (0, i)),
                pl.BlockSpec(memory_space=pl.ANY),
                tile_spec((1, d), lambda i, b, c: (0, 0)),
            ],
            out_specs=tile_spec((MOE_TM, d), lambda i, b, c: (i, 0)),
            scratch_shapes=[
                pltpu.VMEM((2, N_EXPERTS * MOE_HALF, d), BF16),
                pltpu.VMEM((N_EXPERTS, MOE_HALF, d), BF16),
                pltpu.SemaphoreType.DMA((2, N_EXPERTS)),
                pltpu.SemaphoreType.DMA((N_EXPERTS,)),
            ],
        ),
        out_shape=jax.ShapeDtypeStruct((m, d), F32),
        compiler_params=_params("arbitrary"),
        name="moe_combine",
    )(base, n_flat, x, st, ys, g_out.reshape(1, d).astype(F32))


def kernel(x, mem, norm_mix, norm_mem, norm_ffn, mem_norm, final_norm, s5_w_in, s5_lam_re, s5_lam_im, s5_log_dt, s5_b_re, s5_b_im, s5_c_re, s5_c_im, s5_d, s5_w_glu, att_w_qkv, att_w_o, att_rel_bias, lru_w_in, lru_conv_w, lru_conv_b, lru_w_a, lru_b_a, lru_w_x, lru_b_x, lru_lam, lru_w_out, mem_w_q, mem_w_kv, mem_w_o, ffn_w_gu, ffn_w_down, moe_w_router, moe_w_gu, moe_w_down):
    bsz, seq, d = x.shape
    x = x.reshape(bsz * seq, d).astype(F32)

    w_kv_all = jnp.concatenate([mem_w_kv[i] for i in range(DEPTH)], axis=1).astype(BF16)
    kv_all = _norm_linear(mem.reshape(bsz * N_MEM, d).astype(F32), mem_norm, w_kv_all, BF16,
                          tm=bsz * N_MEM, tn=2 * d)

    mem_wq, mem_wo = mem_w_q.astype(BF16), mem_w_o.astype(BF16)
    ffn_wgu, ffn_wd = ffn_w_gu.astype(BF16), ffn_w_down.astype(BF16)
    moe_wgu, moe_wd = moe_w_gu.astype(BF16), moe_w_down.astype(BF16)
    for i in range(DEPTH):
        kind, j = i % 3, i // 3
        if kind == 0:
            x = _s5_layer(x, norm_mix[i], s5_w_in[j], s5_lam_re[j], s5_lam_im[j], s5_log_dt[j],
                          s5_b_re[j], s5_b_im[j], s5_c_re[j], s5_c_im[j], s5_d[j], s5_w_glu[j],
                          bsz, seq)
        elif kind == 1:
            x = _attn_layer(x, norm_mix[i], att_w_qkv[j], att_w_o[j], att_rel_bias[j], bsz, seq)
        else:
            x = _lru_layer(x, norm_mix[i], lru_w_in[j], lru_conv_w[j], lru_conv_b[j], lru_w_a[j],
                           lru_b_a[j], lru_w_x[j], lru_b_x[j], lru_lam[j], lru_w_out[j], bsz, seq)
        if i % 2 == 0:
            x = _mem_ffn_layer(x, norm_mem[i], mem_wq, kv_all, i, mem_wo, norm_ffn[i], ffn_wgu, ffn_wd,
                               i // 2, bsz, seq)
        else:
            x = _moe_layer(x, norm_mem[i], mem_wq, kv_all, i, mem_wo, norm_ffn[i], moe_w_router[i // 2],
                           moe_wgu, moe_wd, i // 2, bsz, seq,
                           g_out=final_norm if i == DEPTH - 1 else None)
    return x.reshape(bsz, seq, d)
```

```python
import functools
import math

import jax
import jax.numpy as jnp
from jax import lax
from jax.experimental import pallas as pl
from jax.experimental.pallas import tpu as pltpu

F32 = jnp.float32
BF16 = jnp.bfloat16
HIGHEST = lax.Precision.HIGHEST

D_MODEL = 1024
DEPTH = 4
CHUNK = 64
N_MEM = 256
EPS = 1e-6

S5_GROUP = 16
S5_GROUPS = D_MODEL // S5_GROUP
S5_STATE = 64

ATT_HEADS = 16
ATT_HEAD_DIM = D_MODEL // ATT_HEADS
LEFT_CHUNKS = 8
MAX_REL = 128
ATT_TQ = 4 * CHUNK
ATT_LEFT = LEFT_CHUNKS * CHUNK
ATT_TK = ATT_LEFT + ATT_TQ
NEG_INF = -1e30
LOG2E = math.log2(math.e)

LRU_BLOCKS = 8
LRU_BLOCK_W = D_MODEL // LRU_BLOCKS
CONV_W = 4
LRU_C = 8.0
LRU_T = 512
SUBLANES = 8
LANES = 128

MEM_HEADS = 4
MEM_HEAD_DIM = D_MODEL // MEM_HEADS

N_EXPERTS = 8
FFN_TF = 256

VMEM_LIMIT = 56 * 1024 * 1024


def _params(*sem):
    return pltpu.CompilerParams(dimension_semantics=sem, vmem_limit_bytes=VMEM_LIMIT)


def _rms(x, g):
    ms = jnp.mean(x * x, axis=-1, keepdims=True)
    return x * lax.rsqrt(ms + EPS) * g


def _norm_linear_kernel(x_ref, g_ref, w_ref, o_ref, hn_ref):
    @pl.when(pl.program_id(1) == 0)
    def _():
        hn_ref[...] = _rms(x_ref[...], g_ref[...]).astype(BF16)

    o_ref[...] = jnp.dot(hn_ref[...], w_ref[...], preferred_element_type=F32).astype(o_ref.dtype)


def _norm_linear(x, g, w, out_dtype, tm=512, tn=None):
    m, d = x.shape
    n = w.shape[1]
    tn = n if tn is None else tn
    return pl.pallas_call(
        _norm_linear_kernel,
        grid=(m // tm, n // tn),
        in_specs=[
            pl.BlockSpec((tm, d), lambda i, j: (i, 0)),
            pl.BlockSpec((1, d), lambda i, j: (0, 0)),
            pl.BlockSpec((d, tn), lambda i, j: (0, j)),
        ],
        out_specs=pl.BlockSpec((tm, tn), lambda i, j: (i, j)),
        out_shape=jax.ShapeDtypeStruct((m, n), out_dtype),
        scratch_shapes=[pltpu.VMEM((tm, d), BF16)],
        compiler_params=_params("parallel", "arbitrary"),
        name="norm_linear",
    )(x, g.reshape(1, d), w)


def _s5_operators(lam_re, lam_im, log_dt, b_re, b_im, c_re, c_im, d_skip, n_chunks):
    g, n, p = S5_GROUPS, S5_STATE, S5_GROUP
    lr = lam_re.astype(F32)
    li = lam_im.astype(F32)
    dt = jnp.exp(log_dt.astype(F32))[:, None]
    mag = jnp.exp(lr * dt)
    ar = mag * jnp.cos(li * dt)
    ai = mag * jnp.sin(li * dt)
    den = lr * lr + li * li
    fr = ((ar - 1.0) * lr + ai * li) / den
    fi = (ai * lr - (ar - 1.0) * li) / den
    bbr = fr[..., None] * b_re - fi[..., None] * b_im
    bbi = fr[..., None] * b_im + fi[..., None] * b_re
    bbt = jnp.concatenate([bbr, bbi], axis=1).transpose(0, 2, 1)

    kk = jnp.arange(CHUNK + 1, dtype=F32)
    mag_k = jnp.exp((lr * dt)[..., None] * kk)
    pr = mag_k * jnp.cos((li * dt)[..., None] * kk)
    pi = mag_k * jnp.sin((li * dt)[..., None] * kk)
    pk = jnp.concatenate([pr[..., :CHUNK], pi[..., :CHUNK], pr[..., 1:], pi[..., 1:],
                          pr[..., CHUNK - 1::-1], pi[..., CHUNK - 1::-1]], axis=-1)
    cb = jnp.concatenate([c_re.transpose(0, 2, 1), c_im.transpose(0, 2, 1), bbr, bbi], axis=-1)
    dq = jnp.broadcast_to(d_skip.reshape(g, p, 1), (g, p, LANES))
    qr, qi = [pr[..., CHUNK]], [pi[..., CHUNK]]
    for _ in range(int(math.log2(n_chunks)) - 1):
        qr, qi = qr + [qr[-1] * qr[-1] - qi[-1] * qi[-1]], qi + [2.0 * qr[-1] * qi[-1]]
    levels = len(qr)
    qr, qi = jnp.stack(qr), jnp.stack(qi)
    m1 = jnp.concatenate([qr, qr], axis=-1)
    m2 = jnp.concatenate([-qi, qi], axis=-1)
    ap = jnp.stack([m1, m2], axis=-1).transpose(1, 2, 0, 3).reshape(g, 2 * n, 2 * levels)
    return bbt, pk, cb, dq, ap


def _s5_in_kernel(x3_ref, g_ref, wt_ref, u3_ref, xbuf, ubuf, sem_in, sem_out):
    s = pl.program_id(0)
    last = pl.num_programs(0) - 1
    slot = s % 2

    def load(step):
        return pltpu.make_async_copy(x3_ref.at[pl.ds(0, x3_ref.shape[0]), step], xbuf.at[step % 2],
                                     sem_in.at[step % 2])

    def store(step):
        return pltpu.make_async_copy(ubuf.at[step % 2], u3_ref.at[pl.ds(0, u3_ref.shape[0]), step],
                                     sem_out.at[step % 2])

    pl.when(s == 0)(lambda: load(s).start())
    pl.when(s < last)(lambda: load(s + 1).start())
    load(s).wait()
    pl.when(s >= 2)(lambda: store(s - 2).wait())
    hn = _rms(xbuf[slot], g_ref[...]).astype(BF16)
    ubuf[slot] = lax.dot_general(wt_ref[...], hn, (((1,), (1,)), ((), ())), preferred_element_type=F32)
    store(s).start()

    @pl.when(s == last)
    def _():
        store(s - 1).wait()
        store(s).wait()


def _tile_lanes(a, times):
    a2 = jnp.concatenate([a, a], axis=1)
    return jnp.concatenate([a2] * (times // 2), axis=1)


def _spread_lanes(a):
    r, c = a.shape
    low = lax.broadcasted_iota(jnp.int32, (r, LANES), 1) < CHUNK
    pairs = [jnp.where(low, jnp.broadcast_to(a[:, 2 * j:2 * j + 1], (r, LANES)),
                       jnp.broadcast_to(a[:, 2 * j + 1:2 * j + 2], (r, LANES))) for j in range(c // 2)]
    return jnp.concatenate(pairs, axis=1)


def _ssm_kernel(u_ref, bbt_ref, pk_ref, cb_ref, dq_ref, ap_ref, o_ref, toet_ref, *, n_chunks):
    n = S5_STATE
    rows = S5_GROUP * CHUNK
    ncol = u_ref.shape[-1]
    u = u_ref[0].reshape(rows, ncol).astype(BF16)

    pk = pk_ref[0]
    cb = cb_ref[0]
    cr, ci, br, bi = [_spread_lanes(cb[:, S5_GROUP * k:S5_GROUP * (k + 1)]) for k in range(4)]
    p0r, p0i, p1r, p1i, rr, ri = [_tile_lanes(pk[:, CHUNK * k:CHUNK * (k + 1)], S5_GROUP) for k in range(6)]
    lk = jnp.concatenate([cr * p0r - ci * p0i, -(cr * p0i + ci * p0r)], axis=0)
    cinjt = jnp.concatenate([cr * p1r - ci * p1i, -(cr * p1i + ci * p1r)], axis=0).astype(BF16)
    bm = jnp.concatenate([rr * br - ri * bi, rr * bi + ri * br], axis=0).astype(BF16)

    kern = jnp.dot(bbt_ref[0], lk, preferred_element_type=F32, precision=HIGHEST)
    lane = lax.broadcasted_iota(jnp.int32, (S5_GROUP, rows), 1)
    own_lag0 = lane == lax.broadcasted_iota(jnp.int32, (S5_GROUP, rows), 0) * CHUNK
    kern = kern + jnp.where(own_lag0, jnp.concatenate([dq_ref[0]] * (rows // LANES), axis=1), 0.0)
    causal = (lax.broadcasted_iota(jnp.int32, (CHUNK, rows), 1) % CHUNK
              >= lax.broadcasted_iota(jnp.int32, (CHUNK, rows), 0))
    for p in range(S5_GROUP):
        base = jnp.broadcast_to(kern[p:p + 1, :], (CHUNK, rows))
        rolled = pltpu.roll(base, 0, 1, stride=1, stride_axis=0)
        toet_ref[p * CHUNK:(p + 1) * CHUNK, :] = jnp.where(causal, rolled, 0.0).astype(BF16)
    tn_dims = (((0,), (0,)), ((), ()))
    y = lax.dot_general(toet_ref[...], u, tn_dims, preferred_element_type=F32)
    x = jnp.dot(bm, u, preferred_element_type=F32)
    ap = ap_ref[0]
    col = lax.broadcasted_iota(jnp.int32, x.shape, 1) % n_chunks
    for k in range(int(math.log2(n_chunks))):
        sh = 1 << k
        xs = jnp.where(col >= sh, pltpu.roll(x, sh, 1), 0.0)
        xsw = jnp.concatenate([xs[n:], xs[:n]], axis=0)
        x = x + ap[:, 2 * k:2 * k + 1] * xs + ap[:, 2 * k + 1:2 * k + 2] * xsw
    h0 = jnp.where(col >= 1, pltpu.roll(x, 1, 1), 0.0)
    h0_hi = h0.astype(BF16)
    h0_lo = (h0 - h0_hi.astype(F32)).astype(BF16)
    y = y + lax.dot_general(cinjt, h0_hi, tn_dims, preferred_element_type=F32)
    y = y + lax.dot_general(cinjt, h0_lo, tn_dims, preferred_element_type=F32)
    o_ref[0] = jax.nn.gelu(y).reshape(S5_GROUP, CHUNK, ncol)


def _s5_out_kernel(v3_ref, x3_ref, w_ref, o3_ref, vbuf, xbuf, obuf, sem_v, sem_x, sem_o):
    s = pl.program_id(0)
    last = pl.num_programs(0) - 1
    slot = s % 2

    def load_v(step):
        return pltpu.make_async_copy(v3_ref.at[pl.ds(0, v3_ref.shape[0]), step], vbuf.at[step % 2],
                                     sem_v.at[step % 2])

    def load_x(step):
        return pltpu.make_async_copy(x3_ref.at[pl.ds(0, x3_ref.shape[0]), step], xbuf.at[step % 2],
                                     sem_x.at[step % 2])

    def store(step):
        return pltpu.make_async_copy(obuf.at[step % 2], o3_ref.at[pl.ds(0, o3_ref.shape[0]), step],
                                     sem_o.at[step % 2])

    def load(step):
        load_v(step).start()
        load_x(step).start()

    pl.when(s == 0)(lambda: load(s))
    pl.when(s < last)(lambda: load(s + 1))
    load_v(s).wait()
    load_x(s).wait()
    pl.when(s >= 2)(lambda: store(s - 2).wait())
    z = lax.dot_general(vbuf[slot].astype(BF16), w_ref[...], (((0,), (0,)), ((), ())),
                        preferred_element_type=F32)
    d = xbuf.shape[-1]
    obuf[slot] = xbuf[slot] + z[:, :d] * jax.nn.sigmoid(z[:, d:])
    store(s).start()

    @pl.when(s == last)
    def _():
        store(s - 1).wait()
        store(s).wait()


def _s5_layer(x, g_norm, w_in, lam_re, lam_im, log_dt, b_re, b_im, c_re, c_im, d_skip, w_glu,
              bsz, seq):
    m, d = x.shape
    n_chunks = seq // CHUNK
    ncol = bsz * n_chunks
    rows = S5_GROUP * CHUNK
    bbt, pk, cb, dq, ap = _s5_operators(lam_re, lam_im, log_dt, b_re, b_im, c_re, c_im, d_skip,
                                        n_chunks)
    x3 = x.reshape(ncol, CHUNK, d)
    hbm = pl.BlockSpec(memory_space=pl.ANY)
    u3 = pl.pallas_call(
        _s5_in_kernel,
        grid=(CHUNK,),
        in_specs=[hbm, pl.BlockSpec((1, d), lambda s: (0, 0)), pl.BlockSpec((d, d), lambda s: (0, 0))],
        out_specs=hbm,
        out_shape=jax.ShapeDtypeStruct((d, CHUNK, ncol), F32),
        scratch_shapes=[
            pltpu.VMEM((2, ncol, d), F32),
            pltpu.VMEM((2, d, ncol), F32),
            pltpu.SemaphoreType.DMA((2,)),
            pltpu.SemaphoreType.DMA((2,)),
        ],
        compiler_params=_params("arbitrary"),
        name="s5_in",
    )(x3, g_norm.reshape(1, d), w_in.T.astype(BF16))
    blk4 = (1, S5_GROUP, CHUNK, ncol)
    per_group = lambda a: pl.BlockSpec((1,) + a.shape[1:], lambda g: (g, 0, 0))
    v4 = pl.pallas_call(
        functools.partial(_ssm_kernel, n_chunks=n_chunks),
        grid=(S5_GROUPS,),
        in_specs=[pl.BlockSpec(blk4, lambda g: (g, 0, 0, 0))]
        + [per_group(a) for a in (bbt, pk, cb, dq, ap)],
        out_specs=pl.BlockSpec(blk4, lambda g: (g, 0, 0, 0)),
        out_shape=jax.ShapeDtypeStruct((S5_GROUPS, S5_GROUP, CHUNK, ncol), F32),
        scratch_shapes=[pltpu.VMEM((rows, rows), BF16)],
        compiler_params=_params("parallel"),
        name="s5_ssm",
    )(u3.reshape(S5_GROUPS, S5_GROUP, CHUNK, ncol), bbt, pk, cb, dq, ap)
    o3 = pl.pallas_call(
        _s5_out_kernel,
        grid=(CHUNK,),
        in_specs=[hbm, hbm, pl.BlockSpec((d, 2 * d), lambda s: (0, 0))],
        out_specs=hbm,
        out_shape=jax.ShapeDtypeStruct((ncol, CHUNK, d), F32),
        scratch_shapes=[
            pltpu.VMEM((2, d, ncol), F32),
            pltpu.VMEM((2, ncol, d), F32),
            pltpu.VMEM((2, ncol, d), F32),
            pltpu.SemaphoreType.DMA((2,)),
            pltpu.SemaphoreType.DMA((2,)),
            pltpu.SemaphoreType.DMA((2,)),
        ],
        compiler_params=_params("arbitrary"),
        name="s5_out",
    )(v4.reshape(d, CHUNK, ncol), x3, w_glu.astype(BF16))
    return o3.reshape(m, d)


ATT_PERIOD = ATT_TQ + ATT_TK


def _attn_bias_seed(rel_bias):
    k = jnp.arange(ATT_PERIOD)
    rel = jnp.where(k < ATT_TK, ATT_LEFT - k, ATT_LEFT + ATT_PERIOD - k)
    return jnp.take(rel_bias.astype(F32), jnp.clip(rel, -MAX_REL, MAX_REL) + MAX_REL, axis=1)


def _attn_kernel(x_ref, q_ref, k0_ref, k1_ref, k2_ref, v0_ref, v1_ref, v2_ref, seed_ref, wo_ref,
                 o_ref, bias_ref, ocat_ref):
    i = pl.program_id(1)

    @pl.when((pl.program_id(0) == 0) & (i == 0))
    def _():
        r = lax.broadcasted_iota(jnp.int32, (ATT_TQ, ATT_TK), 0)
        j = lax.broadcasted_iota(jnp.int32, (ATT_TQ, ATT_TK), 1)
        dchunk = (r + ATT_LEFT) // CHUNK - j // CHUNK
        valid = (dchunk >= 0) & (dchunk <= LEFT_CHUNKS)
        for h in range(ATT_HEADS):
            seed = jnp.broadcast_to(seed_ref[h:h + 1, :], (ATT_TQ, ATT_PERIOD))
            rolled = pltpu.roll(seed, 0, 1, stride=1, stride_axis=0)
            bias_ref[h] = jnp.where(valid, rolled[:, :ATT_TK] * LOG2E, NEG_INF)

    lane = lax.broadcasted_iota(jnp.int32, (1, LANES), 1)
    scale = ATT_HEAD_DIM ** -0.5
    heads_per = LANES // ATT_HEAD_DIM

    def attend(mask_pad):
        if mask_pad:
            col = lax.broadcasted_iota(jnp.int32, (ATT_TQ, ATT_TK), 1)
            pad = col < ATT_LEFT - i * ATT_TQ
        for hp in range(D_MODEL // LANES):
            sl = slice(hp * LANES, (hp + 1) * LANES)
            q = q_ref[:, sl] * scale
            k = jnp.concatenate([k0_ref[:, sl], k1_ref[:, sl], k2_ref[:, sl]], axis=0)
            v = jnp.concatenate([v0_ref[:, sl], v1_ref[:, sl], v2_ref[:, sl]], axis=0)
            acc = jnp.zeros((ATT_TQ, LANES), F32)
            for h in range(heads_per):
                in_head = (lane >= h * ATT_HEAD_DIM) & (lane < (h + 1) * ATT_HEAD_DIM)
                qh = jnp.where(in_head, q, jnp.zeros_like(q))
                s = lax.dot_general(qh, k, (((1,), (1,)), ((), ())), preferred_element_type=F32)
                s = s + bias_ref[hp * heads_per + h]
                if mask_pad:
                    s = jnp.where(pad, NEG_INF, s)
                p = jnp.exp2(s - jnp.max(s, axis=-1, keepdims=True))
                vh = jnp.where(in_head, v, jnp.ones_like(v))
                pv = jnp.dot(p.astype(BF16), vh, preferred_element_type=F32)
                acc = jnp.where(in_head, pv / pltpu.roll(pv, ATT_HEAD_DIM, 1), acc)
            ocat_ref[:, sl] = acc.astype(BF16)

    first_steps = ATT_LEFT // ATT_TQ
    pl.when(i < first_steps)(functools.partial(attend, True))
    pl.when(i >= first_steps)(functools.partial(attend, False))
    o_ref[...] = x_ref[...] + jnp.dot(ocat_ref[...], wo_ref[...], preferred_element_type=F32)


def _attn_layer(x, g_norm, w_qkv, w_o, rel_bias, bsz, seq):
    m, d = x.shape
    w_qkv = jnp.concatenate([w_qkv[:, :d] * LOG2E, w_qkv[:, d:]], axis=1)
    qkv = _norm_linear(x, g_norm, w_qkv.astype(BF16), BF16, tm=1024)
    nq = seq // ATT_TQ

    def rows(col, back):
        return lambda b, i: (b * nq + jnp.maximum(i - back, 0), col)

    blk = (ATT_TQ, d)
    nkb = ATT_TK // ATT_TQ
    return pl.pallas_call(
        _attn_kernel,
        grid=(bsz, nq),
        in_specs=(
            [pl.BlockSpec(blk, rows(0, 0)), pl.BlockSpec(blk, rows(0, 0))]
            + [pl.BlockSpec(blk, rows(1, nkb - 1 - kb)) for kb in range(nkb)]
            + [pl.BlockSpec(blk, rows(2, nkb - 1 - kb)) for kb in range(nkb)]
            + [pl.BlockSpec((ATT_HEADS, ATT_PERIOD), lambda b, i: (0, 0)),
               pl.BlockSpec((d, d), lambda b, i: (0, 0))]
        ),
        out_specs=pl.BlockSpec(blk, rows(0, 0)),
        out_shape=jax.ShapeDtypeStruct((m, d), F32),
        scratch_shapes=[pltpu.VMEM((ATT_HEADS, ATT_TQ, ATT_TK), F32), pltpu.VMEM((ATT_TQ, d), BF16)],
        compiler_params=_params("arbitrary", "arbitrary"),
        name="chunk_attn",
    )(x, *([qkv] * (1 + 2 * nkb)), _attn_bias_seed(rel_bias), w_o.astype(BF16))


def _lru_kernel(x_ref, g_ref, win_ref, cw_ref, cb_ref, wax_ref, ba_ref, bx_ref, lam_ref, wout_ref,
                o_ref, prev_ref, carry_ref, a_s, b_s):
    t, w = x_ref.shape

    @pl.when(pl.program_id(1) == 0)
    def _():
        prev_ref[...] = jnp.zeros_like(prev_ref)
        carry_ref[...] = jnp.zeros_like(carry_ref)

    x = x_ref[...]
    z = jnp.dot(_rms(x, g_ref[...]).astype(BF16), win_ref[...], preferred_element_type=F32)
    gate = z[:, :w]
    xr = z[:, w:]
    xext = jnp.concatenate([prev_ref[...], xr], axis=0)
    cw = cw_ref[...]
    xc = cb_ref[...] + cw[0:1] * xext[SUBLANES - 3:SUBLANES - 3 + t]
    for k in range(1, CONV_W):
        xc = xc + cw[k:k + 1] * xext[SUBLANES - 3 + k:SUBLANES - 3 + k + t]
    prev_ref[...] = xr[t - SUBLANES:]

    xcb = xc.astype(BF16)
    pre_a, pre_x = [], []
    for blk in range(LRU_BLOCKS):
        pre = jnp.dot(xcb[:, blk * LRU_BLOCK_W:(blk + 1) * LRU_BLOCK_W], wax_ref[blk],
                      preferred_element_type=F32)
        pre_a.append(pre[:, :LRU_BLOCK_W])
        pre_x.append(pre[:, LRU_BLOCK_W:])
    r = jax.nn.sigmoid(jnp.concatenate(pre_a, axis=1) + ba_ref[...])
    ig = jax.nn.sigmoid(jnp.concatenate(pre_x, axis=1) + bx_ref[...])
    z = -lam_ref[...]
    softplus = jnp.maximum(z, 0.0) + jnp.log1p(jnp.exp(-jnp.abs(z)))
    log_a = -LRU_C * r * softplus
    a = jnp.exp(log_a)
    b = jnp.sqrt(1.0 - a * a) * (ig * xc)

    a3 = a.reshape(t // SUBLANES, SUBLANES, w)
    b3 = b.reshape(t // SUBLANES, SUBLANES, w)
    sub = lax.broadcasted_iota(jnp.int32, a3.shape, 1)
    for sh in (1, 2, 4):
        a_sh = jnp.where(sub >= sh, pltpu.roll(a3, sh, 1), 1.0)
        b_sh = jnp.where(sub >= sh, pltpu.roll(b3, sh, 1), 0.0)
        b3 = a3 * b_sh + b3
        a3 = a3 * a_sh
    a_s[...] = a3
    b_s[...] = b3

    def body(j, carry):
        hb = b_s[j] + a_s[j] * carry
        b_s[j] = hb
        return hb[SUBLANES - 1:SUBLANES, :]

    carry = lax.fori_loop(0, t // SUBLANES, body, carry_ref[0:1, :])
    carry_ref[0:1, :] = carry
    h = b_s[...].reshape(t, w)
    y = (jax.nn.gelu(gate) * h).astype(BF16)
    o_ref[...] = x + jnp.dot(y, wout_ref[...], preferred_element_type=F32)


def _lru_layer(x, g_norm, w_in, conv_w, conv_b, w_a, b_a, w_x, b_x, lam, w_out, bsz, seq):
    m, d = x.shape
    wax = jnp.concatenate([w_a, w_x], axis=-1).astype(BF16)
    nt = seq // LRU_T
    row = lambda v: v.reshape(1, d).astype(F32)
    vec = pl.BlockSpec((1, d), lambda b, i: (0, 0))
    return pl.pallas_call(
        _lru_kernel,
        grid=(bsz, nt),
        in_specs=[
            pl.BlockSpec((LRU_T, d), lambda b, i: (b * nt + i, 0)),
            vec,
            pl.BlockSpec((d, 2 * d), lambda b, i: (0, 0)),
            pl.BlockSpec((CONV_W, d), lambda b, i: (0, 0)),
            vec,
            pl.BlockSpec(wax.shape, lambda b, i: (0, 0, 0)),
            vec, vec, vec,
            pl.BlockSpec((d, d), lambda b, i: (0, 0)),
        ],
        out_specs=pl.BlockSpec((LRU_T, d), lambda b, i: (b * nt + i, 0)),
        out_shape=jax.ShapeDtypeStruct((m, d), F32),
        scratch_shapes=[
            pltpu.VMEM((SUBLANES, d), F32),
            pltpu.VMEM((SUBLANES, d), F32),
            pltpu.VMEM((LRU_T // SUBLANES, SUBLANES, d), F32),
            pltpu.VMEM((LRU_T // SUBLANES, SUBLANES, d), F32),
        ],
        compiler_params=_params("parallel", "arbitrary"),
        name="rglru",
    )(x, row(g_norm), w_in.astype(BF16), conv_w.astype(F32), row(conv_b), wax, row(b_a), row(b_x),
      row(lam), w_out.astype(BF16))


def _memattn(x, g_ref, wq_ref, k_ref, v_ref, wo_ref, ocat_ref):
    hn = _rms(x, g_ref[...]).astype(BF16)
    q = jnp.dot(hn, wq_ref[...], preferred_element_type=F32) * (MEM_HEAD_DIM ** -0.5)
    q = q.astype(BF16)
    for h in range(MEM_HEADS):
        sl = slice(h * MEM_HEAD_DIM, (h + 1) * MEM_HEAD_DIM)
        s = lax.dot_general(q[:, sl], k_ref[:, sl], (((1,), (1,)), ((), ())),
                            preferred_element_type=F32)
        mx = jnp.max(s, axis=-1, keepdims=True)
        p = jnp.exp(s - mx)
        l = jnp.sum(p, axis=-1, keepdims=True)
        o = jnp.dot(p.astype(BF16), v_ref[:, sl], preferred_element_type=F32) / l
        ocat_ref[:, sl] = o.astype(BF16)
    return x + jnp.dot(ocat_ref[...], wo_ref[...], preferred_element_type=F32)


def _mem_ffn_kernel(x_ref, g_ref, wq_ref, k_ref, v_ref, wo_ref, g2_ref, wgu_ref, wd_ref, o_ref, ocat_ref):
    ff = wd_ref.shape[0]
    x1 = _memattn(x_ref[...], g_ref, wq_ref, k_ref, v_ref, wo_ref, ocat_ref)
    hn = _rms(x1, g2_ref[...]).astype(BF16)
    o_ref[...] = x1
    for f in range(ff // FFN_TF):
        lo = f * FFN_TF
        gate = jnp.dot(hn, wgu_ref[:, lo:lo + FFN_TF], preferred_element_type=F32)
        up = jnp.dot(hn, wgu_ref[:, ff + lo:ff + lo + FFN_TF], preferred_element_type=F32)
        act = gate * jax.nn.sigmoid(gate) * up
        o_ref[...] += jnp.dot(act.astype(BF16), wd_ref[lo:lo + FFN_TF, :], preferred_element_type=F32)


def _mem_route_kernel(x_ref, g_ref, wq_ref, k_ref, v_ref, wo_ref, g2_ref, wrt_ref,
                      o_ref, hn_ref, ct_ref, st_ref, cnt_ref, ocat_ref):
    x1 = _memattn(x_ref[...], g_ref, wq_ref, k_ref, v_ref, wo_ref, ocat_ref)
    o_ref[...] = x1
    hn = _rms(x1, g2_ref[...])
    hn_ref[...] = hn.astype(BF16)
    lg = lax.dot_general(wrt_ref[...], hn, (((1,), (1,)), ((), ())),
                         preferred_element_type=F32, precision=HIGHEST)
    idx = lax.broadcasted_iota(jnp.int32, lg.shape, 0)
    m1 = jnp.max(lg, axis=0, keepdims=True)
    i1 = jnp.min(jnp.where(lg == m1, idx, N_EXPERTS), axis=0, keepdims=True)
    oh1 = idx == i1
    lg2 = jnp.where(oh1, -jnp.inf, lg)
    m2 = jnp.max(lg2, axis=0, keepdims=True)
    i2 = jnp.min(jnp.where(lg2 == m2, idx, N_EXPERTS), axis=0, keepdims=True)
    oh2 = idx == i2
    e2 = jnp.exp(m2 - m1)
    g1 = 1.0 / (1.0 + e2)
    g2 = e2 / (1.0 + e2)
    ct_ref[...] = jnp.where(oh1, g1, 0.0) + jnp.where(oh2, g2, 0.0)
    sel = (oh1 | oh2).astype(F32)
    st_ref[...] = sel
    cnt_ref[0] = jnp.broadcast_to(jnp.sum(sel, axis=1, keepdims=True), (N_EXPERTS, LANES))


def _mem_specs(d, nt, tm, layer):
    wspec = pl.BlockSpec((None, d, d), lambda b, i: (layer, 0, 0))
    return [
        pl.BlockSpec((tm, d), lambda b, i: (b * nt + i, 0)),
        pl.BlockSpec((1, d), lambda b, i: (0, 0)),
        wspec,
        pl.BlockSpec((N_MEM, d), lambda b, i: (b, 2 * layer)),
        pl.BlockSpec((N_MEM, d), lambda b, i: (b, 2 * layer + 1)),
        wspec,
    ]


def _mem_ffn_layer(x, g_mem, w_q, kv_all, layer, w_o, g_ffn, w_gu, w_down, ffn_layer, bsz, seq, tm=1024):
    m, d = x.shape
    nt = seq // tm
    ff = w_down.shape[1]
    resident = lambda shape: pl.BlockSpec((None,) + shape, lambda b, i: (ffn_layer, 0, 0),
                                          pipeline_mode=pl.Buffered(1))
    return pl.pallas_call(
        _mem_ffn_kernel,
        grid=(bsz, nt),
        in_specs=_mem_specs(d, nt, tm, layer) + [
            pl.BlockSpec((1, d), lambda b, i: (0, 0)),
            resident((d, 2 * ff)),
            resident((ff, d)),
        ],
        out_specs=pl.BlockSpec((tm, d), lambda b, i: (b * nt + i, 0)),
        out_shape=jax.ShapeDtypeStruct((m, d), F32),
        scratch_shapes=[pltpu.VMEM((tm, d), BF16)],
        compiler_params=_params("parallel", "parallel"),
        name="mem_ffn",
    )(x, g_mem.reshape(1, d), w_q, kv_all, kv_all, w_o, g_ffn.reshape(1, d), w_gu, w_down)


MOE_TM = 512
MOE_TR = 512
MOE_HALF = MOE_TM // 2
MOE_ALIGN = 16
MOE_TF = 512
XS_W = D_MODEL + LANES
MOE_ZR = MOE_TR + MOE_HALF


def _tile_rank(st):
    tm = st.shape[1]
    before = (lax.broadcasted_iota(jnp.int32, (tm, tm), 0)
              < lax.broadcasted_iota(jnp.int32, (tm, tm), 1))
    return jnp.dot(st.astype(BF16), before.astype(BF16), preferred_element_type=F32)


def _one_hot_rows(rank, st, e, half):
    tm = rank.shape[1]
    rows = lax.broadcasted_iota(jnp.int32, (MOE_HALF, tm), 0).astype(F32) + float(half * MOE_HALF)
    hit = (rank[e:e + 1] == rows) & (st[e:e + 1] > 0.0)
    return jnp.where(hit, 1.0, 0.0).astype(BF16)


def _one_hot_all(rank, st):
    return jnp.concatenate([_one_hot_rows(rank, st, e, 0) for e in range(N_EXPERTS)], axis=0)


def _dispatch_kernel(base_ref, n_ref, tail_ref, hn_ref, ct_ref, st_ref, xs_ref, stage, stage2, zbuf,
                     sem, sem2, zsem):
    i = pl.program_id(0)
    last = pl.num_programs(0) - 1
    slot = i % 2

    @pl.when(i == 0)
    def _():
        zbuf[...] = jnp.zeros_like(zbuf)

        def zero(row, nrows, s):
            return pltpu.make_async_copy(zbuf.at[pl.ds(0, nrows)], xs_ref.at[pl.ds(row, nrows)], zsem.at[s])

        tails = [zero(pl.multiple_of(tail_ref[e], MOE_ALIGN), MOE_ZR, e) for e in range(N_EXPERTS)]
        for cp in tails:
            cp.start()
        for cp in tails:
            cp.wait()

        def body(b, carry):
            cp = zero(pl.multiple_of(b * MOE_TR, MOE_TR), MOE_TR, 0)
            cp.start()
            cp.wait()
            return carry

        lax.fori_loop(tail_ref[N_EXPERTS], xs_ref.shape[0] // MOE_TR, body, 0)

    hn = hn_ref[...]
    ct = ct_ref[...]
    st = st_ref[...]
    rank = _tile_rank(st)
    g_hi = ct.astype(BF16)
    g_lo = (ct - g_hi.astype(F32)).astype(BF16)
    gm = jnp.concatenate(
        [g_hi, g_lo, jnp.zeros((LANES - 2 * N_EXPERTS, ct.shape[1]), BF16)], axis=0)
    nt_dims = (((1,), (1,)), ((), ()))

    def copy(step, e):
        row = pl.multiple_of(base_ref[step * N_EXPERTS + e], MOE_ALIGN)
        return pltpu.make_async_copy(stage.at[step % 2, pl.ds(e * MOE_HALF, MOE_HALF)],
                                     xs_ref.at[pl.ds(row, MOE_HALF)], sem.at[step % 2, e])

    def copy2(e):
        row = pl.multiple_of(base_ref[i * N_EXPERTS + e] + MOE_HALF, MOE_ALIGN)
        return pltpu.make_async_copy(stage2.at[e], xs_ref.at[pl.ds(row, MOE_HALF)], sem2.at[e])

    def drain(step):
        for e in range(N_EXPERTS):
            copy(step, e).wait()

    p = _one_hot_all(rank, st)
    stage[slot, :, :D_MODEL] = jnp.dot(p, hn, preferred_element_type=F32).astype(BF16)
    stage[slot, :, D_MODEL:] = lax.dot_general(p, gm, nt_dims, preferred_element_type=F32).astype(BF16)
    pl.when(i >= 1)(lambda: drain(i - 1))
    for e in range(N_EXPERTS):
        copy(i, e).start()

    def fill2(e):
        p2 = _one_hot_rows(rank, st, e, 1)
        stage2[e, :, :D_MODEL] = jnp.dot(p2, hn, preferred_element_type=F32).astype(BF16)
        stage2[e, :, D_MODEL:] = lax.dot_general(p2, gm, nt_dims, preferred_element_type=F32).astype(BF16)
        copy2(e).start()

    for e in range(N_EXPERTS):
        pl.when(n_ref[i * N_EXPERTS + e] > MOE_HALF)(functools.partial(fill2, e))
    for e in range(N_EXPERTS):
        pl.when(n_ref[i * N_EXPERTS + e] > MOE_HALF)(lambda e=e: copy2(e).wait())

    pl.when(i == last)(lambda: drain(i))


def _expert_kernel(be_ref, nu_ref, x_ref, wg_ref, wu_ref, wd_ref, o_ref, acc_ref):
    b = pl.program_id(0)

    @pl.when(b < nu_ref[0])
    def _():
        e = be_ref[b]
        xe = x_ref[...]
        h = xe[:, :D_MODEL]
        ext = xe[:, D_MODEL:].astype(F32)
        lane = lax.broadcasted_iota(jnp.int32, ext.shape, 1)
        g = jnp.sum(jnp.where((lane == e) | (lane == e + N_EXPERTS), ext, 0.0), axis=1, keepdims=True)
        ff = wd_ref.shape[1]
        for c in range(ff // MOE_TF):
            lo = c * MOE_TF
            gate = jnp.dot(h, wg_ref[0, :, lo:lo + MOE_TF], preferred_element_type=F32)
            up = jnp.dot(h, wu_ref[0, :, lo:lo + MOE_TF], preferred_element_type=F32)
            act = (gate * jax.nn.sigmoid(gate) * up * g).astype(BF16)
            part = jnp.dot(act, wd_ref[0, lo:lo + MOE_TF, :], preferred_element_type=F32)
            if c == 0:
                acc_ref[...] = part
            else:
                acc_ref[...] += part
        o_ref[...] = acc_ref[...].astype(o_ref.dtype)

    @pl.when(b >= nu_ref[0])
    def _():
        o_ref[...] = jnp.zeros_like(o_ref)


def _combine_kernel(base_ref, n_ref, x_ref, st_ref, ys_ref, gout_ref, o_ref, ybuf, ybuf2, sem, sem2,
                    *, out_norm):
    i = pl.program_id(0)
    slot = i % 2

    def copy(step, e):
        row = pl.multiple_of(base_ref[step * N_EXPERTS + e], MOE_ALIGN)
        return pltpu.make_async_copy(ys_ref.at[pl.ds(row, MOE_HALF)],
                                     ybuf.at[step % 2, pl.ds(e * MOE_HALF, MOE_HALF)],
                                     sem.at[step % 2, e])

    def copy2(e):
        row = pl.multiple_of(base_ref[i * N_EXPERTS + e] + MOE_HALF, MOE_ALIGN)
        return pltpu.make_async_copy(ys_ref.at[pl.ds(row, MOE_HALF)], ybuf2.at[e], sem2.at[e])

    def fetch(step):
        for e in range(N_EXPERTS):
            copy(step, e).start()

    pl.when(i == 0)(lambda: fetch(i))
    pl.when(i + 1 < pl.num_programs(0))(lambda: fetch(i + 1))
    for e in range(N_EXPERTS):
        pl.when(n_ref[i * N_EXPERTS + e] > MOE_HALF)(lambda e=e: copy2(e).start())

    st = st_ref[...]
    rank = _tile_rank(st)
    p = _one_hot_all(rank, st)
    for e in range(N_EXPERTS):
        copy(i, e).wait()
    tn_dims = (((0,), (0,)), ((), ()))
    o_ref[...] = x_ref[...] + lax.dot_general(p, ybuf[slot], tn_dims, preferred_element_type=F32)

    def gather2(e):
        copy2(e).wait()
        p2 = _one_hot_rows(rank, st, e, 1)
        o_ref[...] += lax.dot_general(p2, ybuf2[e], tn_dims, preferred_element_type=F32)

    for e in range(N_EXPERTS):
        pl.when(n_ref[i * N_EXPERTS + e] > MOE_HALF)(functools.partial(gather2, e))
    if out_norm:
        o_ref[...] = _rms(o_ref[...], gout_ref[...])


def _moe_layer(x, g_mem, w_q, kv_all, layer, w_o, g_ffn, w_router, w_gu, w_down, moe_layer, bsz, seq,
               g_out=None):
    m, d = x.shape
    ff = w_down.shape[2]
    nt = m // MOE_TM
    ntb = seq // MOE_TM
    i32 = jnp.int32
    x, hn, ct, st, cnt = pl.pallas_call(
        _mem_route_kernel,
        grid=(bsz, ntb),
        in_specs=_mem_specs(d, ntb, MOE_TM, layer) + [
            pl.BlockSpec((1, d), lambda b, i: (0, 0)),
            pl.BlockSpec((N_EXPERTS, d), lambda b, i: (0, 0)),
        ],
        out_specs=[
            pl.BlockSpec((MOE_TM, d), lambda b, i: (b * ntb + i, 0)),
            pl.BlockSpec((MOE_TM, d), lambda b, i: (b * ntb + i, 0)),
            pl.BlockSpec((N_EXPERTS, MOE_TM), lambda b, i: (0, b * ntb + i)),
            pl.BlockSpec((N_EXPERTS, MOE_TM), lambda b, i: (0, b * ntb + i)),
            pl.BlockSpec((1, N_EXPERTS, LANES), lambda b, i: (b * ntb + i, 0, 0)),
        ],
        out_shape=[
            jax.ShapeDtypeStruct((m, d), F32),
            jax.ShapeDtypeStruct((m, d), BF16),
            jax.ShapeDtypeStruct((N_EXPERTS, m), F32),
            jax.ShapeDtypeStruct((N_EXPERTS, m), F32),
            jax.ShapeDtypeStruct((nt, N_EXPERTS, LANES), F32),
        ],
        scratch_shapes=[pltpu.VMEM((MOE_TM, d), BF16)],
        compiler_params=_params("parallel", "parallel"),
        name="mem_route",
    )(x, g_mem.reshape(1, d), w_q, kv_all, kv_all, w_o, g_ffn.reshape(1, d), w_router.T.astype(F32))

    n = cnt[:, :, 0].astype(i32)
    chunk = (n + MOE_ALIGN - 1) // MOE_ALIGN * MOE_ALIGN
    seg = (jnp.sum(chunk, axis=0) + MOE_HALF + MOE_TR - 1) // MOE_TR * MOE_TR
    seg_off = jnp.cumsum(seg) - seg
    base = (seg_off[None, :] + jnp.cumsum(chunk, axis=0) - chunk).reshape(-1).astype(i32)
    n_flat = n.reshape(-1)
    bound = 2 * m + nt * N_EXPERTS * (MOE_ALIGN - 1) + N_EXPERTS * (MOE_HALF + MOE_TR - 1)
    nb = -(-bound // MOE_TR)
    blk_end = jnp.cumsum(seg // MOE_TR)
    blk_expert = jnp.minimum(jnp.sum(jnp.arange(nb, dtype=i32)[:, None] >= blk_end[None, :], axis=1),
                             N_EXPERTS - 1).astype(i32)
    n_used = blk_end[-1:].astype(i32)
    rows = nb * MOE_TR
    tail = jnp.concatenate([jnp.maximum(seg_off + seg - MOE_ZR, 0), n_used]).astype(i32)

    tile_spec = lambda shape, imap: pl.BlockSpec(shape, imap)
    xs = pl.pallas_call(
        _dispatch_kernel,
        grid_spec=pltpu.PrefetchScalarGridSpec(
            num_scalar_prefetch=3,
            grid=(nt,),
            in_specs=[
                tile_spec((MOE_TM, d), lambda i, b, c, t: (i, 0)),
                tile_spec((N_EXPERTS, MOE_TM), lambda i, b, c, t: (0, i)),
                tile_spec((N_EXPERTS, MOE_TM), lambda i, b, c, t: (0, i)),
            ],
            out_specs=pl.BlockSpec(memory_space=pl.ANY),
            scratch_shapes=[
                pltpu.VMEM((2, N_EXPERTS * MOE_HALF, XS_W), BF16),
                pltpu.VMEM((N_EXPERTS, MOE_HALF, XS_W), BF16),
                pltpu.VMEM((MOE_ZR, XS_W), BF16),
                pltpu.SemaphoreType.DMA((2, N_EXPERTS)),
                pltpu.SemaphoreType.DMA((N_EXPERTS,)),
                pltpu.SemaphoreType.DMA((N_EXPERTS,)),
            ],
        ),
        out_shape=jax.ShapeDtypeStruct((rows, XS_W), BF16),
        compiler_params=_params("arbitrary"),
        name="moe_dispatch",
    )(base, n_flat, tail, hn, ct, st)

    ys = pl.pallas_call(
        _expert_kernel,
        grid_spec=pltpu.PrefetchScalarGridSpec(
            num_scalar_prefetch=2,
            grid=(nb,),
            in_specs=[
                pl.BlockSpec((MOE_TR, XS_W), lambda b, be, nu: (b, 0)),
                pl.BlockSpec((None, 1, d, ff), lambda b, be, nu: (moe_layer, be[b], 0, 0)),
                pl.BlockSpec((None, 1, d, ff), lambda b, be, nu: (moe_layer, be[b], 0, 1)),
                pl.BlockSpec((None, 1, ff, d), lambda b, be, nu: (moe_layer, be[b], 0, 0)),
            ],
            out_specs=pl.BlockSpec((MOE_TR, d), lambda b, be, nu: (b, 0)),
            scratch_shapes=[pltpu.VMEM((MOE_TR, d), F32)],
        ),
        out_shape=jax.ShapeDtypeStruct((rows, d), BF16),
        compiler_params=_params("arbitrary"),
        name="moe_experts",
    )(blk_expert, n_used, xs, w_gu, w_gu, w_down)

    out_norm = g_out is not None
    g_out = jnp.ones((d,), F32) if g_out is None else g_out
    return pl.pallas_call(
        functools.partial(_combine_kernel, out_norm=out_norm),
        grid_spec=pltpu.PrefetchScalarGridSpec(
            num_scalar_prefetch=2,
            grid=(nt,),
            in_specs=[
                tile_spec((MOE_TM, d), lambda i, b, c: (i, 0)),
                tile_spec((N_EXPERTS, MOE_TM), lambda i, b, c: (0, i)),
                pl.BlockSpec(memory_space=pl.ANY),
                tile_spec((1, d), lambda i, b, c: (0, 0)),
            ],
            out_specs=tile_spec((MOE_TM, d), lambda i, b, c: (i, 0)),
            scratch_shapes=[
                pltpu.VMEM((2, N_EXPERTS * MOE_HALF, d), BF16),
                pltpu.VMEM((N_EXPERTS, MOE_HALF, d), BF16),
                pltpu.SemaphoreType.DMA((2, N_EXPERTS)),
                pltpu.SemaphoreType.DMA((N_EXPERTS,)),
            ],
        ),
        out_shape=jax.ShapeDtypeStruct((m, d), F32),
        compiler_params=_params("arbitrary"),
        name="moe_combine",
    )(base, n_flat, x, st, ys, g_out.reshape(1, d).astype(F32))


def kernel(x, mem, norm_mix, norm_mem, norm_ffn, mem_norm, final_norm, s5_w_in, s5_lam_re, s5_lam_im, s5_log_dt, s5_b_re, s5_b_im, s5_c_re, s5_c_im, s5_d, s5_w_glu, att_w_qkv, att_w_o, att_rel_bias, lru_w_in, lru_conv_w, lru_conv_b, lru_w_a, lru_b_a, lru_w_x, lru_b_x, lru_lam, lru_w_out, mem_w_q, mem_w_kv, mem_w_o, ffn_w_gu, ffn_w_down, moe_w_router, moe_w_gu, moe_w_down):
    bsz, seq, d = x.shape
    x = x.reshape(bsz * seq, d).astype(F32)

    w_kv_all = jnp.concatenate([mem_w_kv[i] for i in range(DEPTH)], axis=1).astype(BF16)
    kv_all = _norm_linear(mem.reshape(bsz * N_MEM, d).astype(F32), mem_norm, w_kv_all, BF16,
                          tm=bsz * N_MEM, tn=2 * d)

    mem_wq, mem_wo = mem_w_q.astype(BF16), mem_w_o.astype(BF16)
    ffn_wgu, ffn_wd = ffn_w_gu.astype(BF16), ffn_w_down.astype(BF16)
    moe_wgu, moe_wd = moe_w_gu.astype(BF16), moe_w_down.astype(BF16)
    for i in range(DEPTH):
        kind, j = i % 3, i // 3
        if kind == 0:
            x = _s5_layer(x, norm_mix[i], s5_w_in[j], s5_lam_re[j], s5_lam_im[j], s5_log_dt[j],
                          s5_b_re[j], s5_b_im[j], s5_c_re[j], s5_c_im[j], s5_d[j], s5_w_glu[j],
                          bsz, seq)
        elif kind == 1:
            x = _attn_layer(x, norm_mix[i], att_w_qkv[j], att_w_o[j], att_rel_bias[j], bsz, seq)
        else:
            x = _lru_layer(x, norm_mix[i], lru_w_in[j], lru_conv_w[j], lru_conv_b[j], lru_w_a[j],
                           lru_b_a[j], lru_w_x[j], lru_b_x[j], lru_lam[j], lru_w_out[j], bsz, seq)
        if i % 2 == 0:
            x = _mem_ffn_layer(x, norm_mem[i], mem_wq, kv_all, i, mem_wo, norm_ffn[i], ffn_wgu, ffn_wd,
                               i // 2, bsz, seq)
        else:
            x = _moe_layer(x, norm_mem[i], mem_wq, kv_all, i, mem_wo, norm_ffn[i], moe_w_router[i // 2],
                           moe_wgu, moe_wd, i // 2, bsz, seq,
                           g_out=final_norm if i == DEPTH - 1 else None)
    return x.reshape(bsz, seq, d)
```

```python
import functools
import math

import jax
import jax.numpy as jnp
from jax import lax
from jax.experimental import pallas as pl
from jax.experimental.pallas import tpu as pltpu

F32 = jnp.float32
BF16 = jnp.bfloat16
HIGHEST = lax.Precision.HIGHEST

D_MODEL = 1024
DEPTH = 4
CHUNK = 64
N_MEM = 256
EPS = 1e-6

S5_GROUP = 16
S5_GROUPS = D_MODEL // S5_GROUP
S5_STATE = 64

ATT_HEADS = 16
ATT_HEAD_DIM = D_MODEL // ATT_HEADS
LEFT_CHUNKS = 8
MAX_REL = 128
ATT_TQ = 4 * CHUNK
ATT_LEFT = LEFT_CHUNKS * CHUNK
ATT_TK = ATT_LEFT + ATT_TQ
NEG_INF = -1e30
LOG2E = math.log2(math.e)

LRU_BLOCKS = 8
LRU_BLOCK_W = D_MODEL // LRU_BLOCKS
CONV_W = 4
LRU_C = 8.0
LRU_T = 512
SUBLANES = 8
LANES = 128

MEM_HEADS = 4
MEM_HEAD_DIM = D_MODEL // MEM_HEADS

N_EXPERTS = 8
FFN_TF = 256

VMEM_LIMIT = 56 * 1024 * 1024


def _params(*sem):
    return pltpu.CompilerParams(dimension_semantics=sem, vmem_limit_bytes=VMEM_LIMIT)


def _rms(x, g):
    ms = jnp.mean(x * x, axis=-1, keepdims=True)
    return x * lax.rsqrt(ms + EPS) * g


def _norm_linear_kernel(x_ref, g_ref, w_ref, o_ref, hn_ref):
    @pl.when(pl.program_id(1) == 0)
    def _():
        hn_ref[...] = _rms(x_ref[...], g_ref[...]).astype(BF16)

    o_ref[...] = jnp.dot(hn_ref[...], w_ref[...], preferred_element_type=F32).astype(o_ref.dtype)


def _norm_linear(x, g, w, out_dtype, tm=512, tn=None):
    m, d = x.shape
    n = w.shape[1]
    tn = n if tn is None else tn
    return pl.pallas_call(
        _norm_linear_kernel,
        grid=(m // tm, n // tn),
        in_specs=[
            pl.BlockSpec((tm, d), lambda i, j: (i, 0)),
            pl.BlockSpec((1, d), lambda i, j: (0, 0)),
            pl.BlockSpec((d, tn), lambda i, j: (0, j)),
        ],
        out_specs=pl.BlockSpec((tm, tn), lambda i, j: (i, j)),
        out_shape=jax.ShapeDtypeStruct((m, n), out_dtype),
        scratch_shapes=[pltpu.VMEM((tm, d), BF16)],
        compiler_params=_params("parallel", "arbitrary"),
        name="norm_linear",
    )(x, g.reshape(1, d), w)


def _s5_operators(lam_re, lam_im, log_dt, b_re, b_im, c_re, c_im, d_skip, n_chunks):
    g, n, p = S5_GROUPS, S5_STATE, S5_GROUP
    lr = lam_re.astype(F32)
    li = lam_im.astype(F32)
    dt = jnp.exp(log_dt.astype(F32))[:, None]
    mag = jnp.exp(lr * dt)
    ar = mag * jnp.cos(li * dt)
    ai = mag * jnp.sin(li * dt)
    den = lr * lr + li * li
    fr = ((ar - 1.0) * lr + ai * li) / den
    fi = (ai * lr - (ar - 1.0) * li) / den
    bbr = fr[..., None] * b_re - fi[..., None] * b_im
    bbi = fr[..., None] * b_im + fi[..., None] * b_re
    bbt = jnp.concatenate([bbr, bbi], axis=1).transpose(0, 2, 1)

    kk = jnp.arange(CHUNK + 1, dtype=F32)
    mag_k = jnp.exp((lr * dt)[..., None] * kk)
    pr = mag_k * jnp.cos((li * dt)[..., None] * kk)
    pi = mag_k * jnp.sin((li * dt)[..., None] * kk)
    pk = jnp.concatenate([pr[..., :CHUNK], pi[..., :CHUNK], pr[..., 1:], pi[..., 1:],
                          pr[..., CHUNK - 1::-1], pi[..., CHUNK - 1::-1]], axis=-1)
    cb = jnp.concatenate([c_re.transpose(0, 2, 1), c_im.transpose(0, 2, 1), bbr, bbi], axis=-1)
    dq = jnp.broadcast_to(d_skip.reshape(g, p, 1), (g, p, LANES))
    qr, qi = [pr[..., CHUNK]], [pi[..., CHUNK]]
    for _ in range(int(math.log2(n_chunks)) - 1):
        qr, qi = qr + [qr[-1] * qr[-1] - qi[-1] * qi[-1]], qi + [2.0 * qr[-1] * qi[-1]]
    levels = len(qr)
    qr, qi = jnp.stack(qr), jnp.stack(qi)
    m1 = jnp.concatenate([qr, qr], axis=-1)
    m2 = jnp.concatenate([-qi, qi], axis=-1)
    ap = jnp.stack([m1, m2], axis=-1).transpose(1, 2, 0, 3).reshape(g, 2 * n, 2 * levels)
    return bbt, pk, cb, dq, ap


def _s5_in_kernel(x3_ref, g_ref, wt_ref, u3_ref, xbuf, ubuf, sem_in, sem_out):
    s = pl.program_id(0)
    last = pl.num_programs(0) - 1
    slot = s % 2

    def load(step):
        return pltpu.make_async_copy(x3_ref.at[pl.ds(0, x3_ref.shape[0]), step], xbuf.at[step % 2],
                                     sem_in.at[step % 2])

    def store(step):
        return pltpu.make_async_copy(ubuf.at[step % 2], u3_ref.at[pl.ds(0, u3_ref.shape[0]), step],
                                     sem_out.at[step % 2])

    pl.when(s == 0)(lambda: load(s).start())
    pl.when(s < last)(lambda: load(s + 1).start())
    load(s).wait()
    pl.when(s >= 2)(lambda: store(s - 2).wait())
    hn = _rms(xbuf[slot], g_ref[...]).astype(BF16)
    ubuf[slot] = lax.dot_general(wt_ref[...], hn, (((1,), (1,)), ((), ())), preferred_element_type=F32)
    store(s).start()

    @pl.when(s == last)
    def _():
        store(s - 1).wait()
        store(s).wait()


def _tile_lanes(a, times):
    a2 = jnp.concatenate([a, a], axis=1)
    return jnp.concatenate([a2] * (times // 2), axis=1)


def _spread_lanes(a):
    r, c = a.shape
    low = lax.broadcasted_iota(jnp.int32, (r, LANES), 1) < CHUNK
    pairs = [jnp.where(low, jnp.broadcast_to(a[:, 2 * j:2 * j + 1], (r, LANES)),
                       jnp.broadcast_to(a[:, 2 * j + 1:2 * j + 2], (r, LANES))) for j in range(c // 2)]
    return jnp.concatenate(pairs, axis=1)


def _ssm_kernel(u_ref, bbt_ref, pk_ref, cb_ref, dq_ref, ap_ref, o_ref, toet_ref, *, n_chunks):
    n = S5_STATE
    rows = S5_GROUP * CHUNK
    ncol = u_ref.shape[-1]
    u = u_ref[0].reshape(rows, ncol).astype(BF16)

    pk = pk_ref[0]
    cb = cb_ref[0]
    cr, ci, br, bi = [_spread_lanes(cb[:, S5_GROUP * k:S5_GROUP * (k + 1)]) for k in range(4)]
    p0r, p0i, p1r, p1i, rr, ri = [_tile_lanes(pk[:, CHUNK * k:CHUNK * (k + 1)], S5_GROUP) for k in range(6)]
    lk = jnp.concatenate([cr * p0r - ci * p0i, -(cr * p0i + ci * p0r)], axis=0)
    cinjt = jnp.concatenate([cr * p1r - ci * p1i, -(cr * p1i + ci * p1r)], axis=0).astype(BF16)
    bm = jnp.concatenate([rr * br - ri * bi, rr * bi + ri * br], axis=0).astype(BF16)

    kern = jnp.dot(bbt_ref[0], lk, preferred_element_type=F32, precision=HIGHEST)
    lane = lax.broadcasted_iota(jnp.int32, (S5_GROUP, rows), 1)
    own_lag0 = lane == lax.broadcasted_iota(jnp.int32, (S5_GROUP, rows), 0) * CHUNK
    kern = kern + jnp.where(own_lag0, jnp.concatenate([dq_ref[0]] * (rows // LANES), axis=1), 0.0)
    causal = (lax.broadcasted_iota(jnp.int32, (CHUNK, rows), 1) % CHUNK
              >= lax.broadcasted_iota(jnp.int32, (CHUNK, rows), 0))
    for p in range(S5_GROUP):
        base = jnp.broadcast_to(kern[p:p + 1, :], (CHUNK, rows))
        rolled = pltpu.roll(base, 0, 1, stride=1, stride_axis=0)
        toet_ref[p * CHUNK:(p + 1) * CHUNK, :] = jnp.where(causal, rolled, 0.0).astype(BF16)
    tn_dims = (((0,), (0,)), ((), ()))
    y = lax.dot_general(toet_ref[...], u, tn_dims, preferred_element_type=F32)
    x = jnp.dot(bm, u, preferred_element_type=F32)
    ap = ap_ref[0]
    col = lax.broadcasted_iota(jnp.int32, x.shape, 1) % n_chunks
    for k in range(int(math.log2(n_chunks))):
        sh = 1 << k
        xs = jnp.where(col >= sh, pltpu.roll(x, sh, 1), 0.0)
        xsw = jnp.concatenate([xs[n:], xs[:n]], axis=0)
        x = x + ap[:, 2 * k:2 * k + 1] * xs + ap[:, 2 * k + 1:2 * k + 2] * xsw
    h0 = jnp.where(col >= 1, pltpu.roll(x, 1, 1), 0.0)
    h0_hi = h0.astype(BF16)
    h0_lo = (h0 - h0_hi.astype(F32)).astype(BF16)
    y = y + lax.dot_general(cinjt, h0_hi, tn_dims, preferred_element_type=F32)
    y = y + lax.dot_general(cinjt, h0_lo, tn_dims, preferred_element_type=F32)
    o_ref[0] = jax.nn.gelu(y).reshape(S5_GROUP, CHUNK, ncol)


def _s5_out_kernel(v3_ref, x3_ref, w_ref, o3_ref, vbuf, xbuf, obuf, sem_v, sem_x, sem_o):
    s = pl.program_id(0)
    last = pl.num_programs(0) - 1
    slot = s % 2

    def load_v(step):
        return pltpu.make_async_copy(v3_ref.at[pl.ds(0, v3_ref.shape[0]), step], vbuf.at[step % 2],
                                     sem_v.at[step % 2])

    def load_x(step):
        return pltpu.make_async_copy(x3_ref.at[pl.ds(0, x3_ref.shape[0]), step], xbuf.at[step % 2],
                                     sem_x.at[step % 2])

    def store(step):
        return pltpu.make_async_copy(obuf.at[step % 2], o3_ref.at[pl.ds(0, o3_ref.shape[0]), step],
                                     sem_o.at[step % 2])

    def load(step):
        load_v(step).start()
        load_x(step).start()

    pl.when(s == 0)(lambda: load(s))
    pl.when(s < last)(lambda: load(s + 1))
    load_v(s).wait()
    load_x(s).wait()
    pl.when(s >= 2)(lambda: store(s - 2).wait())
    z = lax.dot_general(vbuf[slot].astype(BF16), w_ref[...], (((0,), (0,)), ((), ())),
                        preferred_element_type=F32)
    d = xbuf.shape[-1]
    obuf[slot] = xbuf[slot] + z[:, :d] * jax.nn.sigmoid(z[:, d:])
    store(s).start()

    @pl.when(s == last)
    def _():
        store(s - 1).wait()
        store(s).wait()


def _s5_layer(x, g_norm, w_in, lam_re, lam_im, log_dt, b_re, b_im, c_re, c_im, d_skip, w_glu,
              bsz, seq):
    m, d = x.shape
    n_chunks = seq // CHUNK
    ncol = bsz * n_chunks
    rows = S5_GROUP * CHUNK
    bbt, pk, cb, dq, ap = _s5_operators(lam_re, lam_im, log_dt, b_re, b_im, c_re, c_im, d_skip,
                                        n_chunks)
    x3 = x.reshape(ncol, CHUNK, d)
    hbm = pl.BlockSpec(memory_space=pl.ANY)
    u3 = pl.pallas_call(
        _s5_in_kernel,
        grid=(CHUNK,),
        in_specs=[hbm, pl.BlockSpec((1, d), lambda s: (0, 0)), pl.BlockSpec((d, d), lambda s: (0, 0))],
        out_specs=hbm,
        out_shape=jax.ShapeDtypeStruct((d, CHUNK, ncol), F32),
        scratch_shapes=[
            pltpu.VMEM((2, ncol, d), F32),
            pltpu.VMEM((2, d, ncol), F32),
            pltpu.SemaphoreType.DMA((2,)),
            pltpu.SemaphoreType.DMA((2,)),
        ],
        compiler_params=_params("arbitrary"),
        name="s5_in",
    )(x3, g_norm.reshape(1, d), w_in.T.astype(BF16))
    blk4 = (1, S5_GROUP, CHUNK, ncol)
    per_group = lambda a: pl.BlockSpec((1,) + a.shape[1:], lambda g: (g, 0, 0))
    v4 = pl.pallas_call(
        functools.partial(_ssm_kernel, n_chunks=n_chunks),
        grid=(S5_GROUPS,),
        in_specs=[pl.BlockSpec(blk4, lambda g: (g, 0, 0, 0))]
        + [per_group(a) for a in (bbt, pk, cb, dq, ap)],
        out_specs=pl.BlockSpec(blk4, lambda g: (g, 0, 0, 0)),
        out_shape=jax.ShapeDtypeStruct((S5_GROUPS, S5_GROUP, CHUNK, ncol), F32),
        scratch_shapes=[pltpu.VMEM((rows, rows), BF16)],
        compiler_params=_params("parallel"),
        name="s5_ssm",
    )(u3.reshape(S5_GROUPS, S5_GROUP, CHUNK, ncol), bbt, pk, cb, dq, ap)
    o3 = pl.pallas_call(
        _s5_out_kernel,
        grid=(CHUNK,),
        in_specs=[hbm, hbm, pl.BlockSpec((d, 2 * d), lambda s: (0, 0))],
        out_specs=hbm,
        out_shape=jax.ShapeDtypeStruct((ncol, CHUNK, d), F32),
        scratch_shapes=[
            pltpu.VMEM((2, d, ncol), F32),
            pltpu.VMEM((2, ncol, d), F32),
            pltpu.VMEM((2, ncol, d), F32),
            pltpu.SemaphoreType.DMA((2,)),
            pltpu.SemaphoreType.DMA((2,)),
            pltpu.SemaphoreType.DMA((2,)),
        ],
        compiler_params=_params("arbitrary"),
        name="s5_out",
    )(v4.reshape(d, CHUNK, ncol), x3, w_glu.astype(BF16))
    return o3.reshape(m, d)


ATT_PERIOD = ATT_TQ + ATT_TK


def _attn_bias_seed(rel_bias):
    k = jnp.arange(ATT_PERIOD)
    rel = jnp.where(k < ATT_TK, ATT_LEFT - k, ATT_LEFT + ATT_PERIOD - k)
    return jnp.take(rel_bias.astype(F32), jnp.clip(rel, -MAX_REL, MAX_REL) + MAX_REL, axis=1)


def _attn_kernel(x_ref, q_ref, k0_ref, k1_ref, k2_ref, v0_ref, v1_ref, v2_ref, seed_ref, wo_ref,
                 o_ref, bias_ref, ocat_ref):
    i = pl.program_id(1)

    @pl.when((pl.program_id(0) == 0) & (i == 0))
    def _():
        r = lax.broadcasted_iota(jnp.int32, (ATT_TQ, ATT_TK), 0)
        j = lax.broadcasted_iota(jnp.int32, (ATT_TQ, ATT_TK), 1)
        dchunk = (r + ATT_LEFT) // CHUNK - j // CHUNK
        valid = (dchunk >= 0) & (dchunk <= LEFT_CHUNKS)
        for h in range(ATT_HEADS):
            seed = jnp.broadcast_to(seed_ref[h:h + 1, :], (ATT_TQ, ATT_PERIOD))
            rolled = pltpu.roll(seed, 0, 1, stride=1, stride_axis=0)
            bias_ref[h] = jnp.where(valid, rolled[:, :ATT_TK] * LOG2E, NEG_INF)

    lane = lax.broadcasted_iota(jnp.int32, (1, LANES), 1)
    col = lax.broadcasted_iota(jnp.int32, (ATT_TQ, ATT_TK), 1)
    pad = col < ATT_LEFT - i * ATT_TQ
    scale = ATT_HEAD_DIM ** -0.5
    heads_per = LANES // ATT_HEAD_DIM
    for hp in range(D_MODEL // LANES):
        sl = slice(hp * LANES, (hp + 1) * LANES)
        q = q_ref[:, sl] * scale
        k = jnp.concatenate([k0_ref[:, sl], k1_ref[:, sl], k2_ref[:, sl]], axis=0)
        v = jnp.concatenate([v0_ref[:, sl], v1_ref[:, sl], v2_ref[:, sl]], axis=0)
        acc = jnp.zeros((ATT_TQ, LANES), F32)
        for h in range(heads_per):
            in_head = (lane >= h * ATT_HEAD_DIM) & (lane < (h + 1) * ATT_HEAD_DIM)
            qh = jnp.where(in_head, q, jnp.zeros_like(q))
            s = lax.dot_general(qh, k, (((1,), (1,)), ((), ())), preferred_element_type=F32)
            s = jnp.where(pad, NEG_INF, s + bias_ref[hp * heads_per + h])
            p = jnp.exp2(s - jnp.max(s, axis=-1, keepdims=True))
            vh = jnp.where(in_head, v, jnp.ones_like(v))
            pv = jnp.dot(p.astype(BF16), vh, preferred_element_type=F32)
            acc = jnp.where(in_head, pv / pltpu.roll(pv, ATT_HEAD_DIM, 1), acc)
        ocat_ref[:, sl] = acc.astype(BF16)
    o_ref[...] = x_ref[...] + jnp.dot(ocat_ref[...], wo_ref[...], preferred_element_type=F32)


def _attn_layer(x, g_norm, w_qkv, w_o, rel_bias, bsz, seq):
    m, d = x.shape
    w_qkv = jnp.concatenate([w_qkv[:, :d] * LOG2E, w_qkv[:, d:]], axis=1)
    qkv = _norm_linear(x, g_norm, w_qkv.astype(BF16), BF16, tm=1024)
    nq = seq // ATT_TQ

    def rows(col, back):
        return lambda b, i: (b * nq + jnp.maximum(i - back, 0), col)

    blk = (ATT_TQ, d)
    nkb = ATT_TK // ATT_TQ
    return pl.pallas_call(
        _attn_kernel,
        grid=(bsz, nq),
        in_specs=(
            [pl.BlockSpec(blk, rows(0, 0)), pl.BlockSpec(blk, rows(0, 0))]
            + [pl.BlockSpec(blk, rows(1, nkb - 1 - kb)) for kb in range(nkb)]
            + [pl.BlockSpec(blk, rows(2, nkb - 1 - kb)) for kb in range(nkb)]
            + [pl.BlockSpec((ATT_HEADS, ATT_PERIOD), lambda b, i: (0, 0)),
               pl.BlockSpec((d, d), lambda b, i: (0, 0))]
        ),
        out_specs=pl.BlockSpec(blk, rows(0, 0)),
        out_shape=jax.ShapeDtypeStruct((m, d), F32),
        scratch_shapes=[pltpu.VMEM((ATT_HEADS, ATT_TQ, ATT_TK), F32), pltpu.VMEM((ATT_TQ, d), BF16)],
        compiler_params=_params("arbitrary", "arbitrary"),
        name="chunk_attn",
    )(x, *([qkv] * (1 + 2 * nkb)), _attn_bias_seed(rel_bias), w_o.astype(BF16))


def _lru_kernel(x_ref, g_ref, win_ref, cw_ref, cb_ref, wax_ref, ba_ref, bx_ref, lam_ref, wout_ref,
                o_ref, prev_ref, carry_ref, a_s, b_s):
    t, w = x_ref.shape

    @pl.when(pl.program_id(1) == 0)
    def _():
        prev_ref[...] = jnp.zeros_like(prev_ref)
        carry_ref[...] = jnp.zeros_like(carry_ref)

    x = x_ref[...]
    z = jnp.dot(_rms(x, g_ref[...]).astype(BF16), win_ref[...], preferred_element_type=F32)
    gate = z[:, :w]
    xr = z[:, w:]
    xext = jnp.concatenate([prev_ref[...], xr], axis=0)
    cw = cw_ref[...]
    xc = cb_ref[...] + cw[0:1] * xext[SUBLANES - 3:SUBLANES - 3 + t]
    for k in range(1, CONV_W):
        xc = xc + cw[k:k + 1] * xext[SUBLANES - 3 + k:SUBLANES - 3 + k + t]
    prev_ref[...] = xr[t - SUBLANES:]

    xcb = xc.astype(BF16)
    pre_a, pre_x = [], []
    for blk in range(LRU_BLOCKS):
        pre = jnp.dot(xcb[:, blk * LRU_BLOCK_W:(blk + 1) * LRU_BLOCK_W], wax_ref[blk],
                      preferred_element_type=F32)
        pre_a.append(pre[:, :LRU_BLOCK_W])
        pre_x.append(pre[:, LRU_BLOCK_W:])
    r = jax.nn.sigmoid(jnp.concatenate(pre_a, axis=1) + ba_ref[...])
    ig = jax.nn.sigmoid(jnp.concatenate(pre_x, axis=1) + bx_ref[...])
    z = -lam_ref[...]
    softplus = jnp.maximum(z, 0.0) + jnp.log1p(jnp.exp(-jnp.abs(z)))
    log_a = -LRU_C * r * softplus
    a = jnp.exp(log_a)
    b = jnp.sqrt(1.0 - a * a) * (ig * xc)

    a3 = a.reshape(t // SUBLANES, SUBLANES, w)
    b3 = b.reshape(t // SUBLANES, SUBLANES, w)
    sub = lax.broadcasted_iota(jnp.int32, a3.shape, 1)
    for sh in (1, 2, 4):
        a_sh = jnp.where(sub >= sh, pltpu.roll(a3, sh, 1), 1.0)
        b_sh = jnp.where(sub >= sh, pltpu.roll(b3, sh, 1), 0.0)
        b3 = a3 * b_sh + b3
        a3 = a3 * a_sh
    a_s[...] = a3
    b_s[...] = b3

    def body(j, carry):
        hb = b_s[j] + a_s[j] * carry
        b_s[j] = hb
        return hb[SUBLANES - 1:SUBLANES, :]

    carry = lax.fori_loop(0, t // SUBLANES, body, carry_ref[0:1, :])
    carry_ref[0:1, :] = carry
    h = b_s[...].reshape(t, w)
    y = (jax.nn.gelu(gate) * h).astype(BF16)
    o_ref[...] = x + jnp.dot(y, wout_ref[...], preferred_element_type=F32)


def _lru_layer(x, g_norm, w_in, conv_w, conv_b, w_a, b_a, w_x, b_x, lam, w_out, bsz, seq):
    m, d = x.shape
    wax = jnp.concatenate([w_a, w_x], axis=-1).astype(BF16)
    nt = seq // LRU_T
    row = lambda v: v.reshape(1, d).astype(F32)
    vec = pl.BlockSpec((1, d), lambda b, i: (0, 0))
    return pl.pallas_call(
        _lru_kernel,
        grid=(bsz, nt),
        in_specs=[
            pl.BlockSpec((LRU_T, d), lambda b, i: (b * nt + i, 0)),
            vec,
            pl.BlockSpec((d, 2 * d), lambda b, i: (0, 0)),
            pl.BlockSpec((CONV_W, d), lambda b, i: (0, 0)),
            vec,
            pl.BlockSpec(wax.shape, lambda b, i: (0, 0, 0)),
            vec, vec, vec,
            pl.BlockSpec((d, d), lambda b, i: (0, 0)),
        ],
        out_specs=pl.BlockSpec((LRU_T, d), lambda b, i: (b * nt + i, 0)),
        out_shape=jax.ShapeDtypeStruct((m, d), F32),
        scratch_shapes=[
            pltpu.VMEM((SUBLANES, d), F32),
            pltpu.VMEM((SUBLANES, d), F32),
            pltpu.VMEM((LRU_T // SUBLANES, SUBLANES, d), F32),
            pltpu.VMEM((LRU_T // SUBLANES, SUBLANES, d), F32),
        ],
        compiler_params=_params("parallel", "arbitrary"),
        name="rglru",
    )(x, row(g_norm), w_in.astype(BF16), conv_w.astype(F32), row(conv_b), wax, row(b_a), row(b_x),
      row(lam), w_out.astype(BF16))


def _memattn(x, g_ref, wq_ref, k_ref, v_ref, wo_ref, ocat_ref):
    hn = _rms(x, g_ref[...]).astype(BF16)
    q = jnp.dot(hn, wq_ref[...], preferred_element_type=F32) * (MEM_HEAD_DIM ** -0.5)
    q = q.astype(BF16)
    for h in range(MEM_HEADS):
        sl = slice(h * MEM_HEAD_DIM, (h + 1) * MEM_HEAD_DIM)
        s = lax.dot_general(q[:, sl], k_ref[:, sl], (((1,), (1,)), ((), ())),
                            preferred_element_type=F32)
        mx = jnp.max(s, axis=-1, keepdims=True)
        p = jnp.exp(s - mx)
        l = jnp.sum(p, axis=-1, keepdims=True)
        o = jnp.dot(p.astype(BF16), v_ref[:, sl], preferred_element_type=F32) / l
        ocat_ref[:, sl] = o.astype(BF16)
    return x + jnp.dot(ocat_ref[...], wo_ref[...], preferred_element_type=F32)


def _mem_ffn_kernel(x_ref, g_ref, wq_ref, k_ref, v_ref, wo_ref, g2_ref, wgu_ref, wd_ref, o_ref, ocat_ref):
    ff = wd_ref.shape[0]
    x1 = _memattn(x_ref[...], g_ref, wq_ref, k_ref, v_ref, wo_ref, ocat_ref)
    hn = _rms(x1, g2_ref[...]).astype(BF16)
    o_ref[...] = x1
    for f in range(ff // FFN_TF):
        lo = f * FFN_TF
        gate = jnp.dot(hn, wgu_ref[:, lo:lo + FFN_TF], preferred_element_type=F32)
        up = jnp.dot(hn, wgu_ref[:, ff + lo:ff + lo + FFN_TF], preferred_element_type=F32)
        act = gate * jax.nn.sigmoid(gate) * up
        o_ref[...] += jnp.dot(act.astype(BF16), wd_ref[lo:lo + FFN_TF, :], preferred_element_type=F32)


def _mem_route_kernel(x_ref, g_ref, wq_ref, k_ref, v_ref, wo_ref, g2_ref, wrt_ref,
                      o_ref, hn_ref, ct_ref, st_ref, cnt_ref, ocat_ref):
    x1 = _memattn(x_ref[...], g_ref, wq_ref, k_ref, v_ref, wo_ref, ocat_ref)
    o_ref[...] = x1
    hn = _rms(x1, g2_ref[...])
    hn_ref[...] = hn.astype(BF16)
    lg = lax.dot_general(wrt_ref[...], hn, (((1,), (1,)), ((), ())),
                         preferred_element_type=F32, precision=HIGHEST)
    idx = lax.broadcasted_iota(jnp.int32, lg.shape, 0)
    m1 = jnp.max(lg, axis=0, keepdims=True)
    i1 = jnp.min(jnp.where(lg == m1, idx, N_EXPERTS), axis=0, keepdims=True)
    oh1 = idx == i1
    lg2 = jnp.where(oh1, -jnp.inf, lg)
    m2 = jnp.max(lg2, axis=0, keepdims=True)
    i2 = jnp.min(jnp.where(lg2 == m2, idx, N_EXPERTS), axis=0, keepdims=True)
    oh2 = idx == i2
    e2 = jnp.exp(m2 - m1)
    g1 = 1.0 / (1.0 + e2)
    g2 = e2 / (1.0 + e2)
    ct_ref[...] = jnp.where(oh1, g1, 0.0) + jnp.where(oh2, g2, 0.0)
    sel = (oh1 | oh2).astype(F32)
    st_ref[...] = sel
    cnt_ref[0] = jnp.broadcast_to(jnp.sum(sel, axis=1, keepdims=True), (N_EXPERTS, LANES))


def _mem_specs(d, nt, tm, layer):
    wspec = pl.BlockSpec((None, d, d), lambda b, i: (layer, 0, 0))
    return [
        pl.BlockSpec((tm, d), lambda b, i: (b * nt + i, 0)),
        pl.BlockSpec((1, d), lambda b, i: (0, 0)),
        wspec,
        pl.BlockSpec((N_MEM, d), lambda b, i: (b, 2 * layer)),
        pl.BlockSpec((N_MEM, d), lambda b, i: (b, 2 * layer + 1)),
        wspec,
    ]


def _mem_ffn_layer(x, g_mem, w_q, kv_all, layer, w_o, g_ffn, w_gu, w_down, ffn_layer, bsz, seq, tm=1024):
    m, d = x.shape
    nt = seq // tm
    ff = w_down.shape[1]
    resident = lambda shape: pl.BlockSpec((None,) + shape, lambda b, i: (ffn_layer, 0, 0),
                                          pipeline_mode=pl.Buffered(1))
    return pl.pallas_call(
        _mem_ffn_kernel,
        grid=(bsz, nt),
        in_specs=_mem_specs(d, nt, tm, layer) + [
            pl.BlockSpec((1, d), lambda b, i: (0, 0)),
            resident((d, 2 * ff)),
            resident((ff, d)),
        ],
        out_specs=pl.BlockSpec((tm, d), lambda b, i: (b * nt + i, 0)),
        out_shape=jax.ShapeDtypeStruct((m, d), F32),
        scratch_shapes=[pltpu.VMEM((tm, d), BF16)],
        compiler_params=_params("parallel", "parallel"),
        name="mem_ffn",
    )(x, g_mem.reshape(1, d), w_q, kv_all, kv_all, w_o, g_ffn.reshape(1, d), w_gu, w_down)


MOE_TM = 512
MOE_TR = 512
MOE_HALF = MOE_TM // 2
MOE_ALIGN = 16
MOE_TF = 1024
XS_W = D_MODEL + LANES
MOE_ZR = MOE_TR + MOE_HALF


def _tile_rank(st):
    tm = st.shape[1]
    before = (lax.broadcasted_iota(jnp.int32, (tm, tm), 0)
              < lax.broadcasted_iota(jnp.int32, (tm, tm), 1))
    return jnp.dot(st.astype(BF16), before.astype(BF16), preferred_element_type=F32)


def _one_hot_rows(rank, st, e, half):
    tm = rank.shape[1]
    rows = lax.broadcasted_iota(jnp.int32, (MOE_HALF, tm), 0).astype(F32) + float(half * MOE_HALF)
    hit = (rank[e:e + 1] == rows) & (st[e:e + 1] > 0.0)
    return jnp.where(hit, 1.0, 0.0).astype(BF16)


def _one_hot_all(rank, st):
    return jnp.concatenate([_one_hot_rows(rank, st, e, 0) for e in range(N_EXPERTS)], axis=0)


def _dispatch_kernel(base_ref, n_ref, tail_ref, hn_ref, ct_ref, st_ref, xs_ref, stage, stage2, zbuf,
                     sem, sem2, zsem):
    i = pl.program_id(0)
    last = pl.num_programs(0) - 1
    slot = i % 2

    @pl.when(i == 0)
    def _():
        zbuf[...] = jnp.zeros_like(zbuf)

        def zero(row, nrows):
            cp = pltpu.make_async_copy(zbuf.at[pl.ds(0, nrows)], xs_ref.at[pl.ds(row, nrows)], zsem)
            cp.start()
            cp.wait()

        for e in range(N_EXPERTS):
            zero(pl.multiple_of(tail_ref[e], MOE_ALIGN), MOE_ZR)

        def body(b, carry):
            zero(pl.multiple_of(b * MOE_TR, MOE_TR), MOE_TR)
            return carry

        lax.fori_loop(tail_ref[N_EXPERTS], xs_ref.shape[0] // MOE_TR, body, 0)

    hn = hn_ref[...]
    ct = ct_ref[...]
    st = st_ref[...]
    rank = _tile_rank(st)
    g_hi = ct.astype(BF16)
    g_lo = (ct - g_hi.astype(F32)).astype(BF16)
    gm = jnp.concatenate(
        [g_hi, g_lo, jnp.zeros((LANES - 2 * N_EXPERTS, ct.shape[1]), BF16)], axis=0)
    nt_dims = (((1,), (1,)), ((), ()))

    def copy(step, e):
        row = pl.multiple_of(base_ref[step * N_EXPERTS + e], MOE_ALIGN)
        return pltpu.make_async_copy(stage.at[step % 2, pl.ds(e * MOE_HALF, MOE_HALF)],
                                     xs_ref.at[pl.ds(row, MOE_HALF)], sem.at[step % 2, e])

    def copy2(e):
        row = pl.multiple_of(base_ref[i * N_EXPERTS + e] + MOE_HALF, MOE_ALIGN)
        return pltpu.make_async_copy(stage2.at[e], xs_ref.at[pl.ds(row, MOE_HALF)], sem2.at[e])

    def drain(step):
        for e in range(N_EXPERTS):
            copy(step, e).wait()

    p = _one_hot_all(rank, st)
    stage[slot, :, :D_MODEL] = jnp.dot(p, hn, preferred_element_type=F32).astype(BF16)
    stage[slot, :, D_MODEL:] = lax.dot_general(p, gm, nt_dims, preferred_element_type=F32).astype(BF16)
    pl.when(i >= 1)(lambda: drain(i - 1))
    for e in range(N_EXPERTS):
        copy(i, e).start()

    def fill2(e):
        p2 = _one_hot_rows(rank, st, e, 1)
        stage2[e, :, :D_MODEL] = jnp.dot(p2, hn, preferred_element_type=F32).astype(BF16)
        stage2[e, :, D_MODEL:] = lax.dot_general(p2, gm, nt_dims, preferred_element_type=F32).astype(BF16)
        copy2(e).start()

    for e in range(N_EXPERTS):
        pl.when(n_ref[i * N_EXPERTS + e] > MOE_HALF)(functools.partial(fill2, e))
    for e in range(N_EXPERTS):
        pl.when(n_ref[i * N_EXPERTS + e] > MOE_HALF)(lambda e=e: copy2(e).wait())

    pl.when(i == last)(lambda: drain(i))


def _expert_kernel(be_ref, nu_ref, x_ref, wg_ref, wu_ref, wd_ref, o_ref, acc_ref):
    b = pl.program_id(0)

    @pl.when(b < nu_ref[0])
    def _():
        e = be_ref[b]
        xe = x_ref[...]
        h = xe[:, :D_MODEL]
        ext = xe[:, D_MODEL:].astype(F32)
        lane = lax.broadcasted_iota(jnp.int32, ext.shape, 1)
        g = jnp.sum(jnp.where((lane == e) | (lane == e + N_EXPERTS), ext, 0.0), axis=1, keepdims=True)
        ff = wd_ref.shape[1]
        for c in range(ff // MOE_TF):
            lo = c * MOE_TF
            gate = jnp.dot(h, wg_ref[0, :, lo:lo + MOE_TF], preferred_element_type=F32)
            up = jnp.dot(h, wu_ref[0, :, lo:lo + MOE_TF], preferred_element_type=F32)
            act = (gate * jax.nn.sigmoid(gate) * up * g).astype(BF16)
            part = jnp.dot(act, wd_ref[0, lo:lo + MOE_TF, :], preferred_element_type=F32)
            if c == 0:
                acc_ref[...] = part
            else:
                acc_ref[...] += part
        o_ref[...] = acc_ref[...].astype(o_ref.dtype)

    @pl.when(b >= nu_ref[0])
    def _():
        o_ref[...] = jnp.zeros_like(o_ref)


def _combine_kernel(base_ref, n_ref, x_ref, st_ref, ys_ref, gout_ref, o_ref, ybuf, ybuf2, sem, sem2,
                    *, out_norm):
    i = pl.program_id(0)
    slot = i % 2

    def copy(step, e):
        row = pl.multiple_of(base_ref[step * N_EXPERTS + e], MOE_ALIGN)
        return pltpu.make_async_copy(ys_ref.at[pl.ds(row, MOE_HALF)],
                                     ybuf.at[step % 2, pl.ds(e * MOE_HALF, MOE_HALF)],
                                     sem.at[step % 2, e])

    def copy2(e):
        row = pl.multiple_of(base_ref[i * N_EXPERTS + e] + MOE_HALF, MOE_ALIGN)
        return pltpu.make_async_copy(ys_ref.at[pl.ds(row, MOE_HALF)], ybuf2.at[e], sem2.at[e])

    def fetch(step):
        for e in range(N_EXPERTS):
            copy(step, e).start()

    pl.when(i == 0)(lambda: fetch(i))
    pl.when(i + 1 < pl.num_programs(0))(lambda: fetch(i + 1))
    for e in range(N_EXPERTS):
        pl.when(n_ref[i * N_EXPERTS + e] > MOE_HALF)(lambda e=e: copy2(e).start())

    st = st_ref[...]
    rank = _tile_rank(st)
    p = _one_hot_all(rank, st)
    for e in range(N_EXPERTS):
        copy(i, e).wait()
    tn_dims = (((0,), (0,)), ((), ()))
    o_ref[...] = x_ref[...] + lax.dot_general(p, ybuf[slot], tn_dims, preferred_element_type=F32)

    def gather2(e):
        copy2(e).wait()
        p2 = _one_hot_rows(rank, st, e, 1)
        o_ref[...] += lax.dot_general(p2, ybuf2[e], tn_dims, preferred_element_type=F32)

    for e in range(N_EXPERTS):
        pl.when(n_ref[i * N_EXPERTS + e] > MOE_HALF)(functools.partial(gather2, e))
    if out_norm:
        o_ref[...] = _rms(o_ref[...], gout_ref[...])


def _moe_layer(x, g_mem, w_q, kv_all, layer, w_o, g_ffn, w_router, w_gu, w_down, moe_layer, bsz, seq,
               g_out=None):
    m, d = x.shape
    ff = w_down.shape[2]
    nt = m // MOE_TM
    ntb = seq // MOE_TM
    i32 = jnp.int32
    x, hn, ct, st, cnt = pl.pallas_call(
        _mem_route_kernel,
        grid=(bsz, ntb),
        in_specs=_mem_specs(d, ntb, MOE_TM, layer) + [
            pl.BlockSpec((1, d), lambda b, i: (0, 0)),
            pl.BlockSpec((N_EXPERTS, d), lambda b, i: (0, 0)),
        ],
        out_specs=[
            pl.BlockSpec((MOE_TM, d), lambda b, i: (b * ntb + i, 0)),
            pl.BlockSpec((MOE_TM, d), lambda b, i: (b * ntb + i, 0)),
            pl.BlockSpec((N_EXPERTS, MOE_TM), lambda b, i: (0, b * ntb + i)),
            pl.BlockSpec((N_EXPERTS, MOE_TM), lambda b, i: (0, b * ntb + i)),
            pl.BlockSpec((1, N_EXPERTS, LANES), lambda b, i: (b * ntb + i, 0, 0)),
        ],
        out_shape=[
            jax.ShapeDtypeStruct((m, d), F32),
            jax.ShapeDtypeStruct((m, d), BF16),
            jax.ShapeDtypeStruct((N_EXPERTS, m), F32),
            jax.ShapeDtypeStruct((N_EXPERTS, m), F32),
            jax.ShapeDtypeStruct((nt, N_EXPERTS, LANES), F32),
        ],
        scratch_shapes=[pltpu.VMEM((MOE_TM, d), BF16)],
        compiler_params=_params("parallel", "parallel"),
        name="mem_route",
    )(x, g_mem.reshape(1, d), w_q, kv_all, kv_all, w_o, g_ffn.reshape(1, d), w_router.T.astype(F32))

    n = cnt[:, :, 0].astype(i32)
    chunk = (n + MOE_ALIGN - 1) // MOE_ALIGN * MOE_ALIGN
    seg = (jnp.sum(chunk, axis=0) + MOE_HALF + MOE_TR - 1) // MOE_TR * MOE_TR
    seg_off = jnp.cumsum(seg) - seg
    base = (seg_off[None, :] + jnp.cumsum(chunk, axis=0) - chunk).reshape(-1).astype(i32)
    n_flat = n.reshape(-1)
    bound = 2 * m + nt * N_EXPERTS * (MOE_ALIGN - 1) + N_EXPERTS * (MOE_HALF + MOE_TR - 1)
    nb = -(-bound // MOE_TR)
    blk_end = jnp.cumsum(seg // MOE_TR)
    blk_expert = jnp.minimum(jnp.sum(jnp.arange(nb, dtype=i32)[:, None] >= blk_end[None, :], axis=1),
                             N_EXPERTS - 1).astype(i32)
    n_used = blk_end[-1:].astype(i32)
    rows = nb * MOE_TR
    tail = jnp.concatenate([jnp.maximum(seg_off + seg - MOE_ZR, 0), n_used]).astype(i32)

    tile_spec = lambda shape, imap: pl.BlockSpec(shape, imap)
    xs = pl.pallas_call(
        _dispatch_kernel,
        grid_spec=pltpu.PrefetchScalarGridSpec(
            num_scalar_prefetch=3,
            grid=(nt,),
            in_specs=[
                tile_spec((MOE_TM, d), lambda i, b, c, t: (i, 0)),
                tile_spec((N_EXPERTS, MOE_TM), lambda i, b, c, t: (0, i)),
                tile_spec((N_EXPERTS, MOE_TM), lambda i, b, c, t: (0, i)),
            ],
            out_specs=pl.BlockSpec(memory_space=pl.ANY),
            scratch_shapes=[
                pltpu.VMEM((2, N_EXPERTS * MOE_HALF, XS_W), BF16),
                pltpu.VMEM((N_EXPERTS, MOE_HALF, XS_W), BF16),
                pltpu.VMEM((MOE_ZR, XS_W), BF16),
                pltpu.SemaphoreType.DMA((2, N_EXPERTS)),
                pltpu.SemaphoreType.DMA((N_EXPERTS,)),
                pltpu.SemaphoreType.DMA(()),
            ],
        ),
        out_shape=jax.ShapeDtypeStruct((rows, XS_W), BF16),
        compiler_params=_params("arbitrary"),
        name="moe_dispatch",
    )(base, n_flat, tail, hn, ct, st)

    ys = pl.pallas_call(
        _expert_kernel,
        grid_spec=pltpu.PrefetchScalarGridSpec(
            num_scalar_prefetch=2,
            grid=(nb,),
            in_specs=[
                pl.BlockSpec((MOE_TR, XS_W), lambda b, be, nu: (b, 0)),
                pl.BlockSpec((None, 1, d, ff), lambda b, be, nu: (moe_layer, be[b], 0, 0)),
                pl.BlockSpec((None, 1, d, ff), lambda b, be, nu: (moe_layer, be[b], 0, 1)),
                pl.BlockSpec((None, 1, ff, d), lambda b, be, nu: (moe_layer, be[b], 0, 0)),
            ],
            out_specs=pl.BlockSpec((MOE_TR, d), lambda b, be, nu: (b, 0)),
            scratch_shapes=[pltpu.VMEM((MOE_TR, d), F32)],
        ),
        out_shape=jax.ShapeDtypeStruct((rows, d), BF16),
        compiler_params=_params("arbitrary"),
        name="moe_experts",
    )(blk_expert, n_used, xs, w_gu, w_gu, w_down)

    out_norm = g_out is not None
    g_out = jnp.ones((d,), F32) if g_out is None else g_out
    return pl.pallas_call(
        functools.partial(_combine_kernel, out_norm=out_norm),
        grid_spec=pltpu.PrefetchScalarGridSpec(
            num_scalar_prefetch=2,
            grid=(nt,),
            in_specs=[
                tile_spec((MOE_TM, d), lambda i, b, c: (i, 0)),
                tile_spec((N_EXPERTS, MOE_TM), lambda i, b, c: (0, i)),
                pl.BlockSpec(memory_space=pl.ANY),
                tile_spec((1, d), lambda i, b, c: (0, 0)),
            ],
            out_specs=tile_spec((MOE_TM, d), lambda i, b, c: (i, 0)),
            scratch_shapes=[
                pltpu.VMEM((2, N_EXPERTS * MOE_HALF, d), BF16),
                pltpu.VMEM((N_EXPERTS, MOE_HALF, d), BF16),
                pltpu.SemaphoreType.DMA((2, N_EXPERTS)),
                pltpu.SemaphoreType.DMA((N_EXPERTS,)),
            ],
        ),
        out_shape=jax.ShapeDtypeStruct((m, d), F32),
        compiler_params=_params("arbitrary"),
        name="moe_combine",
    )(base, n_flat, x, st, ys, g_out.reshape(1, d).astype(F32))


def kernel(x, mem, norm_mix, norm_mem, norm_ffn, mem_norm, final_norm, s5_w_in, s5_lam_re, s5_lam_im, s5_log_dt, s5_b_re, s5_b_im, s5_c_re, s5_c_im, s5_d, s5_w_glu, att_w_qkv, att_w_o, att_rel_bias, lru_w_in, lru_conv_w, lru_conv_b, lru_w_a, lru_b_a, lru_w_x, lru_b_x, lru_lam, lru_w_out, mem_w_q, mem_w_kv, mem_w_o, ffn_w_gu, ffn_w_down, moe_w_router, moe_w_gu, moe_w_down):
    bsz, seq, d = x.shape
    x = x.reshape(bsz * seq, d).astype(F32)

    w_kv_all = jnp.concatenate([mem_w_kv[i] for i in range(DEPTH)], axis=1).astype(BF16)
    kv_all = _norm_linear(mem.reshape(bsz * N_MEM, d).astype(F32), mem_norm, w_kv_all, BF16,
                          tm=bsz * N_MEM, tn=2 * d)

    mem_wq, mem_wo = mem_w_q.astype(BF16), mem_w_o.astype(BF16)
    ffn_wgu, ffn_wd = ffn_w_gu.astype(BF16), ffn_w_down.astype(BF16)
    moe_wgu, moe_wd = moe_w_gu.astype(BF16), moe_w_down.astype(BF16)
    for i in range(DEPTH):
        kind, j = i % 3, i // 3
        if kind == 0:
            x = _s5_layer(x, norm_mix[i], s5_w_in[j], s5_lam_re[j], s5_lam_im[j], s5_log_dt[j],
                          s5_b_re[j], s5_b_im[j], s5_c_re[j], s5_c_im[j], s5_d[j], s5_w_glu[j],
                          bsz, seq)
        elif kind == 1:
            x = _attn_layer(x, norm_mix[i], att_w_qkv[j], att_w_o[j], att_rel_bias[j], bsz, seq)
        else:
            x = _lru_layer(x, norm_mix[i], lru_w_in[j], lru_conv_w[j], lru_conv_b[j], lru_w_a[j],
                           lru_b_a[j], lru_w_x[j], lru_b_x[j], lru_lam[j], lru_w_out[j], bsz, seq)
        if i % 2 == 0:
            x = _mem_ffn_layer(x, norm_mem[i], mem_wq, kv_all, i, mem_wo, norm_ffn[i], ffn_wgu, ffn_wd,
                               i // 2, bsz, seq)
        else:
            x = _moe_layer(x, norm_mem[i], mem_wq, kv_all, i, mem_wo, norm_ffn[i], moe_w_router[i // 2],
                           moe_wgu, moe_wd, i // 2, bsz, seq,
                           g_out=final_norm if i == DEPTH - 1 else None)
    return x.reshape(bsz, seq, d)
```
